```python
import jax, jax.numpy as jnp
from jax import lax
import numpy as np

D_MODEL = 1024
BATCH = 8
SEQ = 2048
DEPTH = 2
DEC_BATCH = 128
DEC_SEQ = 8
PAST_LEN = 16384
PAGE_SIZE = 128

SGU_WIDTH = D_MODEL
SGU_GROUPS = 4
SGU_CHUNK = 128
MLSTM_WIDTH = D_MODEL
MLSTM_HEADS = 4
MLSTM_DK = MLSTM_WIDTH // MLSTM_HEADS
MLSTM_DV = MLSTM_WIDTH // MLSTM_HEADS
MLSTM_CHUNK = 128
POOL_WIDTH = D_MODEL
POOL_WINDOWS = (2, 4, 8, 16)
POOL_GROUPS = 4
POOL_BUF = 15
N_BRANCH = 3
EPS = 1e-6

IN_SIZES = (N_BRANCH * D_MODEL, SGU_WIDTH, SGU_WIDTH, SGU_WIDTH,
            MLSTM_WIDTH, MLSTM_WIDTH, MLSTM_WIDTH, MLSTM_WIDTH, MLSTM_WIDTH,
            MLSTM_HEADS, MLSTM_HEADS, POOL_WIDTH, POOL_WIDTH)
N_IN = sum(IN_SIZES)

kernel_name = "gated_sgu_mlstm_pool_decoder_step"


def rmsnorm(x, g):
    xf = x.astype(jnp.float32)
    y = xf * lax.rsqrt(jnp.mean(xf * xf, axis=-1, keepdims=True) + EPS)
    return (y * g.astype(jnp.float32)).astype(x.dtype)


def layernorm(x, g, b):
    xf = x.astype(jnp.float32)
    mu = jnp.mean(xf, axis=-1, keepdims=True)
    var = jnp.mean(jnp.square(xf - mu), axis=-1, keepdims=True)
    y = (xf - mu) * lax.rsqrt(var + EPS)
    return (y * g.astype(jnp.float32) + b.astype(jnp.float32)).astype(x.dtype)


def head_layernorm(h, g):
    B, T, W = h.shape
    hf = h.astype(jnp.float32).reshape(B, T, MLSTM_HEADS, W // MLSTM_HEADS)
    mu = jnp.mean(hf, axis=-1, keepdims=True)
    var = jnp.mean(jnp.square(hf - mu), axis=-1, keepdims=True)
    y = ((hf - mu) * lax.rsqrt(var + EPS)).reshape(B, T, W)
    return (y * g.astype(jnp.float32)).astype(h.dtype)


def chunk_spatial_gate(v, w_s, b_s):
    B, T, W = v.shape
    L = min(T, SGU_CHUNK)
    n_chunks = T // L
    vr = v.reshape(B, n_chunks, L, SGU_GROUPS, W // SGU_GROUPS)
    w = jnp.tril(w_s[:, :L, :L]).astype(v.dtype)
    s = jnp.einsum('gij,bnjgc->bnigc', w, vr) + b_s[:, :L].T.astype(v.dtype)[None, None, :, :, None]
    return s.reshape(B, T, W)


def mlstm(q, k, v, ig, fg, C0, n0, m0):
    B, T, _ = q.shape
    L = min(T, MLSTM_CHUNK)
    n_chunks = T // L

    def heads(a, d):
        return a.astype(jnp.float32).reshape(B, n_chunks, L, MLSTM_HEADS, d).transpose(1, 0, 3, 2, 4)

    def gates(a):
        return a.reshape(B, n_chunks, L, MLSTM_HEADS).transpose(1, 0, 3, 2)

    qh = heads(q, MLSTM_DK)
    kh = heads(k, MLSTM_DK) * (MLSTM_DK ** -0.5)
    vh = heads(v, MLSTM_DV)
    igh = gates(ig.astype(jnp.float32))
    lfh = gates(jax.nn.log_sigmoid(fg.astype(jnp.float32)))
    causal = jnp.tril(jnp.ones((L, L), dtype=bool))

    def step(carry, inp):
        C, n, m = carry
        qc, kc, vc, ic, lc = inp
        b = jnp.cumsum(lc, axis=-1)
        d_log = b[..., :, None] - b[..., None, :] + ic[..., None, :]
        d_log = jnp.where(causal, d_log, -jnp.inf)
        inter = b + m[..., None]
        m_t = jnp.maximum(inter, jnp.max(d_log, axis=-1))
        w_intra = jnp.exp(d_log - m_t[..., None])
        w_inter = jnp.exp(inter - m_t)
        s = jnp.einsum('bhtd,bhsd->bhts', qc, kc) * w_intra
        num = w_inter[..., None] * jnp.einsum('bhtd,bhde->bhte', qc, C) + jnp.einsum('bhts,bhse->bhte', s, vc)
        den = w_inter * jnp.einsum('bhtd,bhd->bht', qc, n) + jnp.sum(s, axis=-1)
        h = num / jnp.maximum(jnp.abs(den), jnp.exp(-m_t))[..., None]
        b_end = b[..., -1]
        log_end = b_end[..., None] - b + ic
        m_new = jnp.maximum(b_end + m, jnp.max(log_end, axis=-1))
        decay = jnp.exp(b_end + m - m_new)
        w_end = jnp.exp(log_end - m_new[..., None])
        kw = kc * w_end[..., None]
        C_new = decay[..., None, None] * C + jnp.einsum('bhsd,bhse->bhde', kw, vc)
        n_new = decay[..., None] * n + jnp.sum(kw, axis=2)
        return (C_new, n_new, m_new), h

    init = (C0.astype(jnp.float32), n0.astype(jnp.float32), m0.astype(jnp.float32))
    (C, n, m), hs = lax.scan(step, init, (qh, kh, vh, igh, lfh))
    h = hs.transpose(1, 0, 3, 2, 4).reshape(B, T, MLSTM_HEADS * MLSTM_DV)
    return h.astype(q.dtype), C, n, m


def multiscale_pool(p, buf, n_past):
    B, T, W = p.shape
    wg = W // POOL_GROUPS
    full = jnp.concatenate([buf.astype(p.dtype), p], axis=1)
    cs = jnp.cumsum(full.astype(jnp.float32), axis=1)
    cs = jnp.concatenate([jnp.zeros((B, 1, W), jnp.float32), cs], axis=1)
    end = cs[:, POOL_BUF + 1:]
    pos = jnp.arange(T) + n_past
    pf = p.astype(jnp.float32)
    outs = []
    for g, w in enumerate(POOL_WINDOWS):
        sl = slice(g * wg, (g + 1) * wg)
        start = cs[:, POOL_BUF + 1 - w:POOL_BUF + 1 - w + T, sl]
        cnt = jnp.minimum(pos + 1, w).astype(jnp.float32)
        outs.append((end[..., sl] - start) / cnt[None, :, None] - pf[..., sl])
    return jnp.concatenate(outs, axis=-1).astype(p.dtype), full[:, -POOL_BUF:]


def mixer_layer(x, c, st_C, st_n, st_m, pool_buf, n_past,
                w_mod, b_mod, norm_g, w_in, b_if, sgu_ln_g, sgu_ln_b, w_sgu, b_sgu,
                mlstm_norm_g, w_pool, pool_scale, w_br_a, w_br_b, w_br_c, w_out):
    B, T, _ = x.shape
    mod = jax.nn.silu(c) @ w_mod + b_mod
    shift, scale, gate = jnp.split(mod, 3, axis=-1)
    h = rmsnorm(x, norm_g) * (1 + scale[:, None]) + shift[:, None]
    proj = h @ w_in
    split_idx = np.cumsum(IN_SIZES)[:-1].tolist()
    (mg, u, va, za, q, k, vb, ob, zb, ig, fg, p, zc) = jnp.split(proj, split_idx, axis=-1)

    u = jax.nn.gelu(u)
    va = layernorm(jax.nn.gelu(va), sgu_ln_g, sgu_ln_b)
    ya = u * chunk_spatial_gate(va, w_sgu, b_sgu) * jax.nn.silu(za)

    hb, C_new, n_new, m_new = mlstm(q, k, vb, ig + b_if[:MLSTM_HEADS], fg + b_if[MLSTM_HEADS:], st_C, st_n, st_m)
    hb = jax.nn.sigmoid(ob) * hb
    yb = head_layernorm(hb, mlstm_norm_g) * jax.nn.silu(zb)

    pooled, new_buf = multiscale_pool(p, pool_buf, n_past)
    pg = pooled.reshape(B, T, POOL_GROUPS, POOL_WIDTH // POOL_GROUPS)
    pm = jnp.einsum('btgc,gcd->btgd', pg, w_pool).reshape(B, T, POOL_WIDTH)
    yc = pm * pool_scale * jax.nn.silu(zc)

    g_a, g_b, g_c = jnp.split(jax.nn.sigmoid(mg), 3, axis=-1)
    merged = g_a * (ya @ w_br_a) + g_b * (yb @ w_br_b) + g_c * (yc @ w_br_c)
    x = x + gate[:, None] * (merged @ w_out)
    return x, C_new, n_new, m_new, new_buf, va


def setup_inputs(seed: int = 0) -> dict:
    key = jax.random.key(seed)
    ks = jax.random.split(key, 26)
    f32 = jnp.float32

    def nrm(k, shape, s):
        return jax.random.normal(k, shape, f32) * s

    wg = POOL_WIDTH // POOL_GROUPS
    b_if = jnp.concatenate([nrm(ks[9], (DEPTH, MLSTM_HEADS), 0.1),
                            3.0 + nrm(ks[10], (DEPTH, MLSTM_HEADS), 0.1)], axis=-1)
    return {
        "x_prompt": nrm(ks[0], (BATCH, SEQ, D_MODEL), 1.0),
        "x_sample": nrm(ks[1], (DEC_BATCH, DEC_SEQ, D_MODEL), 1.0),
        "c_prompt": nrm(ks[2], (BATCH, D_MODEL), 1.0),
        "c_sample": nrm(ks[3], (DEC_BATCH, D_MODEL), 1.0),
        "state_mlstm_C": nrm(ks[4], (DEPTH, DEC_BATCH, MLSTM_HEADS, MLSTM_DK, MLSTM_DV), 0.1),
        "state_mlstm_n": nrm(ks[5], (DEPTH, DEC_BATCH, MLSTM_HEADS, MLSTM_DK), 0.1),
        "state_mlstm_m": nrm(ks[6], (DEPTH, DEC_BATCH, MLSTM_HEADS), 1.0),
        "state_pool": nrm(ks[7], (DEPTH, DEC_BATCH, POOL_BUF, POOL_WIDTH), 1.0),
        "w_mod": nrm(ks[8], (DEPTH, D_MODEL, 3 * D_MODEL), 0.5 * D_MODEL ** -0.5),
        "b_mod": nrm(ks[11], (DEPTH, 3 * D_MODEL), 0.02),
        "norm_g": 1.0 + nrm(ks[12], (DEPTH, D_MODEL), 0.02),
        "w_in": nrm(ks[13], (DEPTH, D_MODEL, N_IN), D_MODEL ** -0.5),
        "b_if": b_if,
        "sgu_ln_g": 1.0 + nrm(ks[14], (DEPTH, SGU_WIDTH), 0.02),
        "sgu_ln_b": nrm(ks[15], (DEPTH, SGU_WIDTH), 0.02),
        "w_sgu": nrm(ks[16], (DEPTH, SGU_GROUPS, SGU_CHUNK, SGU_CHUNK), 0.5 * SGU_CHUNK ** -0.5),
        "b_sgu": 1.0 + nrm(ks[17], (DEPTH, SGU_GROUPS, SGU_CHUNK), 0.02),
        "mlstm_norm_g": 1.0 + nrm(ks[18], (DEPTH, MLSTM_WIDTH), 0.02),
        "w_pool": nrm(ks[19], (DEPTH, POOL_GROUPS, wg, wg), wg ** -0.5),
        "pool_scale": 1.0 + nrm(ks[20], (DEPTH, POOL_WIDTH), 0.02),
        "w_br_a": nrm(ks[21], (DEPTH, SGU_WIDTH, D_MODEL), SGU_WIDTH ** -0.5),
        "w_br_b": nrm(ks[22], (DEPTH, MLSTM_WIDTH, D_MODEL), MLSTM_WIDTH ** -0.5),
        "w_br_c": nrm(ks[23], (DEPTH, POOL_WIDTH, D_MODEL), POOL_WIDTH ** -0.5),
        "w_out": nrm(ks[24], (DEPTH, D_MODEL, D_MODEL), D_MODEL ** -0.5),
        "final_norm_g": 1.0 + nrm(ks[25], (D_MODEL,), 0.02),
    }


def reference(x_prompt, x_sample, c_prompt, c_sample, state_mlstm_C, state_mlstm_n, state_mlstm_m, state_pool,
              w_mod, b_mod, norm_g, w_in, b_if, sgu_ln_g, sgu_ln_b, w_sgu, b_sgu, mlstm_norm_g,
              w_pool, pool_scale, w_br_a, w_br_b, w_br_c, w_out, final_norm_g):
    B = x_prompt.shape[0]
    xp, xs = x_prompt, x_sample
    Cp_l, np_l, mp_l, bp_l = [], [], [], []
    Cs_l, ns_l, ms_l, bs_l, vs_l = [], [], [], [], []
    for l in range(DEPTH):
        lw = (w_mod[l], b_mod[l], norm_g[l], w_in[l], b_if[l], sgu_ln_g[l], sgu_ln_b[l], w_sgu[l], b_sgu[l],
              mlstm_norm_g[l], w_pool[l], pool_scale[l], w_br_a[l], w_br_b[l], w_br_c[l], w_out[l])
        C0 = jnp.zeros((B, MLSTM_HEADS, MLSTM_DK, MLSTM_DV), jnp.float32)
        n0 = jnp.zeros((B, MLSTM_HEADS, MLSTM_DK), jnp.float32)
        m0 = jnp.zeros((B, MLSTM_HEADS), jnp.float32)
        buf0 = jnp.zeros((B, POOL_BUF, POOL_WIDTH), xp.dtype)
        xp, Cp, np_, mp, bp, _ = mixer_layer(xp, c_prompt, C0, n0, m0, buf0, 0, *lw)
        xs, Cs, ns, ms, bs, vs = mixer_layer(xs, c_sample, state_mlstm_C[l], state_mlstm_n[l], state_mlstm_m[l],
                                             state_pool[l], PAST_LEN, *lw)
        Cp_l.append(Cp); np_l.append(np_); mp_l.append(mp); bp_l.append(bp)
        Cs_l.append(Cs); ns_l.append(ns); ms_l.append(ms); bs_l.append(bs); vs_l.append(vs)
    y_prompt = rmsnorm(xp, final_norm_g)
    y_sample = rmsnorm(xs, final_norm_g)
    return (y_prompt, y_sample,
            jnp.stack(Cp_l), jnp.stack(np_l), jnp.stack(mp_l), jnp.stack(bp_l),
            jnp.stack(Cs_l), jnp.stack(ns_l), jnp.stack(ms_l), jnp.stack(bs_l), jnp.stack(vs_l))
```

```python
import functools

import jax
import jax.numpy as jnp
from jax import lax
from jax.experimental import pallas as pl
from jax.experimental.pallas import tpu as pltpu

F32 = jnp.float32
BF16 = jnp.bfloat16

D_MODEL = 1024
N_HEADS = 4
HEAD_W = D_MODEL // N_HEADS
CHUNK = 128
POOL_WINDOWS = (2, 4, 8, 16)
POOL_BUF = 15
POOL_PAD = 16
PAST_LEN = 16384
EPS = 1e-6
N_SEG = 13
SEG_GA, SEG_GB, SEG_GC, SEG_U, SEG_VA, SEG_ZA, SEG_Q, SEG_VB, SEG_OB, SEG_ZB, SEG_P, SEG_ZC, SEG_K = range(N_SEG)
VMEM_LIMIT_BYTES = 56 * 1024 * 1024


def _params(*semantics):
    return pltpu.CompilerParams(dimension_semantics=semantics, vmem_limit_bytes=VMEM_LIMIT_BYTES)


def _dot(a, b):
    return jnp.dot(a, b, preferred_element_type=F32)


def _dot_nt(a, b):
    return lax.dot_general(a, b, (((1,), (1,)), ((), ())), preferred_element_type=F32)


def _sigmoid(x):
    return 1.0 / (1.0 + jnp.exp(-x))


def _silu(x):
    return x * _sigmoid(x)


def _log_sigmoid(x):
    return jnp.minimum(x, 0.0) - jnp.log(1.0 + jnp.exp(-jnp.abs(x)))


def _gelu(x):
    return jax.nn.gelu(x, approximate=True)


def _mod_kernel(c_ref, w_ref, b_ref, o_ref):
    c = c_ref[...]
    o_ref[0] = _dot(_silu(c).astype(BF16), w_ref[0].astype(BF16)) + b_ref[0]


def _mod_call(c_all, w_mod, b_mod):
    depth = w_mod.shape[0]
    rows = c_all.shape[0]
    return pl.pallas_call(
        _mod_kernel,
        grid=(depth, 3),
        in_specs=[pl.BlockSpec((rows, D_MODEL), lambda l, j: (0, 0)),
                  pl.BlockSpec((1, D_MODEL, D_MODEL), lambda l, j: (l, 0, j)),
                  pl.BlockSpec((1, 1, D_MODEL), lambda l, j: (l, 0, j))],
        out_specs=pl.BlockSpec((1, rows, D_MODEL), lambda l, j: (l, 0, j)),
        out_shape=jax.ShapeDtypeStruct((depth, rows, 3 * D_MODEL), F32),
        compiler_params=_params("arbitrary", "arbitrary"),
        name="adaln_mod",
    )(c_all, w_mod, b_mod.reshape(depth, 1, 3 * D_MODEL))


def _inproj_kernel(x_ref, shift_ref, scale_ref, ng_ref, wm_ref, wkt_ref, wgt_ref, bif_ref, lng_ref, lnb_ref,
                   p_ref, kt_ref, g_ref, *rest, per_token, emit_f32):
    if emit_f32:
        va_ref, pf_ref, h_scr = rest
    else:
        (h_scr,) = rest
    j = pl.program_id(1)

    @pl.when(j == 0)
    def _():
        x = x_ref[...]
        ms = jnp.mean(x * x, axis=-1, keepdims=True)
        y = (x * lax.rsqrt(ms + EPS)) * ng_ref[...]
        if per_token:
            sc, sh = scale_ref[...], shift_ref[...]
        else:
            sc, sh = scale_ref[0], shift_ref[0]
        h = (y * (1.0 + sc) + sh).astype(BF16)
        h_scr[...] = h
        g_ref[...] = _dot_nt(wgt_ref[...], h) + bif_ref[...]

    acc = _dot(h_scr[...], wm_ref[...])

    @pl.when((j == SEG_GA) | (j == SEG_GB) | (j == SEG_GC) | (j == SEG_OB))
    def _():
        p_ref[...] = _sigmoid(acc).astype(BF16)

    @pl.when(j == SEG_U)
    def _():
        p_ref[...] = _gelu(acc).astype(BF16)

    @pl.when(j == SEG_VA)
    def _():
        gl = _gelu(acc)
        mu = jnp.mean(gl, axis=-1, keepdims=True)
        ctr = gl - mu
        var = jnp.mean(ctr * ctr, axis=-1, keepdims=True)
        y = ctr * lax.rsqrt(var + EPS) * lng_ref[...] + lnb_ref[...]
        p_ref[...] = y.astype(BF16)
        if emit_f32:
            va_ref[...] = y

    @pl.when((j == SEG_ZA) | (j == SEG_ZB) | (j == SEG_ZC))
    def _():
        p_ref[...] = _silu(acc).astype(BF16)

    @pl.when((j == SEG_Q) | (j == SEG_VB))
    def _():
        p_ref[...] = acc.astype(BF16)

    @pl.when(j == SEG_P)
    def _():
        p_ref[...] = acc.astype(BF16)
        if emit_f32:
            pf_ref[...] = acc

    @pl.when(j == SEG_K)
    def _():
        k_scale = HEAD_W ** -0.5
        p_ref[...] = (acc * k_scale).astype(BF16)
        kt_ref[...] = (_dot_nt(wkt_ref[...], h_scr[...]) * k_scale).astype(BF16)


def _inproj_call(x, shift, scale, norm_g, w_main, w_kt, w_gt, b_if, ln_g, ln_b, *, tm, rows_per_mod, emit_f32):
    n = x.shape[0]
    per_token = rows_per_mod is None
    if per_token:
        mod_spec = pl.BlockSpec((tm, D_MODEL), lambda i, j: (i, 0))
    else:
        tiles_per_mod = rows_per_mod // tm
        mod_spec = pl.BlockSpec((1, 1, D_MODEL), lambda i, j: (i // tiles_per_mod, 0, 0))
    vec_spec = pl.BlockSpec((1, D_MODEL), lambda i, j: (0, 0))
    row_spec = pl.BlockSpec((tm, D_MODEL), lambda i, j: (i, 0))
    out_shape = [jax.ShapeDtypeStruct((n, N_SEG * D_MODEL), BF16),
                 jax.ShapeDtypeStruct((D_MODEL, n), BF16),
                 jax.ShapeDtypeStruct((8, n), F32)]
    out_specs = [pl.BlockSpec((tm, D_MODEL), lambda i, j: (i, j)),
                 pl.BlockSpec((D_MODEL, tm), lambda i, j: (0, i)),
                 pl.BlockSpec((8, tm), lambda i, j: (0, i))]
    if emit_f32:
        out_shape += [jax.ShapeDtypeStruct((n, D_MODEL), F32)] * 2
        out_specs += [row_spec, row_spec]
    return pl.pallas_call(
        functools.partial(_inproj_kernel, per_token=per_token, emit_f32=emit_f32),
        grid=(n // tm, N_SEG),
        in_specs=[row_spec, mod_spec, mod_spec, vec_spec,
                  pl.BlockSpec((D_MODEL, D_MODEL), lambda i, j: (0, j)),
                  pl.BlockSpec((D_MODEL, D_MODEL), lambda i, j: (0, 0)),
                  pl.BlockSpec((8, D_MODEL), lambda i, j: (0, 0)),
                  pl.BlockSpec((8, 1), lambda i, j: (0, 0)),
                  vec_spec, vec_spec],
        out_specs=out_specs,
        out_shape=out_shape,
        scratch_shapes=[pltpu.VMEM((tm, D_MODEL), BF16)],
        compiler_params=_params("arbitrary", "arbitrary"),
        name="inproj",
    )(x, shift, scale, norm_g, w_main, w_kt, w_gt, b_if, ln_g, ln_b)


def _block_causal_masks(block_len):
    ti = lax.broadcasted_iota(jnp.int32, (CHUNK, CHUNK), 0)
    si = lax.broadcasted_iota(jnp.int32, (CHUNK, CHUNK), 1)
    shift = block_len.bit_length() - 1
    same = (ti >> shift) == (si >> shift)
    return same, same & (si <= ti), si == ti, si


def _sgu_rows(gu_ref, va_ref, sza_ref, wsgu_ref, bs_ref, ya_ref, causal, n_rows):
    for g in range(N_HEADS):
        cols = slice(g * HEAD_W, (g + 1) * HEAD_W)
        wg = jnp.where(causal, wsgu_ref[g], 0.0).astype(BF16)
        bias = bs_ref[g]
        for r0 in range(0, n_rows, CHUNK):
            rows = slice(r0, r0 + CHUNK)
            s = _dot(wg, va_ref[rows, cols]) + bias
            ya = gu_ref[rows, cols].astype(F32) * s * sza_ref[rows, cols].astype(F32)
            ya_ref[rows, cols] = ya.astype(BF16)


def _head_out(hh, o_blk, z_blk, mng_blk):
    hb = o_blk.astype(F32) * hh
    mu = jnp.mean(hb, axis=-1, keepdims=True)
    ctr = hb - mu
    var = jnp.mean(ctr * ctr, axis=-1, keepdims=True)
    y = ctr * lax.rsqrt(var + EPS) * mng_blk
    return (y * z_blk.astype(F32)).astype(BF16)


def _row_to_col(row, eye):
    return jnp.sum(jnp.where(eye, row, 0.0), axis=1, keepdims=True)


def _col_to_row(col, eye):
    return jnp.sum(jnp.where(eye, col, 0.0), axis=0, keepdims=True)


def _sgu_pool_prompt_kernel(gu_ref, va_ref, sza_ref, p_ref, szc_ref, wsgu_ref, bs_ref, wpool_ref, pscale_ref,
                            ya_ref, yc_ref, pool_ref, fbuf, *, tc):
    c = pl.program_id(1)

    @pl.when(c == 0)
    def _():
        fbuf[0:POOL_PAD, :] = jnp.zeros((POOL_PAD, D_MODEL), F32)

    @pl.when(c > 0)
    def _():
        fbuf[0:POOL_PAD, :] = fbuf[tc:tc + POOL_PAD, :]

    fbuf[POOL_PAD:, :] = p_ref[...].astype(F32)

    _, causal, _, _ = _block_causal_masks(CHUNK)
    _sgu_rows(gu_ref, va_ref, sza_ref, wsgu_ref, bs_ref, ya_ref, causal, tc)

    pos = c * tc + lax.broadcasted_iota(jnp.int32, (tc, 1), 0)
    for g, w in enumerate(POOL_WINDOWS):
        cols = slice(g * HEAD_W, (g + 1) * HEAD_W)
        cur = fbuf[POOL_PAD:POOL_PAD + tc, cols]
        acc = cur
        for k in range(1, w):
            acc = acc + fbuf[POOL_PAD - k:POOL_PAD - k + tc, cols]
        cnt = jnp.minimum(pos + 1, w).astype(F32)
        pooled = acc / cnt - cur
        pm = _dot(pooled.astype(BF16), wpool_ref[g])
        yc_ref[:, cols] = (pm * pscale_ref[:, cols] * szc_ref[:, cols].astype(F32)).astype(BF16)

    @pl.when(c == pl.num_programs(1) - 1)
    def _():
        pool_ref[0] = fbuf[tc:tc + POOL_PAD, :]


def _sgu_pool_prompt_call(p_act, w_sgu, bs_col, w_pool, pool_scale, *, batch, seq, tc):
    n = batch * seq
    nc = seq // tc

    def seg(s):
        return pl.BlockSpec((tc, D_MODEL), lambda b, c: (b * nc + c, s))

    tok_spec = pl.BlockSpec((tc, D_MODEL), lambda b, c: (b * nc + c, 0))
    return pl.pallas_call(
        functools.partial(_sgu_pool_prompt_kernel, tc=tc),
        grid=(batch, nc),
        in_specs=[seg(SEG_U), seg(SEG_VA), seg(SEG_ZA), seg(SEG_P), seg(SEG_ZC),
                  pl.BlockSpec((N_HEADS, CHUNK, CHUNK), lambda b, c: (0, 0, 0)),
                  pl.BlockSpec((N_HEADS, CHUNK, 1), lambda b, c: (0, 0, 0)),
                  pl.BlockSpec((N_HEADS, HEAD_W, HEAD_W), lambda b, c: (0, 0, 0)),
                  pl.BlockSpec((1, D_MODEL), lambda b, c: (0, 0))],
        out_specs=[tok_spec, tok_spec, pl.BlockSpec((1, POOL_PAD, D_MODEL), lambda b, c: (b, 0, 0))],
        out_shape=[jax.ShapeDtypeStruct((n, D_MODEL), BF16), jax.ShapeDtypeStruct((n, D_MODEL), BF16),
                   jax.ShapeDtypeStruct((batch, POOL_PAD, D_MODEL), F32)],
        scratch_shapes=[pltpu.VMEM((tc + POOL_PAD, D_MODEL), F32)],
        compiler_params=_params("arbitrary", "arbitrary"),
        name="sgu_pool_prompt",
    )(p_act, p_act, p_act, p_act, p_act, w_sgu, bs_col, w_pool, pool_scale)


def _sgu_pool_sample_kernel(gu_ref, va_ref, sza_ref, szc_ref, pf_ref, buf_ref, wsgu_ref, bs_ref, wpool_ref,
                            pscale_ref, ya_ref, yc_ref, nbuf_ref, fscr, *, tb, ts, n_past):
    rows = tb * ts
    _, causal, _, _ = _block_causal_masks(ts)
    _sgu_rows(gu_ref, va_ref, sza_ref, wsgu_ref, bs_ref, ya_ref, causal, rows)

    fscr[:, 0:POOL_PAD, :] = buf_ref[...]
    fscr[:, POOL_PAD:, :] = pf_ref[...]
    pos = n_past + lax.broadcasted_iota(jnp.int32, (1, ts, 1), 1)
    for g, w in enumerate(POOL_WINDOWS):
        cols = slice(g * HEAD_W, (g + 1) * HEAD_W)
        cur = fscr[:, POOL_PAD:POOL_PAD + ts, cols]
        acc = cur
        for k in range(1, w):
            acc = acc + fscr[:, POOL_PAD - k:POOL_PAD - k + ts, cols]
        cnt = jnp.minimum(pos + 1, w).astype(F32)
        pooled = (acc / cnt - cur).reshape(rows, HEAD_W)
        pm = _dot(pooled.astype(BF16), wpool_ref[g])
        yc_ref[:, cols] = (pm * pscale_ref[:, cols] * szc_ref[:, cols].astype(F32)).astype(BF16)
    nbuf_ref[...] = fscr[:, POOL_PAD + ts - POOL_BUF:POOL_PAD + ts, :]


def _sgu_pool_sample_call(p_act, p_f32, buf_pad, w_sgu_tiled, bs_col_tiled, w_pool, pool_scale, *, tb, ts):
    n = p_act.shape[0]
    db = n // ts
    rows = tb * ts

    def seg(s):
        return pl.BlockSpec((rows, D_MODEL), lambda i: (i, s))

    tok_spec = pl.BlockSpec((rows, D_MODEL), lambda i: (i, 0))
    return pl.pallas_call(
        functools.partial(_sgu_pool_sample_kernel, tb=tb, ts=ts, n_past=PAST_LEN),
        grid=(db // tb,),
        in_specs=[seg(SEG_U), seg(SEG_VA), seg(SEG_ZA), seg(SEG_ZC),
                  pl.BlockSpec((tb, ts, D_MODEL), lambda i: (i, 0, 0)),
                  pl.BlockSpec((tb, POOL_PAD, D_MODEL), lambda i: (i, 0, 0)),
                  pl.BlockSpec((N_HEADS, CHUNK, CHUNK), lambda i: (0, 0, 0)),
                  pl.BlockSpec((N_HEADS, CHUNK, 1), lambda i: (0, 0, 0)),
                  pl.BlockSpec((N_HEADS, HEAD_W, HEAD_W), lambda i: (0, 0, 0)),
                  pl.BlockSpec((1, D_MODEL), lambda i: (0, 0))],
        out_specs=[tok_spec, tok_spec, pl.BlockSpec((tb, POOL_BUF, D_MODEL), lambda i: (i, 0, 0))],
        out_shape=[jax.ShapeDtypeStruct((n, D_MODEL), BF16), jax.ShapeDtypeStruct((n, D_MODEL), BF16),
                   jax.ShapeDtypeStruct((db, POOL_BUF, D_MODEL), F32)],
        scratch_shapes=[pltpu.VMEM((tb, POOL_PAD + ts, D_MODEL), F32)],
        compiler_params=_params("arbitrary"),
        name="sgu_pool_sample",
    )(p_act, p_act, p_act, p_act, p_f32.reshape(db, ts, D_MODEL), buf_pad, w_sgu_tiled, bs_col_tiled, w_pool,
      pool_scale)


def _mlstm_prompt_kernel(q_ref, v_ref, o_ref, z_ref, k_ref, kt_ref, g_ref, mng_ref,
                         yb_ref, c_out, n_out, m_out, c_scr, n_scr, m_scr):
    c = pl.program_id(1)

    @pl.when(c == 0)
    def _():
        c_scr[...] = jnp.zeros(c_scr.shape, F32)
        n_scr[...] = jnp.zeros(n_scr.shape, F32)
        m_scr[...] = jnp.zeros(m_scr.shape, F32)

    _, causal, eye, _ = _block_causal_masks(CHUNK)
    gates = g_ref[...]
    for h in range(N_HEADS):
        cols = slice(h * HEAD_W, (h + 1) * HEAD_W)
        ig_row = gates[h:h + 1, :]
        lf_row = _log_sigmoid(gates[N_HEADS + h:N_HEADS + h + 1, :])
        b_col = jnp.sum(jnp.where(causal, lf_row, 0.0), axis=1, keepdims=True)
        b_row = _col_to_row(b_col, eye)
        a_row = ig_row - b_row
        m_prev = m_scr[h:h + 1, 0:1]
        d_log = jnp.where(causal, b_col + a_row, -jnp.inf)
        inter = b_col + m_prev
        m_t = jnp.maximum(inter, jnp.max(d_log, axis=1, keepdims=True))
        w_intra = jnp.exp(d_log - m_t)
        w_inter = jnp.exp(inter - m_t)
        qh = q_ref[:, cols]
        vh = v_ref[:, cols]
        kth = kt_ref[cols, :]
        s = _dot(qh, kth) * w_intra
        num = w_inter * _dot(qh, c_scr[h].astype(BF16)) + _dot(s.astype(BF16), vh)
        qn = jnp.sum(qh.astype(F32) * n_scr[h:h + 1, :], axis=1, keepdims=True)
        den = w_inter * qn + jnp.sum(s, axis=1, keepdims=True)
        hh = num / jnp.maximum(jnp.abs(den), jnp.exp(-m_t))
        yb_ref[:, cols] = _head_out(hh, o_ref[:, cols], z_ref[:, cols], mng_ref[:, cols])

        b_end = b_col[CHUNK - 1:CHUNK, :]
        log_end = b_end + a_row
        m_new = jnp.maximum(b_end + m_prev, jnp.max(log_end, axis=1, keepdims=True))
        decay = jnp.exp(b_end + m_prev - m_new)
        w_end_row = jnp.exp(log_end - m_new)
        w_end_col = _row_to_col(w_end_row, eye)
        ktw = (kth.astype(F32) * w_end_row).astype(BF16)
        c_scr[h] = decay * c_scr[h] + _dot(ktw, vh)
        kw = k_ref[:, cols].astype(F32) * w_end_col
        n_scr[h:h + 1, :] = decay * n_scr[h:h + 1, :] + jnp.sum(kw, axis=0, keepdims=True)
        m_scr[h:h + 1, :] = jnp.broadcast_to(m_new, (1, CHUNK))

    @pl.when(c == pl.num_programs(1) - 1)
    def _():
        c_out[0] = c_scr[...]
        n_out[0] = n_scr[0:N_HEADS, :]
        m_out[0] = m_scr[...]


def _mlstm_prompt_call(p_act, k_t, gates, mng, *, batch, seq):
    n = batch * seq
    nc = seq // CHUNK

    def seg(s):
        return pl.BlockSpec((CHUNK, D_MODEL), lambda b, c: (b * nc + c, s))

    return pl.pallas_call(
        _mlstm_prompt_kernel,
        grid=(batch, nc),
        in_specs=[seg(SEG_Q), seg(SEG_VB), seg(SEG_OB), seg(SEG_ZB), seg(SEG_K),
                  pl.BlockSpec((D_MODEL, CHUNK), lambda b, c: (0, b * nc + c)),
                  pl.BlockSpec((8, CHUNK), lambda b, c: (0, b * nc + c)),
                  pl.BlockSpec((1, D_MODEL), lambda b, c: (0, 0))],
        out_specs=[pl.BlockSpec((CHUNK, D_MODEL), lambda b, c: (b * nc + c, 0)),
                   pl.BlockSpec((1, N_HEADS, HEAD_W, HEAD_W), lambda b, c: (b, 0, 0, 0)),
                   pl.BlockSpec((1, N_HEADS, HEAD_W), lambda b, c: (b, 0, 0)),
                   pl.BlockSpec((1, 8, CHUNK), lambda b, c: (b, 0, 0))],
        out_shape=[jax.ShapeDtypeStruct((n, D_MODEL), BF16),
                   jax.ShapeDtypeStruct((batch, N_HEADS, HEAD_W, HEAD_W), F32),
                   jax.ShapeDtypeStruct((batch, N_HEADS, HEAD_W), F32),
                   jax.ShapeDtypeStruct((batch, 8, CHUNK), F32)],
        scratch_shapes=[pltpu.VMEM((N_HEADS, HEAD_W, HEAD_W), F32), pltpu.VMEM((8, HEAD_W), F32),
                        pltpu.VMEM((8, CHUNK), F32)],
        compiler_params=_params("arbitrary", "arbitrary"),
        name="mlstm_prompt",
    )(p_act, p_act, p_act, p_act, p_act, k_t, gates, mng)


def _mlstm_sample_kernel(q_ref, v_ref, o_ref, z_ref, k_ref, kt_ref, g_ref, c0_ref, n0_ref, m0_ref, mng_ref,
                         yb_ref, c_out, n_out, m_out, *, ts):
    h = pl.program_id(1)
    n_seq = CHUNK // ts
    same, causal, eye, si = _block_causal_masks(ts)
    ig_row = g_ref[pl.ds(h, 1), :]
    lf_row = _log_sigmoid(g_ref[pl.ds(N_HEADS + h, 1), :])
    b_col = jnp.sum(jnp.where(causal, lf_row, 0.0), axis=1, keepdims=True)
    b_row = _col_to_row(b_col, eye)
    a_row = ig_row - b_row
    m_col = m0_ref[0]
    d_log = jnp.where(causal, b_col + a_row, -jnp.inf)
    inter = b_col + m_col
    m_t = jnp.maximum(inter, jnp.max(d_log, axis=1, keepdims=True))
    w_intra = jnp.exp(d_log - m_t)
    w_inter = jnp.exp(inter - m_t)
    q = q_ref[...]
    v = v_ref[...]
    kt = kt_ref[...]
    s = _dot(q, kt) * w_intra
    num_inter = jnp.concatenate(
        [_dot(q[b * ts:(b + 1) * ts, :], c0_ref[b, 0].astype(BF16)) for b in range(n_seq)], axis=0)
    n_tok = jnp.concatenate([jnp.broadcast_to(n0_ref[b, 0], (ts, HEAD_W)) for b in range(n_seq)], axis=0)
    num = w_inter * num_inter + _dot(s.astype(BF16), v)
    qn = jnp.sum(q.astype(F32) * n_tok, axis=1, keepdims=True)
    den = w_inter * qn + jnp.sum(s, axis=1, keepdims=True)
    hh = num / jnp.maximum(jnp.abs(den), jnp.exp(-m_t))
    yb_ref[...] = _head_out(hh, o_ref[...], z_ref[...], mng_ref[...])

    last_in_seq = same & ((si & (ts - 1)) == ts - 1)
    b_end_col = jnp.sum(jnp.where(last_in_seq, b_row, 0.0), axis=1, keepdims=True)
    b_end_row = _col_to_row(b_end_col, eye)
    log_end_row = b_end_row + a_row
    seq_max_col = jnp.max(jnp.where(same, log_end_row, -jnp.inf), axis=1, keepdims=True)
    m_new_col = jnp.maximum(b_end_col + m_col, seq_max_col)
    decay_col = jnp.exp(b_end_col + m_col - m_new_col)
    m_new_row = _col_to_row(m_new_col, eye)
    w_end_row = jnp.exp(log_end_row - m_new_row)
    w_end_col = _row_to_col(w_end_row, eye)
    ktw = kt.astype(F32) * w_end_row
    kw = k_ref[...].astype(F32) * w_end_col
    lane_seq = lax.broadcasted_iota(jnp.int32, (1, CHUNK), 1) >> (ts.bit_length() - 1)
    for b in range(n_seq):
        decay = decay_col[b * ts:b * ts + 1, :]
        ktw_b = jnp.where(lane_seq == b, ktw, 0.0).astype(BF16)
        c_out[b, 0] = decay * c0_ref[b, 0] + _dot(ktw_b, v)
        n_out[b, 0] = decay * n0_ref[b, 0] + jnp.sum(kw[b * ts:(b + 1) * ts, :], axis=0, keepdims=True)
        m_out[b, 0] = jnp.broadcast_to(m_new_col[b * ts:b * ts + 1, :], (1, CHUNK))


def _mlstm_sample_call(p_act, k_t, gates, c0, n0, m0_tok, mng, *, ts):
    n = p_act.shape[0]
    db = n // ts
    n_seq = CHUNK // ts

    def seg(s):
        return pl.BlockSpec((CHUNK, HEAD_W), lambda i, h: (i, s * N_HEADS + h))

    state_c = pl.BlockSpec((n_seq, 1, HEAD_W, HEAD_W), lambda i, h: (i, h, 0, 0))
    state_n = pl.BlockSpec((n_seq, 1, 1, HEAD_W), lambda i, h: (i, h, 0, 0))
    return pl.pallas_call(
        functools.partial(_mlstm_sample_kernel, ts=ts),
        grid=(n // CHUNK, N_HEADS),
        in_specs=[seg(SEG_Q), seg(SEG_VB), seg(SEG_OB), seg(SEG_ZB), seg(SEG_K),
                  pl.BlockSpec((HEAD_W, CHUNK), lambda i, h: (h, i)),
                  pl.BlockSpec((8, CHUNK), lambda i, h: (0, i)),
                  state_c, state_n,
                  pl.BlockSpec((1, CHUNK, 1), lambda i, h: (h, i, 0)),
                  pl.BlockSpec((1, HEAD_W), lambda i, h: (0, h))],
        out_specs=[pl.BlockSpec((CHUNK, HEAD_W), lambda i, h: (i, h)),
                   state_c, state_n,
                   pl.BlockSpec((n_seq, 1, 1, CHUNK), lambda i, h: (i, h, 0, 0))],
        out_shape=[jax.ShapeDtypeStruct((n, D_MODEL), BF16),
                   jax.ShapeDtypeStruct((db, N_HEADS, HEAD_W, HEAD_W), F32),
                   jax.ShapeDtypeStruct((db, N_HEADS, 1, HEAD_W), F32),
                   jax.ShapeDtypeStruct((db, N_HEADS, 1, CHUNK), F32)],
        compiler_params=_params("arbitrary", "arbitrary"),
        name="mlstm_sample",
    )(p_act, p_act, p_act, p_act, p_act, k_t, gates, c0, n0, m0_tok, mng)


def _merge_kernel(ya_ref, yb_ref, yc_ref, ga_ref, gb_ref, gc_ref, x_ref, gate_ref, wa_ref, wb_ref, wc_ref,
                  wo_ref, fng_ref, o_ref, *, per_token, final):
    merged = (ga_ref[...].astype(F32) * _dot(ya_ref[...], wa_ref[...])
              + gb_ref[...].astype(F32) * _dot(yb_ref[...], wb_ref[...])
              + gc_ref[...].astype(F32) * _dot(yc_ref[...], wc_ref[...]))
    gate = gate_ref[...] if per_token else gate_ref[0]
    xn = x_ref[...] + gate * _dot(merged.astype(BF16), wo_ref[...])
    if final:
        ms = jnp.mean(xn * xn, axis=-1, keepdims=True)
        xn = (xn * lax.rsqrt(ms + EPS)) * fng_ref[...]
    o_ref[...] = xn


def _merge_call(ya, yb, yc, p_act, x, gate, wa, wb, wc, wo, fng, *, tm, rows_per_mod, final):
    n = x.shape[0]
    per_token = rows_per_mod is None
    tok_spec = pl.BlockSpec((tm, D_MODEL), lambda i: (i, 0))
    if per_token:
        gate_spec = tok_spec
    else:
        tiles_per_mod = rows_per_mod // tm
        gate_spec = pl.BlockSpec((1, 1, D_MODEL), lambda i: (i // tiles_per_mod, 0, 0))

    def seg(s):
        return pl.BlockSpec((tm, D_MODEL), lambda i: (i, s))

    w_spec = pl.BlockSpec((D_MODEL, D_MODEL), lambda i: (0, 0))
    return pl.pallas_call(
        functools.partial(_merge_kernel, per_token=per_token, final=final),
        grid=(n // tm,),
        in_specs=[tok_spec, tok_spec, tok_spec, seg(SEG_GA), seg(SEG_GB), seg(SEG_GC), tok_spec, gate_spec,
                  w_spec, w_spec, w_spec, w_spec, pl.BlockSpec((1, D_MODEL), lambda i: (0, 0))],
        out_specs=tok_spec,
        out_shape=jax.ShapeDtypeStruct((n, D_MODEL), F32),
        compiler_params=_params("arbitrary"),
        name="merge_out",
    )(ya, yb, yc, p_act, p_act, p_act, x, gate, wa, wb, wc, wo, fng)


def _row_tile(n, target):
    tile = min(n, target)
    assert n % tile == 0 and tile % CHUNK == 0, (n, tile)
    return tile


def kernel(x_prompt, x_sample, c_prompt, c_sample, state_mlstm_C, state_mlstm_n, state_mlstm_m, state_pool, w_mod, b_mod, norm_g, w_in, b_if, sgu_ln_g, sgu_ln_b, w_sgu, b_sgu, mlstm_norm_g, w_pool, pool_scale, w_br_a, w_br_b, w_br_c, w_out, final_norm_g):
    batch, seq, _ = x_prompt.shape
    db, ts, _ = x_sample.shape
    depth = w_mod.shape[0]
    assert seq % CHUNK == 0 and CHUNK % ts == 0 and ts & (ts - 1) == 0 and (db * ts) % CHUNK == 0
    n_p, n_s = batch * seq, db * ts

    n_mod = batch + db
    c_all = jnp.concatenate([c_prompt, c_sample, jnp.zeros((-n_mod % 8, D_MODEL), F32)], axis=0)
    mod = _mod_call(c_all, w_mod, b_mod)

    xp = x_prompt.reshape(n_p, D_MODEL)
    xs = x_sample.reshape(n_s, D_MODEL)
    tm_p = _row_tile(seq, 1024)
    tm_s = _row_tile(n_s, 512)
    tc_p = _row_tile(seq, 512)
    tb_s = min(db, 32)
    fng = final_norm_g.reshape(1, D_MODEL)
    n_rep = CHUNK // ts

    outs = {k: [] for k in ("cp", "np", "mp", "bp", "cs", "ns", "ms", "bs", "vs")}
    for l in range(depth):
        wl = w_in[l]
        w_main = jnp.concatenate([wl[:, :7 * D_MODEL], wl[:, 8 * D_MODEL:11 * D_MODEL],
                                  wl[:, 11 * D_MODEL + 8:], wl[:, 7 * D_MODEL:8 * D_MODEL]], axis=1).astype(BF16)
        w_kt = wl[:, 7 * D_MODEL:8 * D_MODEL].T.astype(BF16)
        w_gt = wl[:, 11 * D_MODEL:11 * D_MODEL + 8].T.astype(BF16)
        bif = b_if[l].reshape(8, 1)
        ng = norm_g[l].reshape(1, D_MODEL)
        lng = sgu_ln_g[l].reshape(1, D_MODEL)
        lnb = sgu_ln_b[l].reshape(1, D_MODEL)
        mng = mlstm_norm_g[l].reshape(1, D_MODEL)
        pscale = pool_scale[l].reshape(1, D_MODEL)
        wpool = w_pool[l].astype(BF16)
        wa, wb, wc, wo = (w[l].astype(BF16) for w in (w_br_a, w_br_b, w_br_c, w_out))
        final = l == depth - 1

        mod_p = mod[l, :batch].reshape(batch, 1, 3 * D_MODEL)
        shift_p, scale_p, gate_p = (mod_p[..., i * D_MODEL:(i + 1) * D_MODEL] for i in range(3))
        p_act, k_t, gates = _inproj_call(xp, shift_p, scale_p, ng, w_main, w_kt, w_gt, bif, lng, lnb,
                                         tm=tm_p, rows_per_mod=seq, emit_f32=False)
        ya, yc, pool_p = _sgu_pool_prompt_call(p_act, w_sgu[l], b_sgu[l][:, :, None], wpool, pscale,
                                               batch=batch, seq=seq, tc=tc_p)
        yb, c_p, n_p_state, m_p = _mlstm_prompt_call(p_act, k_t, gates, mng, batch=batch, seq=seq)
        xp = _merge_call(ya, yb, yc, p_act, xp, gate_p, wa, wb, wc, wo, fng,
                         tm=_row_tile(seq, 512), rows_per_mod=seq, final=final)
        outs["cp"].append(c_p)
        outs["np"].append(n_p_state)
        outs["mp"].append(m_p[:, :N_HEADS, 0])
        outs["bp"].append(pool_p[:, 1:, :])

        mod_s = jnp.repeat(mod[l, batch:batch + db], ts, axis=0)
        shift_s, scale_s, gate_s = (mod_s[:, i * D_MODEL:(i + 1) * D_MODEL] for i in range(3))
        p_act, k_t, gates, va_f32, p_f32 = _inproj_call(xs, shift_s, scale_s, ng, w_main, w_kt, w_gt, bif, lng, lnb,
                                                        tm=tm_s, rows_per_mod=None, emit_f32=True)
        w_sgu_tiled = jnp.tile(w_sgu[l][:, :ts, :ts], (1, n_rep, n_rep))
        bs_tiled = jnp.tile(b_sgu[l][:, :ts], (1, n_rep))[:, :, None]
        buf_pad = jnp.pad(state_pool[l], ((0, 0), (POOL_PAD - POOL_BUF, 0), (0, 0)))
        ya, yc, pool_s = _sgu_pool_sample_call(p_act, p_f32, buf_pad, w_sgu_tiled, bs_tiled, wpool, pscale,
                                               tb=tb_s, ts=ts)
        m0_tok = jnp.repeat(state_mlstm_m[l], ts, axis=0).T[:, :, None]
        yb, c_s, n_s_state, m_s = _mlstm_sample_call(p_act, k_t, gates, state_mlstm_C[l],
                                                     state_mlstm_n[l][:, :, None, :], m0_tok, mng, ts=ts)
        xs = _merge_call(ya, yb, yc, p_act, xs, gate_s, wa, wb, wc, wo, fng,
                         tm=_row_tile(n_s, 512), rows_per_mod=None, final=final)
        outs["cs"].append(c_s)
        outs["ns"].append(n_s_state[:, :, 0, :])
        outs["ms"].append(m_s[:, :, 0, 0])
        outs["bs"].append(pool_s)
        outs["vs"].append(va_f32.reshape(db, ts, D_MODEL))

    y_prompt = xp.reshape(batch, seq, D_MODEL)
    y_sample = xs.reshape(db, ts, D_MODEL)
    return (y_prompt, y_sample,
            jnp.stack(outs["cp"]), jnp.stack(outs["np"]), jnp.stack(outs["mp"]), jnp.stack(outs["bp"]),
            jnp.stack(outs["cs"]), jnp.stack(outs["ns"]), jnp.stack(outs["ms"]), jnp.stack(outs["bs"]),
            jnp.stack(outs["vs"]))
```

```python
import functools

import jax
import jax.numpy as jnp
from jax import lax
from jax.experimental import pallas as pl
from jax.experimental.pallas import tpu as pltpu

F32 = jnp.float32
BF16 = jnp.bfloat16

D_MODEL = 1024
N_HEADS = 4
HEAD_W = D_MODEL // N_HEADS
CHUNK = 128
POOL_WINDOWS = (2, 4, 8, 16)
POOL_BUF = 15
POOL_PAD = 16
PAST_LEN = 16384
EPS = 1e-6
N_SEG = 13
SEG_GA, SEG_GB, SEG_GC, SEG_U, SEG_VA, SEG_ZA, SEG_Q, SEG_K, SEG_VB, SEG_OB, SEG_ZB, SEG_P, SEG_ZC = range(N_SEG)
N_ALIGNED_SEG = 11
GATE_COL = N_ALIGNED_SEG * D_MODEL
VMEM_LIMIT_BYTES = 56 * 1024 * 1024


def _params(*semantics):
    return pltpu.CompilerParams(dimension_semantics=semantics, vmem_limit_bytes=VMEM_LIMIT_BYTES)


def _dot(a, b):
    return jnp.dot(a, b, preferred_element_type=F32)


def _dot_nt(a, b):
    return lax.dot_general(a, b, (((1,), (1,)), ((), ())), preferred_element_type=F32)


def _dot_tn(a, b):
    return jnp.dot(a.T, b, preferred_element_type=F32)


def _sigmoid(x):
    return 0.5 * jnp.tanh(0.5 * x) + 0.5


def _silu(x):
    return x * _sigmoid(x)


def _log_sigmoid(x):
    return jnp.minimum(x, 0.0) - jnp.log(1.0 + jnp.exp(-jnp.abs(x)))


def _gelu(x):
    return jax.nn.gelu(x, approximate=True)


def _mod_kernel(c_ref, w_ref, b_ref, o_ref):
    c = c_ref[...]
    o_ref[0] = _dot(_silu(c).astype(BF16), w_ref[0].astype(BF16)) + b_ref[0]


def _mod_call(c_all, w_mod, b_mod):
    depth = w_mod.shape[0]
    rows = c_all.shape[0]
    return pl.pallas_call(
        _mod_kernel,
        grid=(depth, 3),
        in_specs=[pl.BlockSpec((rows, D_MODEL), lambda l, j: (0, 0)),
                  pl.BlockSpec((1, D_MODEL, D_MODEL), lambda l, j: (l, 0, j)),
                  pl.BlockSpec((1, 1, D_MODEL), lambda l, j: (l, 0, j))],
        out_specs=pl.BlockSpec((1, rows, D_MODEL), lambda l, j: (l, 0, j)),
        out_shape=jax.ShapeDtypeStruct((depth, rows, 3 * D_MODEL), F32),
        compiler_params=_params("arbitrary", "arbitrary"),
        name="adaln_mod",
    )(c_all, w_mod, b_mod.reshape(depth, 1, 3 * D_MODEL))


def _wcast_kernel(w_ref, o_ref):
    o_ref[...] = w_ref[...].astype(BF16)


def _wcast_call(w_in):
    depth = w_in.shape[0]
    spec = pl.BlockSpec((1, D_MODEL, D_MODEL), lambda l, j: (l, 0, j))
    return pl.pallas_call(
        _wcast_kernel,
        grid=(depth, N_ALIGNED_SEG),
        in_specs=[spec],
        out_specs=spec,
        out_shape=jax.ShapeDtypeStruct((depth, D_MODEL, GATE_COL), BF16),
        compiler_params=_params("arbitrary", "arbitrary"),
        name="w_in_bf16",
    )(w_in)


def _inproj_kernel(x_ref, shift_ref, scale_ref, ng_ref, w_ref, wpz_ref, wgt_ref, bif_ref, lng_ref, lnb_ref,
                   p_ref, g_ref, *rest, per_token, emit_f32, tm, sub):
    if emit_f32:
        va_ref, pf_ref, h_scr = rest
    else:
        (h_scr,) = rest
    j = pl.program_id(1)

    def normed_rows(rows):
        x = x_ref[rows, :]
        ms = jnp.mean(x * x, axis=-1, keepdims=True)
        y = (x * lax.rsqrt(ms + EPS)) * ng_ref[...]
        if per_token:
            sc, sh = scale_ref[rows, :], shift_ref[rows, :]
        else:
            sc, sh = scale_ref[0], shift_ref[0]
        h = (y * (1.0 + sc) + sh).astype(BF16)
        h_scr[rows, :] = h
        g_ref[:, rows] = _dot_nt(wgt_ref[...], h) + bif_ref[...]
        return h

    def project(weight_ref, epilogue, first=False):
        w = weight_ref[0]
        for r0 in range(0, tm, sub):
            rows = slice(r0, r0 + sub)
            h = normed_rows(rows) if first else h_scr[rows, :]
            epilogue(rows, _dot(h, w))

    def store(fn):
        def epilogue(rows, acc):
            p_ref[rows, :] = fn(acc).astype(BF16)
        return epilogue

    def va_epilogue(rows, acc):
        gl = _gelu(acc)
        mu = jnp.mean(gl, axis=-1, keepdims=True)
        ctr = gl - mu
        var = jnp.mean(ctr * ctr, axis=-1, keepdims=True)
        y = ctr * lax.rsqrt(var + EPS) * lng_ref[...] + lnb_ref[...]
        p_ref[rows, :] = y.astype(BF16)
        if emit_f32:
            va_ref[rows, :] = y

    def p_epilogue(rows, acc):
        p_ref[rows, :] = acc.astype(BF16)
        if emit_f32:
            pf_ref[rows, :] = acc

    pl.when(j == SEG_GA)(lambda: project(w_ref, store(_sigmoid), first=True))
    pl.when((j == SEG_GB) | (j == SEG_GC) | (j == SEG_OB))(lambda: project(w_ref, store(_sigmoid)))
    pl.when(j == SEG_U)(lambda: project(w_ref, store(_gelu)))
    pl.when(j == SEG_VA)(lambda: project(w_ref, va_epilogue))
    pl.when((j == SEG_ZA) | (j == SEG_ZB))(lambda: project(w_ref, store(_silu)))
    pl.when((j == SEG_Q) | (j == SEG_VB))(lambda: project(w_ref, store(lambda a: a)))
    pl.when(j == SEG_K)(lambda: project(w_ref, store(lambda a: a * HEAD_W ** -0.5)))
    pl.when(j == SEG_P)(lambda: project(wpz_ref, p_epilogue))
    pl.when(j == SEG_ZC)(lambda: project(wpz_ref, store(_silu)))


def _inproj_call(x, shift, scale, norm_g, w_bf, w_pz, w_gt, b_if, ln_g, ln_b, *, layer, tm, rows_per_mod, emit_f32):
    n = x.shape[0]
    per_token = rows_per_mod is None
    if per_token:
        mod_spec = pl.BlockSpec((tm, D_MODEL), lambda i, j: (i, 0))
    else:
        tiles_per_mod = rows_per_mod // tm
        mod_spec = pl.BlockSpec((1, 1, D_MODEL), lambda i, j: (i // tiles_per_mod, 0, 0))
    vec_spec = pl.BlockSpec((1, D_MODEL), lambda i, j: (0, 0))
    row_spec = pl.BlockSpec((tm, D_MODEL), lambda i, j: (i, 0))
    out_shape = [jax.ShapeDtypeStruct((n, N_SEG * D_MODEL), BF16), jax.ShapeDtypeStruct((8, n), F32)]
    out_specs = [pl.BlockSpec((tm, D_MODEL), lambda i, j: (i, j)), pl.BlockSpec((8, tm), lambda i, j: (0, i))]
    if emit_f32:
        out_shape += [jax.ShapeDtypeStruct((n, D_MODEL), F32)] * 2
        out_specs += [row_spec, row_spec]
    last_aligned = N_ALIGNED_SEG - 1
    return pl.pallas_call(
        functools.partial(_inproj_kernel, per_token=per_token, emit_f32=emit_f32, tm=tm, sub=min(tm, 256)),
        grid=(n // tm, N_SEG),
        in_specs=[row_spec, mod_spec, mod_spec, vec_spec,
                  pl.BlockSpec((1, D_MODEL, D_MODEL), lambda i, j: (layer, 0, jnp.minimum(j, last_aligned))),
                  pl.BlockSpec((1, D_MODEL, D_MODEL), lambda i, j: (layer, 0, jnp.maximum(j - N_ALIGNED_SEG, 0))),
                  pl.BlockSpec((8, D_MODEL), lambda i, j: (0, 0)),
                  pl.BlockSpec((8, 1), lambda i, j: (0, 0)),
                  vec_spec, vec_spec],
        out_specs=out_specs,
        out_shape=out_shape,
        scratch_shapes=[pltpu.VMEM((tm, D_MODEL), BF16)],
        compiler_params=_params("arbitrary", "arbitrary"),
        name="inproj",
    )(x, shift, scale, norm_g, w_bf, w_pz, w_gt, b_if, ln_g, ln_b)


def _block_causal_masks(block_len):
    ti = lax.broadcasted_iota(jnp.int32, (CHUNK, CHUNK), 0)
    si = lax.broadcasted_iota(jnp.int32, (CHUNK, CHUNK), 1)
    shift = block_len.bit_length() - 1
    same = (ti >> shift) == (si >> shift)
    return same, same & (si <= ti), si == ti, si


def _sgu_rows(gu_ref, va_ref, sza_ref, wsgu_ref, bs_ref, ya_ref, causal, n_rows):
    for g in range(N_HEADS):
        cols = slice(g * HEAD_W, (g + 1) * HEAD_W)
        wg = jnp.where(causal, wsgu_ref[g], 0.0).astype(BF16)
        bias = bs_ref[g]
        for r0 in range(0, n_rows, CHUNK):
            rows = slice(r0, r0 + CHUNK)
            s = _dot(wg, va_ref[rows, cols]) + bias
            ya = gu_ref[rows, cols].astype(F32) * s * sza_ref[rows, cols].astype(F32)
            ya_ref[rows, cols] = ya.astype(BF16)


def _head_out(hh, o_blk, z_blk, mng_blk):
    hb = o_blk.astype(F32) * hh
    mu = jnp.mean(hb, axis=-1, keepdims=True)
    ctr = hb - mu
    var = jnp.mean(ctr * ctr, axis=-1, keepdims=True)
    y = ctr * lax.rsqrt(var + EPS) * mng_blk
    return (y * z_blk.astype(F32)).astype(BF16)


def _row_to_col(row, eye):
    return jnp.sum(jnp.where(eye, row, 0.0), axis=1, keepdims=True)


def _col_to_row(col, eye):
    return jnp.sum(jnp.where(eye, col, 0.0), axis=0, keepdims=True)


def _sgu_pool_prompt_kernel(gu_ref, va_ref, sza_ref, p_ref, szc_ref, wsgu_ref, bs_ref, wpool_ref, pscale_ref,
                            ya_ref, yc_ref, pool_ref, fbuf, *, tc):
    c = pl.program_id(1)

    @pl.when(c == 0)
    def _():
        fbuf[0:POOL_PAD, :] = jnp.zeros((POOL_PAD, D_MODEL), F32)

    @pl.when(c > 0)
    def _():
        fbuf[0:POOL_PAD, :] = fbuf[tc:tc + POOL_PAD, :]

    fbuf[POOL_PAD:, :] = p_ref[...].astype(F32)

    _, causal, _, _ = _block_causal_masks(CHUNK)
    _sgu_rows(gu_ref, va_ref, sza_ref, wsgu_ref, bs_ref, ya_ref, causal, tc)

    pos = c * tc + lax.broadcasted_iota(jnp.int32, (tc, 1), 0)
    for g, w in enumerate(POOL_WINDOWS):
        cols = slice(g * HEAD_W, (g + 1) * HEAD_W)
        cur = fbuf[POOL_PAD:POOL_PAD + tc, cols]
        acc = cur
        for k in range(1, w):
            acc = acc + fbuf[POOL_PAD - k:POOL_PAD - k + tc, cols]
        cnt = jnp.minimum(pos + 1, w).astype(F32)
        pooled = acc / cnt - cur
        pm = _dot(pooled.astype(BF16), wpool_ref[g])
        yc_ref[:, cols] = (pm * pscale_ref[:, cols] * szc_ref[:, cols].astype(F32)).astype(BF16)

    @pl.when(c == pl.num_programs(1) - 1)
    def _():
        pool_ref[0] = fbuf[tc:tc + POOL_PAD, :]


def _sgu_pool_prompt_call(p_act, w_sgu, bs_col, w_pool, pool_scale, *, batch, seq, tc):
    n = batch * seq
    nc = seq // tc

    def seg(s):
        return pl.BlockSpec((tc, D_MODEL), lambda b, c: (b * nc + c, s))

    tok_spec = pl.BlockSpec((tc, D_MODEL), lambda b, c: (b * nc + c, 0))
    return pl.pallas_call(
        functools.partial(_sgu_pool_prompt_kernel, tc=tc),
        grid=(batch, nc),
        in_specs=[seg(SEG_U), seg(SEG_VA), seg(SEG_ZA), seg(SEG_P), seg(SEG_ZC),
                  pl.BlockSpec((N_HEADS, CHUNK, CHUNK), lambda b, c: (0, 0, 0)),
                  pl.BlockSpec((N_HEADS, CHUNK, 1), lambda b, c: (0, 0, 0)),
                  pl.BlockSpec((N_HEADS, HEAD_W, HEAD_W), lambda b, c: (0, 0, 0)),
                  pl.BlockSpec((1, D_MODEL), lambda b, c: (0, 0))],
        out_specs=[tok_spec, tok_spec, pl.BlockSpec((1, POOL_PAD, D_MODEL), lambda b, c: (b, 0, 0))],
        out_shape=[jax.ShapeDtypeStruct((n, D_MODEL), BF16), jax.ShapeDtypeStruct((n, D_MODEL), BF16),
                   jax.ShapeDtypeStruct((batch, POOL_PAD, D_MODEL), F32)],
        scratch_shapes=[pltpu.VMEM((tc + POOL_PAD, D_MODEL), F32)],
        compiler_params=_params("arbitrary", "arbitrary"),
        name="sgu_pool_prompt",
    )(p_act, p_act, p_act, p_act, p_act, w_sgu, bs_col, w_pool, pool_scale)


def _sgu_pool_sample_kernel(gu_ref, va_ref, sza_ref, szc_ref, pf_ref, buf_ref, wsgu_ref, bs_ref, wpool_ref,
                            pscale_ref, ya_ref, yc_ref, nbuf_ref, fscr, *, tb, ts, n_past):
    rows = tb * ts
    _, causal, _, _ = _block_causal_masks(ts)
    _sgu_rows(gu_ref, va_ref, sza_ref, wsgu_ref, bs_ref, ya_ref, causal, rows)

    fscr[:, 0:POOL_PAD, :] = buf_ref[...]
    fscr[:, POOL_PAD:, :] = pf_ref[...]
    pos = n_past + lax.broadcasted_iota(jnp.int32, (1, ts, 1), 1)
    for g, w in enumerate(POOL_WINDOWS):
        cols = slice(g * HEAD_W, (g + 1) * HEAD_W)
        cur = fscr[:, POOL_PAD:POOL_PAD + ts, cols]
        acc = cur
        for k in range(1, w):
            acc = acc + fscr[:, POOL_PAD - k:POOL_PAD - k + ts, cols]
        cnt = jnp.minimum(pos + 1, w).astype(F32)
        pooled = (acc / cnt - cur).reshape(rows, HEAD_W)
        pm = _dot(pooled.astype(BF16), wpool_ref[g])
        yc_ref[:, cols] = (pm * pscale_ref[:, cols] * szc_ref[:, cols].astype(F32)).astype(BF16)
    nbuf_ref[...] = fscr[:, POOL_PAD + ts - POOL_BUF:POOL_PAD + ts, :]


def _sgu_pool_sample_call(p_act, p_f32, buf_pad, w_sgu_tiled, bs_col_tiled, w_pool, pool_scale, *, tb, ts):
    n = p_act.shape[0]
    db = n // ts
    rows = tb * ts

    def seg(s):
        return pl.BlockSpec((rows, D_MODEL), lambda i: (i, s))

    tok_spec = pl.BlockSpec((rows, D_MODEL), lambda i: (i, 0))
    return pl.pallas_call(
        functools.partial(_sgu_pool_sample_kernel, tb=tb, ts=ts, n_past=PAST_LEN),
        grid=(db // tb,),
        in_specs=[seg(SEG_U), seg(SEG_VA), seg(SEG_ZA), seg(SEG_ZC),
                  pl.BlockSpec((tb, ts, D_MODEL), lambda i: (i, 0, 0)),
                  pl.BlockSpec((tb, POOL_PAD, D_MODEL), lambda i: (i, 0, 0)),
                  pl.BlockSpec((N_HEADS, CHUNK, CHUNK), lambda i: (0, 0, 0)),
                  pl.BlockSpec((N_HEADS, CHUNK, 1), lambda i: (0, 0, 0)),
                  pl.BlockSpec((N_HEADS, HEAD_W, HEAD_W), lambda i: (0, 0, 0)),
                  pl.BlockSpec((1, D_MODEL), lambda i: (0, 0))],
        out_specs=[tok_spec, tok_spec, pl.BlockSpec((tb, POOL_BUF, D_MODEL), lambda i: (i, 0, 0))],
        out_shape=[jax.ShapeDtypeStruct((n, D_MODEL), BF16), jax.ShapeDtypeStruct((n, D_MODEL), BF16),
                   jax.ShapeDtypeStruct((db, POOL_BUF, D_MODEL), F32)],
        scratch_shapes=[pltpu.VMEM((tb, POOL_PAD + ts, D_MODEL), F32)],
        compiler_params=_params("arbitrary"),
        name="sgu_pool_sample",
    )(p_act, p_act, p_act, p_act, p_f32.reshape(db, ts, D_MODEL), buf_pad, w_sgu_tiled, bs_col_tiled, w_pool,
      pool_scale)


def _mlstm_prompt_kernel(q_ref, k_ref, v_ref, o_ref, z_ref, g_ref, mng_ref,
                         yb_ref, c_out, n_out, m_out, c_scr, n_scr, m_scr, *, n_chunks):
    step = pl.program_id(1)

    @pl.when(step == 0)
    def _():
        c_scr[...] = jnp.zeros(c_scr.shape, F32)
        n_scr[...] = jnp.zeros(n_scr.shape, F32)
        m_scr[...] = jnp.zeros(m_scr.shape, F32)

    _, causal, eye, _ = _block_causal_masks(CHUNK)
    for ci in range(n_chunks):
        rows = slice(ci * CHUNK, (ci + 1) * CHUNK)
        gates = g_ref[:, rows]
        for h in range(N_HEADS):
            cols = slice(h * HEAD_W, (h + 1) * HEAD_W)
            ig_row = gates[h:h + 1, :]
            lf_row = _log_sigmoid(gates[N_HEADS + h:N_HEADS + h + 1, :])
            b_col = jnp.sum(jnp.where(causal, lf_row, 0.0), axis=1, keepdims=True)
            b_row = _col_to_row(b_col, eye)
            a_row = ig_row - b_row
            d_log = jnp.where(causal, b_col + a_row, -jnp.inf)
            r_col = jnp.max(d_log, axis=1, keepdims=True)
            qh = q_ref[rows, cols]
            kh = k_ref[rows, cols]
            vh = v_ref[rows, cols]
            s = _dot_nt(qh, kh) * jnp.exp(d_log - r_col)
            num_intra = _dot(s.astype(BF16), vh)
            den_intra = jnp.sum(s, axis=1, keepdims=True)
            b_end = b_col[CHUNK - 1:CHUNK, :]
            log_end = b_end + a_row
            r_end = jnp.max(log_end, axis=1, keepdims=True)
            w_end_col = _row_to_col(jnp.exp(log_end - r_end), eye)
            kw = kh.astype(F32) * w_end_col
            c_upd = _dot_tn(kw.astype(BF16), vh)
            n_upd = jnp.sum(kw, axis=0, keepdims=True)
            m_prev = m_scr[h:h + 1, 0:1]
            inter = b_col + m_prev
            m_t = jnp.maximum(inter, r_col)
            w_intra = jnp.exp(r_col - m_t)
            w_inter = jnp.exp(inter - m_t)
            num = w_inter * _dot(qh, c_scr[h].astype(BF16)) + w_intra * num_intra
            qn = jnp.sum(qh.astype(F32) * n_scr[h:h + 1, :], axis=1, keepdims=True)
            den = w_inter * qn + w_intra * den_intra
            hh = num / jnp.maximum(jnp.abs(den), jnp.exp(-m_t))
            yb_ref[rows, cols] = _head_out(hh, o_ref[rows, cols], z_ref[rows, cols], mng_ref[:, cols])
            m_new = jnp.maximum(b_end + m_prev, r_end)
            decay = jnp.exp(b_end + m_prev - m_new)
            w_end = jnp.exp(r_end - m_new)
            c_scr[h] = decay * c_scr[h] + w_end * c_upd
            n_scr[h:h + 1, :] = decay * n_scr[h:h + 1, :] + w_end * n_upd
            m_scr[h:h + 1, :] = jnp.broadcast_to(m_new, (1, CHUNK))

    @pl.when(step == pl.num_programs(1) - 1)
    def _():
        c_out[0] = c_scr[...]
        n_out[0] = n_scr[0:N_HEADS, :]
        m_out[0] = m_scr[...]


def _mlstm_prompt_call(p_act, gates, mng, *, batch, seq, n_chunks):
    n = batch * seq
    rows = n_chunks * CHUNK
    steps = seq // rows

    def seg(s):
        return pl.BlockSpec((rows, D_MODEL), lambda b, c: (b * steps + c, s))

    return pl.pallas_call(
        functools.partial(_mlstm_prompt_kernel, n_chunks=n_chunks),
        grid=(batch, steps),
        in_specs=[seg(SEG_Q), seg(SEG_K), seg(SEG_VB), seg(SEG_OB), seg(SEG_ZB),
                  pl.BlockSpec((8, rows), lambda b, c: (0, b * steps + c)),
                  pl.BlockSpec((1, D_MODEL), lambda b, c: (0, 0))],
        out_specs=[pl.BlockSpec((rows, D_MODEL), lambda b, c: (b * steps + c, 0)),
                   pl.BlockSpec((1, N_HEADS, HEAD_W, HEAD_W), lambda b, c: (b, 0, 0, 0)),
                   pl.BlockSpec((1, N_HEADS, HEAD_W), lambda b, c: (b, 0, 0)),
                   pl.BlockSpec((1, 8, CHUNK), lambda b, c: (b, 0, 0))],
        out_shape=[jax.ShapeDtypeStruct((n, D_MODEL), BF16),
                   jax.ShapeDtypeStruct((batch, N_HEADS, HEAD_W, HEAD_W), F32),
                   jax.ShapeDtypeStruct((batch, N_HEADS, HEAD_W), F32),
                   jax.ShapeDtypeStruct((batch, 8, CHUNK), F32)],
        scratch_shapes=[pltpu.VMEM((N_HEADS, HEAD_W, HEAD_W), F32), pltpu.VMEM((8, HEAD_W), F32),
                        pltpu.VMEM((8, CHUNK), F32)],
        compiler_params=_params("arbitrary", "arbitrary"),
        name="mlstm_prompt",
    )(p_act, p_act, p_act, p_act, p_act, gates, mng)


def _mlstm_sample_kernel(q_ref, k_ref, v_ref, o_ref, z_ref, g_ref, c0_ref, n0_ref, m0_ref, mng_ref,
                         yb_ref, c_out, n_out, m_out, *, ts):
    h = pl.program_id(1)
    n_seq = CHUNK // ts
    same, causal, eye, si = _block_causal_masks(ts)
    ig_row = g_ref[pl.ds(h, 1), :]
    lf_row = _log_sigmoid(g_ref[pl.ds(N_HEADS + h, 1), :])
    b_col = jnp.sum(jnp.where(causal, lf_row, 0.0), axis=1, keepdims=True)
    b_row = _col_to_row(b_col, eye)
    a_row = ig_row - b_row
    m_col = m0_ref[0]
    d_log = jnp.where(causal, b_col + a_row, -jnp.inf)
    inter = b_col + m_col
    m_t = jnp.maximum(inter, jnp.max(d_log, axis=1, keepdims=True))
    w_intra = jnp.exp(d_log - m_t)
    w_inter = jnp.exp(inter - m_t)
    q = q_ref[...]
    k = k_ref[...]
    v = v_ref[...]
    s = _dot_nt(q, k) * w_intra
    num_inter = jnp.concatenate(
        [_dot(q[b * ts:(b + 1) * ts, :], c0_ref[b, 0].astype(BF16)) for b in range(n_seq)], axis=0)
    n_tok = jnp.concatenate([jnp.broadcast_to(n0_ref[b, 0], (ts, HEAD_W)) for b in range(n_seq)], axis=0)
    num = w_inter * num_inter + _dot(s.astype(BF16), v)
    qn = jnp.sum(q.astype(F32) * n_tok, axis=1, keepdims=True)
    den = w_inter * qn + jnp.sum(s, axis=1, keepdims=True)
    hh = num / jnp.maximum(jnp.abs(den), jnp.exp(-m_t))
    yb_ref[...] = _head_out(hh, o_ref[...], z_ref[...], mng_ref[...])

    last_in_seq = same & ((si & (ts - 1)) == ts - 1)
    b_end_col = jnp.sum(jnp.where(last_in_seq, b_row, 0.0), axis=1, keepdims=True)
    b_end_row = _col_to_row(b_end_col, eye)
    log_end_row = b_end_row + a_row
    seq_max_col = jnp.max(jnp.where(same, log_end_row, -jnp.inf), axis=1, keepdims=True)
    m_new_col = jnp.maximum(b_end_col + m_col, seq_max_col)
    decay_col = jnp.exp(b_end_col + m_col - m_new_col)
    m_new_row = _col_to_row(m_new_col, eye)
    w_end_col = _row_to_col(jnp.exp(log_end_row - m_new_row), eye)
    kw = k.astype(F32) * w_end_col
    kw_t = kw.T
    lane_seq = lax.broadcasted_iota(jnp.int32, (1, CHUNK), 1) >> (ts.bit_length() - 1)
    for b in range(n_seq):
        decay = decay_col[b * ts:b * ts + 1, :]
        kw_b = jnp.where(lane_seq == b, kw_t, 0.0).astype(BF16)
        c_out[b, 0] = decay * c0_ref[b, 0] + _dot(kw_b, v)
        n_out[b, 0] = decay * n0_ref[b, 0] + jnp.sum(kw[b * ts:(b + 1) * ts, :], axis=0, keepdims=True)
        m_out[b, 0] = jnp.broadcast_to(m_new_col[b * ts:b * ts + 1, :], (1, CHUNK))


def _mlstm_sample_call(p_act, gates, c0, n0, m0_tok, mng, *, ts):
    n = p_act.shape[0]
    db = n // ts
    n_seq = CHUNK // ts

    def seg(s):
        return pl.BlockSpec((CHUNK, HEAD_W), lambda i, h: (i, s * N_HEADS + h))

    state_c = pl.BlockSpec((n_seq, 1, HEAD_W, HEAD_W), lambda i, h: (i, h, 0, 0))
    state_n = pl.BlockSpec((n_seq, 1, 1, HEAD_W), lambda i, h: (i, h, 0, 0))
    return pl.pallas_call(
        functools.partial(_mlstm_sample_kernel, ts=ts),
        grid=(n // CHUNK, N_HEADS),
        in_specs=[seg(SEG_Q), seg(SEG_K), seg(SEG_VB), seg(SEG_OB), seg(SEG_ZB),
                  pl.BlockSpec((8, CHUNK), lambda i, h: (0, i)),
                  state_c, state_n,
                  pl.BlockSpec((1, CHUNK, 1), lambda i, h: (h, i, 0)),
                  pl.BlockSpec((1, HEAD_W), lambda i, h: (0, h))],
        out_specs=[pl.BlockSpec((CHUNK, HEAD_W), lambda i, h: (i, h)),
                   state_c, state_n,
                   pl.BlockSpec((n_seq, 1, 1, CHUNK), lambda i, h: (i, h, 0, 0))],
        out_shape=[jax.ShapeDtypeStruct((n, D_MODEL), BF16),
                   jax.ShapeDtypeStruct((db, N_HEADS, HEAD_W, HEAD_W), F32),
                   jax.ShapeDtypeStruct((db, N_HEADS, 1, HEAD_W), F32),
                   jax.ShapeDtypeStruct((db, N_HEADS, 1, CHUNK), F32)],
        compiler_params=_params("arbitrary", "arbitrary"),
        name="mlstm_sample",
    )(p_act, p_act, p_act, p_act, p_act, gates, c0, n0, m0_tok, mng)


def _merge_kernel(ya_ref, yb_ref, yc_ref, ga_ref, gb_ref, gc_ref, x_ref, gate_ref, wa_ref, wb_ref, wc_ref,
                  wo_ref, fng_ref, o_ref, *, per_token, final):
    merged = (ga_ref[...].astype(F32) * _dot(ya_ref[...], wa_ref[...])
              + gb_ref[...].astype(F32) * _dot(yb_ref[...], wb_ref[...])
              + gc_ref[...].astype(F32) * _dot(yc_ref[...], wc_ref[...]))
    gate = gate_ref[...] if per_token else gate_ref[0]
    xn = x_ref[...] + gate * _dot(merged.astype(BF16), wo_ref[...])
    if final:
        ms = jnp.mean(xn * xn, axis=-1, keepdims=True)
        xn = (xn * lax.rsqrt(ms + EPS)) * fng_ref[...]
    o_ref[...] = xn


def _merge_call(ya, yb, yc, p_act, x, gate, wa, wb, wc, wo, fng, *, tm, rows_per_mod, final):
    n = x.shape[0]
    per_token = rows_per_mod is None
    tok_spec = pl.BlockSpec((tm, D_MODEL), lambda i: (i, 0))
    if per_token:
        gate_spec = tok_spec
    else:
        tiles_per_mod = rows_per_mod // tm
        gate_spec = pl.BlockSpec((1, 1, D_MODEL), lambda i: (i // tiles_per_mod, 0, 0))

    def seg(s):
        return pl.BlockSpec((tm, D_MODEL), lambda i: (i, s))

    w_spec = pl.BlockSpec((D_MODEL, D_MODEL), lambda i: (0, 0))
    return pl.pallas_call(
        functools.partial(_merge_kernel, per_token=per_token, final=final),
        grid=(n // tm,),
        in_specs=[tok_spec, tok_spec, tok_spec, seg(SEG_GA), seg(SEG_GB), seg(SEG_GC), tok_spec, gate_spec,
                  w_spec, w_spec, w_spec, w_spec, pl.BlockSpec((1, D_MODEL), lambda i: (0, 0))],
        out_specs=tok_spec,
        out_shape=jax.ShapeDtypeStruct((n, D_MODEL), F32),
        compiler_params=_params("arbitrary"),
        name="merge_out",
    )(ya, yb, yc, p_act, p_act, p_act, x, gate, wa, wb, wc, wo, fng)


def _row_tile(n, target):
    tile = min(n, target)
    assert n % tile == 0 and tile % CHUNK == 0, (n, tile)
    return tile


def kernel(x_prompt, x_sample, c_prompt, c_sample, state_mlstm_C, state_mlstm_n, state_mlstm_m, state_pool, w_mod, b_mod, norm_g, w_in, b_if, sgu_ln_g, sgu_ln_b, w_sgu, b_sgu, mlstm_norm_g, w_pool, pool_scale, w_br_a, w_br_b, w_br_c, w_out, final_norm_g):
    batch, seq, _ = x_prompt.shape
    db, ts, _ = x_sample.shape
    depth = w_mod.shape[0]
    assert seq % CHUNK == 0 and CHUNK % ts == 0 and ts & (ts - 1) == 0 and (db * ts) % CHUNK == 0
    n_p, n_s = batch * seq, db * ts

    n_mod = batch + db
    c_all = jnp.concatenate([c_prompt, c_sample, jnp.zeros((-n_mod % 8, D_MODEL), F32)], axis=0)
    mod = _mod_call(c_all, w_mod, b_mod)

    w_bf = _wcast_call(w_in)
    w_pz = w_in[:, :, GATE_COL + 8:].astype(BF16)
    w_gt = jnp.swapaxes(w_in[:, :, GATE_COL:GATE_COL + 8], 1, 2).astype(BF16)

    xp = x_prompt.reshape(n_p, D_MODEL)
    xs = x_sample.reshape(n_s, D_MODEL)
    tm_p = _row_tile(seq, 1024)
    tm_s = _row_tile(n_s, 512)
    tc_p = _row_tile(seq, 512)
    mlstm_chunks = min(seq // CHUNK, 4)
    assert (seq // CHUNK) % mlstm_chunks == 0
    tb_s = min(db, 32)
    fng = final_norm_g.reshape(1, D_MODEL)
    n_rep = CHUNK // ts

    outs = {k: [] for k in ("cp", "np", "mp", "bp", "cs", "ns", "ms", "bs", "vs")}
    for l in range(depth):
        bif = b_if[l].reshape(8, 1)
        ng = norm_g[l].reshape(1, D_MODEL)
        lng = sgu_ln_g[l].reshape(1, D_MODEL)
        lnb = sgu_ln_b[l].reshape(1, D_MODEL)
        mng = mlstm_norm_g[l].reshape(1, D_MODEL)
        pscale = pool_scale[l].reshape(1, D_MODEL)
        wpool = w_pool[l].astype(BF16)
        wa, wb, wc, wo = (w[l].astype(BF16) for w in (w_br_a, w_br_b, w_br_c, w_out))
        final = l == depth - 1

        mod_p = mod[l, :batch].reshape(batch, 1, 3 * D_MODEL)
        shift_p, scale_p, gate_p = (mod_p[..., i * D_MODEL:(i + 1) * D_MODEL] for i in range(3))
        p_act, gates = _inproj_call(xp, shift_p, scale_p, ng, w_bf, w_pz, w_gt[l], bif, lng, lnb,
                                    layer=l, tm=tm_p, rows_per_mod=seq, emit_f32=False)
        ya, yc, pool_p = _sgu_pool_prompt_call(p_act, w_sgu[l], b_sgu[l][:, :, None], wpool, pscale,
                                               batch=batch, seq=seq, tc=tc_p)
        yb, c_p, n_p_state, m_p = _mlstm_prompt_call(p_act, gates, mng, batch=batch, seq=seq,
                                                     n_chunks=mlstm_chunks)
        xp = _merge_call(ya, yb, yc, p_act, xp, gate_p, wa, wb, wc, wo, fng,
                         tm=_row_tile(seq, 512), rows_per_mod=seq, final=final)
        outs["cp"].append(c_p)
        outs["np"].append(n_p_state)
        outs["mp"].append(m_p[:, :N_HEADS, 0])
        outs["bp"].append(pool_p[:, 1:, :])

        mod_s = jnp.repeat(mod[l, batch:batch + db], ts, axis=0)
        shift_s, scale_s, gate_s = (mod_s[:, i * D_MODEL:(i + 1) * D_MODEL] for i in range(3))
        p_act, gates, va_f32, p_f32 = _inproj_call(xs, shift_s, scale_s, ng, w_bf, w_pz, w_gt[l], bif, lng, lnb,
                                                   layer=l, tm=tm_s, rows_per_mod=None, emit_f32=True)
        w_sgu_tiled = jnp.tile(w_sgu[l][:, :ts, :ts], (1, n_rep, n_rep))
        bs_tiled = jnp.tile(b_sgu[l][:, :ts], (1, n_rep))[:, :, None]
        buf_pad = jnp.pad(state_pool[l], ((0, 0), (POOL_PAD - POOL_BUF, 0), (0, 0)))
        ya, yc, pool_s = _sgu_pool_sample_call(p_act, p_f32, buf_pad, w_sgu_tiled, bs_tiled, wpool, pscale,
                                               tb=tb_s, ts=ts)
        m0_tok = jnp.repeat(state_mlstm_m[l], ts, axis=0).T[:, :, None]
        yb, c_s, n_s_state, m_s = _mlstm_sample_call(p_act, gates, state_mlstm_C[l],
                                                     state_mlstm_n[l][:, :, None, :], m0_tok, mng, ts=ts)
        xs = _merge_call(ya, yb, yc, p_act, xs, gate_s, wa, wb, wc, wo, fng,
                         tm=_row_tile(n_s, 512), rows_per_mod=None, final=final)
        outs["cs"].append(c_s)
        outs["ns"].append(n_s_state[:, :, 0, :])
        outs["ms"].append(m_s[:, :, 0, 0])
        outs["bs"].append(pool_s)
        outs["vs"].append(va_f32.reshape(db, ts, D_MODEL))

    y_prompt = xp.reshape(batch, seq, D_MODEL)
    y_sample = xs.reshape(db, ts, D_MODEL)
    return (y_prompt, y_sample,
            jnp.stack(outs["cp"]), jnp.stack(outs["np"]), jnp.stack(outs["mp"]), jnp.stack(outs["bp"]),
            jnp.stack(outs["cs"]), jnp.stack(outs["ns"]), jnp.stack(outs["ms"]), jnp.stack(outs["bs"]),
            jnp.stack(outs["vs"]))
```

```python
import functools

import jax
import jax.numpy as jnp
from jax import lax
from jax.experimental import pallas as pl
from jax.experimental.pallas import tpu as pltpu

F32 = jnp.float32
BF16 = jnp.bfloat16

D_MODEL = 1024
N_HEADS = 4
HEAD_W = D_MODEL // N_HEADS
CHUNK = 128
POOL_WINDOWS = (2, 4, 8, 16)
POOL_BUF = 15
POOL_PAD = 16
PAST_LEN = 16384
EPS = 1e-6
N_SEG = 13
SEG_GA, SEG_GB, SEG_GC, SEG_U, SEG_VA, SEG_ZA, SEG_Q, SEG_K, SEG_VB, SEG_OB, SEG_ZB, SEG_P, SEG_ZC = range(N_SEG)
N_ALIGNED_SEG = 11
N_GATE_COLS = 2 * N_HEADS
GATE_COL = N_ALIGNED_SEG * D_MODEL
VMEM_LIMIT_BYTES = 56 * 1024 * 1024


def _params(*semantics):
    return pltpu.CompilerParams(dimension_semantics=semantics, vmem_limit_bytes=VMEM_LIMIT_BYTES)


def _dot(a, b):
    return jnp.dot(a, b, preferred_element_type=F32)


def _dot_nt(a, b):
    return lax.dot_general(a, b, (((1,), (1,)), ((), ())), preferred_element_type=F32)


def _dot_tn(a, b):
    return jnp.dot(a.T, b, preferred_element_type=F32)


def _sigmoid(x):
    return 0.5 * jnp.tanh(0.5 * x) + 0.5


def _silu(x):
    return x * _sigmoid(x)


def _log_sigmoid(x):
    return jnp.minimum(x, 0.0) - jnp.log(1.0 + jnp.exp(-jnp.abs(x)))


def _gelu(x):
    return jax.nn.gelu(x, approximate=True)


def _mod_kernel(c_ref, w_ref, b_ref, o_ref):
    c = c_ref[...]
    o_ref[0] = _dot(_silu(c).astype(BF16), w_ref[0].astype(BF16)) + b_ref[0]


def _mod_call(c_all, w_mod, b_mod):
    depth = w_mod.shape[0]
    rows = c_all.shape[0]
    return pl.pallas_call(
        _mod_kernel,
        grid=(depth, 3),
        in_specs=[pl.BlockSpec((rows, D_MODEL), lambda l, j: (0, 0)),
                  pl.BlockSpec((1, D_MODEL, D_MODEL), lambda l, j: (l, 0, j)),
                  pl.BlockSpec((1, 1, D_MODEL), lambda l, j: (l, 0, j))],
        out_specs=pl.BlockSpec((1, rows, D_MODEL), lambda l, j: (l, 0, j)),
        out_shape=jax.ShapeDtypeStruct((depth, rows, 3 * D_MODEL), F32),
        compiler_params=_params("arbitrary", "arbitrary"),
        name="adaln_mod",
    )(c_all, w_mod, b_mod.reshape(depth, 1, 3 * D_MODEL))


def _wcast_kernel(w_ref, wnext_ref, o_ref):
    j = pl.program_id(1)

    @pl.when(j < N_ALIGNED_SEG)
    def _():
        o_ref[...] = w_ref[...].astype(BF16)

    @pl.when(j >= N_ALIGNED_SEG)
    def _():
        both = jnp.concatenate([w_ref[0], wnext_ref[0]], axis=1)
        o_ref[0] = both[:, N_GATE_COLS:N_GATE_COLS + D_MODEL].astype(BF16)


def _wcast_call(w_in):
    depth = w_in.shape[0]
    lane_blocks = D_MODEL // CHUNK
    spec = pl.BlockSpec((1, D_MODEL, D_MODEL), lambda l, j: (l, 0, j))
    return pl.pallas_call(
        _wcast_kernel,
        grid=(depth, N_SEG),
        in_specs=[spec, pl.BlockSpec((1, D_MODEL, CHUNK), lambda l, j: (l, 0, (j + 1) * lane_blocks))],
        out_specs=spec,
        out_shape=jax.ShapeDtypeStruct((depth, D_MODEL, N_SEG * D_MODEL), BF16),
        compiler_params=_params("arbitrary", "arbitrary"),
        name="w_in_bf16",
    )(w_in, w_in)


def _inproj_kernel(x_ref, shift_ref, scale_ref, ng_ref, w_ref, wgt_ref, bif_ref, lng_ref, lnb_ref,
                   p_ref, g_ref, *rest, per_token, emit_f32, tm, sub):
    if emit_f32:
        va_ref, pf_ref, h_scr = rest
    else:
        (h_scr,) = rest
    j = pl.program_id(1)

    def normed_rows(rows):
        x = x_ref[rows, :]
        ms = jnp.mean(x * x, axis=-1, keepdims=True)
        y = (x * lax.rsqrt(ms + EPS)) * ng_ref[...]
        if per_token:
            sc, sh = scale_ref[rows, :], shift_ref[rows, :]
        else:
            sc, sh = scale_ref[0], shift_ref[0]
        h = (y * (1.0 + sc) + sh).astype(BF16)
        h_scr[rows, :] = h
        g_ref[:, rows] = _dot_nt(wgt_ref[...], h) + bif_ref[...]
        return h

    def project(epilogue, first=False):
        w = w_ref[0]
        for r0 in range(0, tm, sub):
            rows = slice(r0, r0 + sub)
            h = normed_rows(rows) if first else h_scr[rows, :]
            epilogue(rows, _dot(h, w))

    def store(fn):
        def epilogue(rows, acc):
            p_ref[rows, :] = fn(acc).astype(BF16)
        return epilogue

    def va_epilogue(rows, acc):
        gl = _gelu(acc)
        mu = jnp.mean(gl, axis=-1, keepdims=True)
        ctr = gl - mu
        var = jnp.mean(ctr * ctr, axis=-1, keepdims=True)
        y = ctr * lax.rsqrt(var + EPS) * lng_ref[...] + lnb_ref[...]
        p_ref[rows, :] = y.astype(BF16)
        if emit_f32:
            va_ref[rows, :] = y

    def p_epilogue(rows, acc):
        p_ref[rows, :] = acc.astype(BF16)
        if emit_f32:
            pf_ref[rows, :] = acc

    pl.when(j == SEG_GA)(lambda: project(store(_sigmoid), first=True))
    pl.when((j == SEG_GB) | (j == SEG_GC) | (j == SEG_OB))(lambda: project(store(_sigmoid)))
    pl.when(j == SEG_U)(lambda: project(store(_gelu)))
    pl.when(j == SEG_VA)(lambda: project(va_epilogue))
    pl.when((j == SEG_ZA) | (j == SEG_ZB) | (j == SEG_ZC))(lambda: project(store(_silu)))
    pl.when((j == SEG_Q) | (j == SEG_VB))(lambda: project(store(lambda a: a)))
    pl.when(j == SEG_K)(lambda: project(store(lambda a: a * HEAD_W ** -0.5)))
    pl.when(j == SEG_P)(lambda: project(p_epilogue))


def _inproj_call(x, shift, scale, norm_g, w_bf, w_gt, b_if, ln_g, ln_b, *, layer, tm, rows_per_mod, emit_f32):
    n = x.shape[0]
    per_token = rows_per_mod is None
    if per_token:
        mod_spec = pl.BlockSpec((tm, D_MODEL), lambda i, j: (i, 0))
    else:
        tiles_per_mod = rows_per_mod // tm
        mod_spec = pl.BlockSpec((1, 1, D_MODEL), lambda i, j: (i // tiles_per_mod, 0, 0))
    vec_spec = pl.BlockSpec((1, D_MODEL), lambda i, j: (0, 0))
    row_spec = pl.BlockSpec((tm, D_MODEL), lambda i, j: (i, 0))
    out_shape = [jax.ShapeDtypeStruct((n, N_SEG * D_MODEL), BF16), jax.ShapeDtypeStruct((8, n), F32)]
    out_specs = [pl.BlockSpec((tm, D_MODEL), lambda i, j: (i, j)), pl.BlockSpec((8, tm), lambda i, j: (0, i))]
    if emit_f32:
        out_shape += [jax.ShapeDtypeStruct((n, D_MODEL), F32)] * 2
        out_specs += [row_spec, row_spec]
    return pl.pallas_call(
        functools.partial(_inproj_kernel, per_token=per_token, emit_f32=emit_f32, tm=tm, sub=max(tm // 4, CHUNK)),
        grid=(n // tm, N_SEG),
        in_specs=[row_spec, mod_spec, mod_spec, vec_spec,
                  pl.BlockSpec((1, D_MODEL, D_MODEL), lambda i, j: (layer, 0, j)),
                  pl.BlockSpec((8, D_MODEL), lambda i, j: (0, 0)),
                  pl.BlockSpec((8, 1), lambda i, j: (0, 0)),
                  vec_spec, vec_spec],
        out_specs=out_specs,
        out_shape=out_shape,
        scratch_shapes=[pltpu.VMEM((tm, D_MODEL), BF16)],
        compiler_params=_params("arbitrary", "arbitrary"),
        name="inproj",
    )(x, shift, scale, norm_g, w_bf, w_gt, b_if, ln_g, ln_b)


def _block_causal_masks(block_len):
    ti = lax.broadcasted_iota(jnp.int32, (CHUNK, CHUNK), 0)
    si = lax.broadcasted_iota(jnp.int32, (CHUNK, CHUNK), 1)
    shift = block_len.bit_length() - 1
    same = (ti >> shift) == (si >> shift)
    return same, same & (si <= ti), si == ti, si


def _sgu_rows(gu_ref, va_ref, sza_ref, wsgu_ref, bs_ref, ya_ref, causal, n_rows):
    for g in range(N_HEADS):
        cols = slice(g * HEAD_W, (g + 1) * HEAD_W)
        wg = jnp.where(causal, wsgu_ref[g], 0.0).astype(BF16)
        bias = bs_ref[g]
        for r0 in range(0, n_rows, CHUNK):
            rows = slice(r0, r0 + CHUNK)
            s = _dot(wg, va_ref[rows, cols]) + bias
            ya = gu_ref[rows, cols].astype(F32) * s * sza_ref[rows, cols].astype(F32)
            ya_ref[rows, cols] = ya.astype(BF16)


def _head_out(hh, o_blk, z_blk, mng_blk):
    hb = o_blk.astype(F32) * hh
    mu = jnp.mean(hb, axis=-1, keepdims=True)
    ctr = hb - mu
    var = jnp.mean(ctr * ctr, axis=-1, keepdims=True)
    y = ctr * lax.rsqrt(var + EPS) * mng_blk
    return (y * z_blk.astype(F32)).astype(BF16)


def _row_to_col(row, eye):
    return jnp.sum(jnp.where(eye, row, 0.0), axis=1, keepdims=True)


def _col_to_row(col, eye):
    return jnp.sum(jnp.where(eye, col, 0.0), axis=0, keepdims=True)


def _sgu_pool_prompt_kernel(gu_ref, va_ref, sza_ref, p_ref, szc_ref, wsgu_ref, bs_ref, wpool_ref, pscale_ref,
                            ya_ref, yc_ref, pool_ref, fbuf, *, tc):
    c = pl.program_id(1)

    @pl.when(c == 0)
    def _():
        fbuf[0:POOL_PAD, :] = jnp.zeros((POOL_PAD, D_MODEL), F32)

    @pl.when(c > 0)
    def _():
        fbuf[0:POOL_PAD, :] = fbuf[tc:tc + POOL_PAD, :]

    fbuf[POOL_PAD:, :] = p_ref[...].astype(F32)

    _, causal, _, _ = _block_causal_masks(CHUNK)
    _sgu_rows(gu_ref, va_ref, sza_ref, wsgu_ref, bs_ref, ya_ref, causal, tc)

    pos = c * tc + lax.broadcasted_iota(jnp.int32, (tc, 1), 0)
    for g, w in enumerate(POOL_WINDOWS):
        cols = slice(g * HEAD_W, (g + 1) * HEAD_W)
        cur = fbuf[POOL_PAD:POOL_PAD + tc, cols]
        acc = cur
        for k in range(1, w):
            acc = acc + fbuf[POOL_PAD - k:POOL_PAD - k + tc, cols]
        cnt = jnp.minimum(pos + 1, w).astype(F32)
        pooled = acc / cnt - cur
        pm = _dot(pooled.astype(BF16), wpool_ref[g])
        yc_ref[:, cols] = (pm * pscale_ref[:, cols] * szc_ref[:, cols].astype(F32)).astype(BF16)

    @pl.when(c == pl.num_programs(1) - 1)
    def _():
        pool_ref[0] = fbuf[tc:tc + POOL_PAD, :]


def _sgu_pool_prompt_call(p_act, w_sgu, bs_col, w_pool, pool_scale, *, batch, seq, tc):
    n = batch * seq
    nc = seq // tc

    def seg(s):
        return pl.BlockSpec((tc, D_MODEL), lambda b, c: (b * nc + c, s))

    tok_spec = pl.BlockSpec((tc, D_MODEL), lambda b, c: (b * nc + c, 0))
    return pl.pallas_call(
        functools.partial(_sgu_pool_prompt_kernel, tc=tc),
        grid=(batch, nc),
        in_specs=[seg(SEG_U), seg(SEG_VA), seg(SEG_ZA), seg(SEG_P), seg(SEG_ZC),
                  pl.BlockSpec((N_HEADS, CHUNK, CHUNK), lambda b, c: (0, 0, 0)),
                  pl.BlockSpec((N_HEADS, CHUNK, 1), lambda b, c: (0, 0, 0)),
                  pl.BlockSpec((N_HEADS, HEAD_W, HEAD_W), lambda b, c: (0, 0, 0)),
                  pl.BlockSpec((1, D_MODEL), lambda b, c: (0, 0))],
        out_specs=[tok_spec, tok_spec, pl.BlockSpec((1, POOL_PAD, D_MODEL), lambda b, c: (b, 0, 0))],
        out_shape=[jax.ShapeDtypeStruct((n, D_MODEL), BF16), jax.ShapeDtypeStruct((n, D_MODEL), BF16),
                   jax.ShapeDtypeStruct((batch, POOL_PAD, D_MODEL), F32)],
        scratch_shapes=[pltpu.VMEM((tc + POOL_PAD, D_MODEL), F32)],
        compiler_params=_params("arbitrary", "arbitrary"),
        name="sgu_pool_prompt",
    )(p_act, p_act, p_act, p_act, p_act, w_sgu, bs_col, w_pool, pool_scale)


def _sgu_pool_sample_kernel(gu_ref, va_ref, sza_ref, szc_ref, pf_ref, buf_ref, wsgu_ref, bs_ref, wpool_ref,
                            pscale_ref, ya_ref, yc_ref, nbuf_ref, fscr, *, tb, ts, n_past):
    rows = tb * ts
    _, causal, _, _ = _block_causal_masks(ts)
    _sgu_rows(gu_ref, va_ref, sza_ref, wsgu_ref, bs_ref, ya_ref, causal, rows)

    fscr[:, 0:POOL_PAD, :] = buf_ref[...]
    fscr[:, POOL_PAD:, :] = pf_ref[...]
    pos = n_past + lax.broadcasted_iota(jnp.int32, (1, ts, 1), 1)
    for g, w in enumerate(POOL_WINDOWS):
        cols = slice(g * HEAD_W, (g + 1) * HEAD_W)
        cur = fscr[:, POOL_PAD:POOL_PAD + ts, cols]
        acc = cur
        for k in range(1, w):
            acc = acc + fscr[:, POOL_PAD - k:POOL_PAD - k + ts, cols]
        cnt = jnp.minimum(pos + 1, w).astype(F32)
        pooled = (acc / cnt - cur).reshape(rows, HEAD_W)
        pm = _dot(pooled.astype(BF16), wpool_ref[g])
        yc_ref[:, cols] = (pm * pscale_ref[:, cols] * szc_ref[:, cols].astype(F32)).astype(BF16)
    nbuf_ref[...] = fscr[:, POOL_PAD + ts - POOL_BUF:POOL_PAD + ts, :]


def _sgu_pool_sample_call(p_act, p_f32, buf_pad, w_sgu_tiled, bs_col_tiled, w_pool, pool_scale, *, tb, ts):
    n = p_act.shape[0]
    db = n // ts
    rows = tb * ts

    def seg(s):
        return pl.BlockSpec((rows, D_MODEL), lambda i: (i, s))

    tok_spec = pl.BlockSpec((rows, D_MODEL), lambda i: (i, 0))
    return pl.pallas_call(
        functools.partial(_sgu_pool_sample_kernel, tb=tb, ts=ts, n_past=PAST_LEN),
        grid=(db // tb,),
        in_specs=[seg(SEG_U), seg(SEG_VA), seg(SEG_ZA), seg(SEG_ZC),
                  pl.BlockSpec((tb, ts, D_MODEL), lambda i: (i, 0, 0)),
                  pl.BlockSpec((tb, POOL_PAD, D_MODEL), lambda i: (i, 0, 0)),
                  pl.BlockSpec((N_HEADS, CHUNK, CHUNK), lambda i: (0, 0, 0)),
                  pl.BlockSpec((N_HEADS, CHUNK, 1), lambda i: (0, 0, 0)),
                  pl.BlockSpec((N_HEADS, HEAD_W, HEAD_W), lambda i: (0, 0, 0)),
                  pl.BlockSpec((1, D_MODEL), lambda i: (0, 0))],
        out_specs=[tok_spec, tok_spec, pl.BlockSpec((tb, POOL_BUF, D_MODEL), lambda i: (i, 0, 0))],
        out_shape=[jax.ShapeDtypeStruct((n, D_MODEL), BF16), jax.ShapeDtypeStruct((n, D_MODEL), BF16),
                   jax.ShapeDtypeStruct((db, POOL_BUF, D_MODEL), F32)],
        scratch_shapes=[pltpu.VMEM((tb, POOL_PAD + ts, D_MODEL), F32)],
        compiler_params=_params("arbitrary"),
        name="sgu_pool_sample",
    )(p_act, p_act, p_act, p_act, p_f32.reshape(db, ts, D_MODEL), buf_pad, w_sgu_tiled, bs_col_tiled, w_pool,
      pool_scale)


def _mlstm_prompt_kernel(q_ref, k_ref, v_ref, o_ref, z_ref, g_ref, mng_ref,
                         yb_ref, c_out, n_out, m_out, c_scr, n_scr, m_scr, *, n_chunks):
    step = pl.program_id(1)

    @pl.when(step == 0)
    def _():
        c_scr[...] = jnp.zeros(c_scr.shape, F32)
        n_scr[...] = jnp.zeros(n_scr.shape, F32)
        m_scr[...] = jnp.zeros(m_scr.shape, F32)

    _, causal, eye, _ = _block_causal_masks(CHUNK)
    heads = range(N_HEADS)
    probs = [(ci, h) for ci in range(n_chunks) for h in heads]

    def rows(p):
        return slice(p[0] * CHUNK, (p[0] + 1) * CHUNK)

    def cols(p):
        return slice(p[1] * HEAD_W, (p[1] + 1) * HEAD_W)

    def each(fn, keys=probs):
        return {p: fn(p) for p in keys}

    ig_all = g_ref[0:N_HEADS, :]
    lf_all = _log_sigmoid(g_ref[N_HEADS:2 * N_HEADS, :])
    lf_row = each(lambda p: lf_all[p[1]:p[1] + 1, rows(p)])
    b_col = each(lambda p: jnp.sum(jnp.where(causal, lf_row[p], 0.0), axis=1, keepdims=True))
    b_row = each(lambda p: _col_to_row(b_col[p], eye))
    a_row = each(lambda p: ig_all[p[1]:p[1] + 1, rows(p)] - b_row[p])
    d_log = each(lambda p: jnp.where(causal, b_col[p] + a_row[p], -jnp.inf))
    r_col = each(lambda p: jnp.max(d_log[p], axis=1, keepdims=True))
    qk = each(lambda p: _dot_nt(q_ref[rows(p), cols(p)], k_ref[rows(p), cols(p)]))
    s = each(lambda p: qk[p] * jnp.exp(d_log[p] - r_col[p]))
    num_intra = each(lambda p: _dot(s[p].astype(BF16), v_ref[rows(p), cols(p)]))
    den_intra = each(lambda p: jnp.sum(s[p], axis=1, keepdims=True))
    b_end = each(lambda p: b_col[p][CHUNK - 1:CHUNK, :])
    log_end = each(lambda p: b_end[p] + a_row[p])
    r_end = each(lambda p: jnp.max(log_end[p], axis=1, keepdims=True))
    w_end_col = each(lambda p: _row_to_col(jnp.exp(log_end[p] - r_end[p]), eye))
    kw = each(lambda p: k_ref[rows(p), cols(p)].astype(F32) * w_end_col[p])
    c_upd = each(lambda p: _dot_tn(kw[p].astype(BF16), v_ref[rows(p), cols(p)]))
    n_upd = each(lambda p: jnp.sum(kw[p], axis=0, keepdims=True))

    for ci in range(n_chunks):
        ps = [(ci, h) for h in heads]
        m_prev = each(lambda p: m_scr[p[1]:p[1] + 1, 0:1], ps)
        inter = each(lambda p: b_col[p] + m_prev[p], ps)
        m_t = each(lambda p: jnp.maximum(inter[p], r_col[p]), ps)
        w_intra = each(lambda p: jnp.exp(r_col[p] - m_t[p]), ps)
        w_inter = each(lambda p: jnp.exp(inter[p] - m_t[p]), ps)
        qc = each(lambda p: _dot(q_ref[rows(p), cols(p)], c_scr[p[1]].astype(BF16)), ps)
        qn = each(lambda p: jnp.sum(q_ref[rows(p), cols(p)].astype(F32) * n_scr[p[1]:p[1] + 1, :],
                                    axis=1, keepdims=True), ps)
        den = each(lambda p: w_inter[p] * qn[p] + w_intra[p] * den_intra[p], ps)
        inv = each(lambda p: 1.0 / jnp.maximum(jnp.abs(den[p]), jnp.exp(-m_t[p])), ps)
        hh = each(lambda p: (w_inter[p] * qc[p] + w_intra[p] * num_intra[p]) * inv[p], ps)
        for p in ps:
            yb_ref[rows(p), cols(p)] = _head_out(hh[p], o_ref[rows(p), cols(p)], z_ref[rows(p), cols(p)],
                                                 mng_ref[:, cols(p)])
        m_new = each(lambda p: jnp.maximum(b_end[p] + m_prev[p], r_end[p]), ps)
        decay = each(lambda p: jnp.exp(b_end[p] + m_prev[p] - m_new[p]), ps)
        w_end = each(lambda p: jnp.exp(r_end[p] - m_new[p]), ps)
        for p in ps:
            h = p[1]
            c_scr[h] = decay[p] * c_scr[h] + w_end[p] * c_upd[p]
            n_scr[h:h + 1, :] = decay[p] * n_scr[h:h + 1, :] + w_end[p] * n_upd[p]
            m_scr[h:h + 1, :] = jnp.broadcast_to(m_new[p], (1, CHUNK))

    @pl.when(step == pl.num_programs(1) - 1)
    def _():
        c_out[0] = c_scr[...]
        n_out[0] = n_scr[0:N_HEADS, :]
        m_out[0] = m_scr[...]


def _mlstm_prompt_call(p_act, gates, mng, *, batch, seq, n_chunks):
    n = batch * seq
    rows = n_chunks * CHUNK
    steps = seq // rows

    def seg(s):
        return pl.BlockSpec((rows, D_MODEL), lambda b, c: (b * steps + c, s))

    return pl.pallas_call(
        functools.partial(_mlstm_prompt_kernel, n_chunks=n_chunks),
        grid=(batch, steps),
        in_specs=[seg(SEG_Q), seg(SEG_K), seg(SEG_VB), seg(SEG_OB), seg(SEG_ZB),
                  pl.BlockSpec((8, rows), lambda b, c: (0, b * steps + c)),
                  pl.BlockSpec((1, D_MODEL), lambda b, c: (0, 0))],
        out_specs=[pl.BlockSpec((rows, D_MODEL), lambda b, c: (b * steps + c, 0)),
                   pl.BlockSpec((1, N_HEADS, HEAD_W, HEAD_W), lambda b, c: (b, 0, 0, 0)),
                   pl.BlockSpec((1, N_HEADS, HEAD_W), lambda b, c: (b, 0, 0)),
                   pl.BlockSpec((1, 8, CHUNK), lambda b, c: (b, 0, 0))],
        out_shape=[jax.ShapeDtypeStruct((n, D_MODEL), BF16),
                   jax.ShapeDtypeStruct((batch, N_HEADS, HEAD_W, HEAD_W), F32),
                   jax.ShapeDtypeStruct((batch, N_HEADS, HEAD_W), F32),
                   jax.ShapeDtypeStruct((batch, 8, CHUNK), F32)],
        scratch_shapes=[pltpu.VMEM((N_HEADS, HEAD_W, HEAD_W), F32), pltpu.VMEM((8, HEAD_W), F32),
                        pltpu.VMEM((8, CHUNK), F32)],
        compiler_params=_params("arbitrary", "arbitrary"),
        name="mlstm_prompt",
    )(p_act, p_act, p_act, p_act, p_act, gates, mng)


def _mlstm_sample_kernel(q_ref, k_ref, v_ref, o_ref, z_ref, g_ref, c0_ref, n0_ref, m0_ref, mng_ref, *rest, ts):
    yb_ref, c_out, n_out, m_out = rest[-4:]
    h = pl.program_id(1)
    n_seq = CHUNK // ts
    same, causal, eye, si = _block_causal_masks(ts)
    ig_row = g_ref[pl.ds(h, 1), :]
    lf_row = _log_sigmoid(g_ref[pl.ds(N_HEADS + h, 1), :])
    b_col = jnp.sum(jnp.where(causal, lf_row, 0.0), axis=1, keepdims=True)
    b_row = _col_to_row(b_col, eye)
    a_row = ig_row - b_row
    m_col = m0_ref[0]
    d_log = jnp.where(causal, b_col + a_row, -jnp.inf)
    inter = b_col + m_col
    m_t = jnp.maximum(inter, jnp.max(d_log, axis=1, keepdims=True))
    w_intra = jnp.exp(d_log - m_t)
    w_inter = jnp.exp(inter - m_t)
    q = q_ref[...]
    k = k_ref[...]
    v = v_ref[...]
    s = _dot_nt(q, k) * w_intra
    num_inter = jnp.concatenate(
        [_dot(q[b * ts:(b + 1) * ts, :], c0_ref[0, b, 0].astype(BF16)) for b in range(n_seq)], axis=0)
    n_tok = jnp.concatenate([jnp.broadcast_to(n0_ref[b, 0], (ts, HEAD_W)) for b in range(n_seq)], axis=0)
    num = w_inter * num_inter + _dot(s.astype(BF16), v)
    qn = jnp.sum(q.astype(F32) * n_tok, axis=1, keepdims=True)
    den = w_inter * qn + jnp.sum(s, axis=1, keepdims=True)
    hh = num / jnp.maximum(jnp.abs(den), jnp.exp(-m_t))
    yb_ref[...] = _head_out(hh, o_ref[...], z_ref[...], mng_ref[...])

    last_in_seq = same & ((si & (ts - 1)) == ts - 1)
    b_end_col = jnp.sum(jnp.where(last_in_seq, b_row, 0.0), axis=1, keepdims=True)
    b_end_row = _col_to_row(b_end_col, eye)
    log_end_row = b_end_row + a_row
    seq_max_col = jnp.max(jnp.where(same, log_end_row, -jnp.inf), axis=1, keepdims=True)
    m_new_col = jnp.maximum(b_end_col + m_col, seq_max_col)
    decay_col = jnp.exp(b_end_col + m_col - m_new_col)
    m_new_row = _col_to_row(m_new_col, eye)
    w_end_col = _row_to_col(jnp.exp(log_end_row - m_new_row), eye)
    kw = k.astype(F32) * w_end_col
    kw_t = kw.T
    lane_seq = lax.broadcasted_iota(jnp.int32, (1, CHUNK), 1) >> (ts.bit_length() - 1)
    for b in range(n_seq):
        decay = decay_col[b * ts:b * ts + 1, :]
        kw_b = jnp.where(lane_seq == b, kw_t, 0.0).astype(BF16)
        c_out[0, b, 0] = decay * c0_ref[0, b, 0] + _dot(kw_b, v)
        n_out[b, 0] = decay * n0_ref[b, 0] + jnp.sum(kw[b * ts:(b + 1) * ts, :], axis=0, keepdims=True)
        m_out[b, 0] = jnp.broadcast_to(m_new_col[b * ts:b * ts + 1, :], (1, CHUNK))


def _mlstm_sample_call(p_act, gates, c0_all, n0, m0_tok, mng, c_new_all, *, layer, ts):
    n = p_act.shape[0]
    db = n // ts
    n_seq = CHUNK // ts

    def seg(s):
        return pl.BlockSpec((CHUNK, HEAD_W), lambda i, h: (i, s * N_HEADS + h))

    state_c = pl.BlockSpec((1, n_seq, 1, HEAD_W, HEAD_W), lambda i, h: (layer, i, h, 0, 0))
    state_n = pl.BlockSpec((n_seq, 1, 1, HEAD_W), lambda i, h: (i, h, 0, 0))
    in_specs = [seg(SEG_Q), seg(SEG_K), seg(SEG_VB), seg(SEG_OB), seg(SEG_ZB),
                pl.BlockSpec((8, CHUNK), lambda i, h: (0, i)),
                state_c, state_n,
                pl.BlockSpec((1, CHUNK, 1), lambda i, h: (h, i, 0)),
                pl.BlockSpec((1, HEAD_W), lambda i, h: (0, h))]
    args = [p_act, p_act, p_act, p_act, p_act, gates, c0_all, n0, m0_tok, mng]
    aliases = {}
    if c_new_all is not None:
        aliases = {len(args): 1}
        in_specs.append(pl.BlockSpec(memory_space=pl.ANY))
        args.append(c_new_all)
    return pl.pallas_call(
        functools.partial(_mlstm_sample_kernel, ts=ts),
        grid=(n // CHUNK, N_HEADS),
        in_specs=in_specs,
        out_specs=[pl.BlockSpec((CHUNK, HEAD_W), lambda i, h: (i, h)),
                   state_c, state_n,
                   pl.BlockSpec((n_seq, 1, 1, CHUNK), lambda i, h: (i, h, 0, 0))],
        out_shape=[jax.ShapeDtypeStruct((n, D_MODEL), BF16),
                   jax.ShapeDtypeStruct(c0_all.shape, F32),
                   jax.ShapeDtypeStruct((db, N_HEADS, 1, HEAD_W), F32),
                   jax.ShapeDtypeStruct((db, N_HEADS, 1, CHUNK), F32)],
        input_output_aliases=aliases,
        compiler_params=_params("arbitrary", "arbitrary"),
        name="mlstm_sample",
    )(*args)


def _merge_kernel(ya_ref, yb_ref, yc_ref, ga_ref, gb_ref, gc_ref, x_ref, gate_ref, wa_ref, wb_ref, wc_ref,
                  wo_ref, fng_ref, o_ref, *, per_token, final):
    merged = (ga_ref[...].astype(F32) * _dot(ya_ref[...], wa_ref[...])
              + gb_ref[...].astype(F32) * _dot(yb_ref[...], wb_ref[...])
              + gc_ref[...].astype(F32) * _dot(yc_ref[...], wc_ref[...]))
    gate = gate_ref[...] if per_token else gate_ref[0]
    xn = x_ref[...] + gate * _dot(merged.astype(BF16), wo_ref[...])
    if final:
        ms = jnp.mean(xn * xn, axis=-1, keepdims=True)
        xn = (xn * lax.rsqrt(ms + EPS)) * fng_ref[...]
    o_ref[...] = xn


def _merge_call(ya, yb, yc, p_act, x, gate, wa, wb, wc, wo, fng, *, tm, rows_per_mod, final):
    n = x.shape[0]
    per_token = rows_per_mod is None
    tok_spec = pl.BlockSpec((tm, D_MODEL), lambda i: (i, 0))
    if per_token:
        gate_spec = tok_spec
    else:
        tiles_per_mod = rows_per_mod // tm
        gate_spec = pl.BlockSpec((1, 1, D_MODEL), lambda i: (i // tiles_per_mod, 0, 0))

    def seg(s):
        return pl.BlockSpec((tm, D_MODEL), lambda i: (i, s))

    w_spec = pl.BlockSpec((D_MODEL, D_MODEL), lambda i: (0, 0))
    return pl.pallas_call(
        functools.partial(_merge_kernel, per_token=per_token, final=final),
        grid=(n // tm,),
        in_specs=[tok_spec, tok_spec, tok_spec, seg(SEG_GA), seg(SEG_GB), seg(SEG_GC), tok_spec, gate_spec,
                  w_spec, w_spec, w_spec, w_spec, pl.BlockSpec((1, D_MODEL), lambda i: (0, 0))],
        out_specs=tok_spec,
        out_shape=jax.ShapeDtypeStruct((n, D_MODEL), F32),
        compiler_params=_params("arbitrary"),
        name="merge_out",
    )(ya, yb, yc, p_act, p_act, p_act, x, gate, wa, wb, wc, wo, fng)


def _row_tile(n, target):
    tile = min(n, target)
    assert n % tile == 0 and tile % CHUNK == 0, (n, tile)
    return tile


def kernel(x_prompt, x_sample, c_prompt, c_sample, state_mlstm_C, state_mlstm_n, state_mlstm_m, state_pool, w_mod, b_mod, norm_g, w_in, b_if, sgu_ln_g, sgu_ln_b, w_sgu, b_sgu, mlstm_norm_g, w_pool, pool_scale, w_br_a, w_br_b, w_br_c, w_out, final_norm_g):
    batch, seq, _ = x_prompt.shape
    db, ts, _ = x_sample.shape
    depth = w_mod.shape[0]
    assert seq % CHUNK == 0 and CHUNK % ts == 0 and ts & (ts - 1) == 0 and (db * ts) % CHUNK == 0
    n_p, n_s = batch * seq, db * ts

    n_mod = batch + db
    c_all = jnp.concatenate([c_prompt, c_sample, jnp.zeros((-n_mod % 8, D_MODEL), F32)], axis=0)
    mod = _mod_call(c_all, w_mod, b_mod)

    w_bf = _wcast_call(w_in)
    w_gt = jnp.swapaxes(w_in[:, :, GATE_COL:GATE_COL + N_GATE_COLS], 1, 2).astype(BF16)

    xp = x_prompt.reshape(n_p, D_MODEL)
    xs = x_sample.reshape(n_s, D_MODEL)
    tm_p = _row_tile(seq, 2048)
    tm_s = _row_tile(n_s, 512)
    tc_p = _row_tile(seq, 512)
    mlstm_chunks = min(seq // CHUNK, 4)
    assert (seq // CHUNK) % mlstm_chunks == 0
    tb_s = min(db, 32)
    fng = final_norm_g.reshape(1, D_MODEL)
    n_rep = CHUNK // ts

    outs = {k: [] for k in ("cp", "np", "mp", "bp", "ns", "ms", "bs", "vs")}
    c_s_all = None
    for l in range(depth):
        bif = b_if[l].reshape(8, 1)
        ng = norm_g[l].reshape(1, D_MODEL)
        lng = sgu_ln_g[l].reshape(1, D_MODEL)
        lnb = sgu_ln_b[l].reshape(1, D_MODEL)
        mng = mlstm_norm_g[l].reshape(1, D_MODEL)
        pscale = pool_scale[l].reshape(1, D_MODEL)
        wpool = w_pool[l].astype(BF16)
        wa, wb, wc, wo = (w[l].astype(BF16) for w in (w_br_a, w_br_b, w_br_c, w_out))
        final = l == depth - 1

        mod_p = mod[l, :batch].reshape(batch, 1, 3 * D_MODEL)
        shift_p, scale_p, gate_p = (mod_p[..., i * D_MODEL:(i + 1) * D_MODEL] for i in range(3))
        p_act, gates = _inproj_call(xp, shift_p, scale_p, ng, w_bf, w_gt[l], bif, lng, lnb,
                                    layer=l, tm=tm_p, rows_per_mod=seq, emit_f32=False)
        ya, yc, pool_p = _sgu_pool_prompt_call(p_act, w_sgu[l], b_sgu[l][:, :, None], wpool, pscale,
                                               batch=batch, seq=seq, tc=tc_p)
        yb, c_p, n_p_state, m_p = _mlstm_prompt_call(p_act, gates, mng, batch=batch, seq=seq,
                                                     n_chunks=mlstm_chunks)
        xp = _merge_call(ya, yb, yc, p_act, xp, gate_p, wa, wb, wc, wo, fng,
                         tm=_row_tile(seq, 512), rows_per_mod=seq, final=final)
        outs["cp"].append(c_p)
        outs["np"].append(n_p_state)
        outs["mp"].append(m_p[:, :N_HEADS, 0])
        outs["bp"].append(pool_p[:, 1:, :])

        mod_s = jnp.repeat(mod[l, batch:batch + db], ts, axis=0)
        shift_s, scale_s, gate_s = (mod_s[:, i * D_MODEL:(i + 1) * D_MODEL] for i in range(3))
        p_act, gates, va_f32, p_f32 = _inproj_call(xs, shift_s, scale_s, ng, w_bf, w_gt[l], bif, lng, lnb,
                                                   layer=l, tm=tm_s, rows_per_mod=None, emit_f32=True)
        w_sgu_tiled = jnp.tile(w_sgu[l][:, :ts, :ts], (1, n_rep, n_rep))
        bs_tiled = jnp.tile(b_sgu[l][:, :ts], (1, n_rep))[:, :, None]
        buf_pad = jnp.pad(state_pool[l], ((0, 0), (POOL_PAD - POOL_BUF, 0), (0, 0)))
        ya, yc, pool_s = _sgu_pool_sample_call(p_act, p_f32, buf_pad, w_sgu_tiled, bs_tiled, wpool, pscale,
                                               tb=tb_s, ts=ts)
        m0_tok = jnp.repeat(state_mlstm_m[l], ts, axis=0).T[:, :, None]
        yb, c_s_all, n_s_state, m_s = _mlstm_sample_call(p_act, gates, state_mlstm_C,
                                                         state_mlstm_n[l][:, :, None, :], m0_tok, mng, c_s_all,
                                                         layer=l, ts=ts)
        xs = _merge_call(ya, yb, yc, p_act, xs, gate_s, wa, wb, wc, wo, fng,
                         tm=_row_tile(n_s, 512), rows_per_mod=None, final=final)
        outs["ns"].append(n_s_state[:, :, 0, :])
        outs["ms"].append(m_s[:, :, 0, 0])
        outs["bs"].append(pool_s)
        outs["vs"].append(va_f32.reshape(db, ts, D_MODEL))

    y_prompt = xp.reshape(batch, seq, D_MODEL)
    y_sample = xs.reshape(db, ts, D_MODEL)
    return (y_prompt, y_sample,
            jnp.stack(outs["cp"]), jnp.stack(outs["np"]), jnp.stack(outs["mp"]), jnp.stack(outs["bp"]),
            c_s_all, jnp.stack(outs["ns"]), jnp.stack(outs["ms"]), jnp.stack(outs["bs"]),
            jnp.stack(outs["vs"]))
```

```python
import functools

import jax
import jax.numpy as jnp
from jax import lax
from jax.experimental import pallas as pl
from jax.experimental.pallas import tpu as pltpu

F32 = jnp.float32
BF16 = jnp.bfloat16

D_MODEL = 1024
N_HEADS = 4
HEAD_W = D_MODEL // N_HEADS
CHUNK = 128
POOL_WINDOWS = (2, 4, 8, 16)
POOL_BUF = 15
POOL_PAD = 16
PAST_LEN = 16384
EPS = 1e-6
N_SEG = 13
SEG_GA, SEG_GB, SEG_GC, SEG_U, SEG_VA, SEG_ZA, SEG_Q, SEG_K, SEG_VB, SEG_OB, SEG_ZB, SEG_P, SEG_ZC = range(N_SEG)
N_ALIGNED_SEG = 11
N_GATE_COLS = 2 * N_HEADS
GATE_COL = N_ALIGNED_SEG * D_MODEL
VMEM_LIMIT_BYTES = 56 * 1024 * 1024


def _params(*semantics):
    return pltpu.CompilerParams(dimension_semantics=semantics, vmem_limit_bytes=VMEM_LIMIT_BYTES)


def _dot(a, b):
    return jnp.dot(a, b, preferred_element_type=F32)


def _dot_nt(a, b):
    return lax.dot_general(a, b, (((1,), (1,)), ((), ())), preferred_element_type=F32)


def _dot_tn(a, b):
    return jnp.dot(a.T, b, preferred_element_type=F32)


def _sigmoid(x):
    return 0.5 * jnp.tanh(0.5 * x) + 0.5


def _silu(x):
    return x * _sigmoid(x)


def _log_sigmoid(x):
    return jnp.minimum(x, 0.0) - jnp.log(1.0 + jnp.exp(-jnp.abs(x)))


def _gelu(x):
    return jax.nn.gelu(x, approximate=True)


def _mod_kernel(c_ref, w_ref, b_ref, o_ref):
    c = c_ref[...]
    o_ref[0] = _dot(_silu(c).astype(BF16), w_ref[0].astype(BF16)) + b_ref[0]


def _mod_call(c_all, w_mod, b_mod):
    depth = w_mod.shape[0]
    rows = c_all.shape[0]
    return pl.pallas_call(
        _mod_kernel,
        grid=(depth, 3),
        in_specs=[pl.BlockSpec((rows, D_MODEL), lambda l, j: (0, 0)),
                  pl.BlockSpec((1, D_MODEL, D_MODEL), lambda l, j: (l, 0, j)),
                  pl.BlockSpec((1, 1, D_MODEL), lambda l, j: (l, 0, j))],
        out_specs=pl.BlockSpec((1, rows, D_MODEL), lambda l, j: (l, 0, j)),
        out_shape=jax.ShapeDtypeStruct((depth, rows, 3 * D_MODEL), F32),
        compiler_params=_params("arbitrary", "arbitrary"),
        name="adaln_mod",
    )(c_all, w_mod, b_mod.reshape(depth, 1, 3 * D_MODEL))


def _wcast_kernel(w_ref, wnext_ref, o_ref):
    j = pl.program_id(1)

    @pl.when(j < N_ALIGNED_SEG)
    def _():
        o_ref[0] = w_ref[0].T.astype(BF16)

    @pl.when(j >= N_ALIGNED_SEG)
    def _():
        rows = jnp.concatenate([w_ref[0][N_GATE_COLS:, :], wnext_ref[0]], axis=0)
        o_ref[0] = rows.T.astype(BF16)


def _wcast_call(w_in_t):
    depth = w_in_t.shape[0]
    next_blocks = D_MODEL // N_GATE_COLS
    return pl.pallas_call(
        _wcast_kernel,
        grid=(depth, N_SEG),
        in_specs=[pl.BlockSpec((1, D_MODEL, D_MODEL), lambda l, j: (l, j, 0)),
                  pl.BlockSpec((1, N_GATE_COLS, D_MODEL), lambda l, j: (l, (j + 1) * next_blocks, 0))],
        out_specs=pl.BlockSpec((1, D_MODEL, D_MODEL), lambda l, j: (l, 0, j)),
        out_shape=jax.ShapeDtypeStruct((depth, D_MODEL, N_SEG * D_MODEL), BF16),
        compiler_params=_params("arbitrary", "arbitrary"),
        name="w_in_bf16",
    )(w_in_t, w_in_t)


def _inproj_kernel(x_ref, shift_ref, scale_ref, ng_ref, w_ref, wgt_ref, bif_ref, lng_ref, lnb_ref,
                   p_ref, g_ref, *rest, per_token, emit_f32, tm, sub):
    if emit_f32:
        va_ref, pf_ref, h_scr = rest
    else:
        (h_scr,) = rest
    j = pl.program_id(1)

    def normed_rows(rows):
        x = x_ref[rows, :]
        ms = jnp.mean(x * x, axis=-1, keepdims=True)
        y = (x * lax.rsqrt(ms + EPS)) * ng_ref[...]
        if per_token:
            sc, sh = scale_ref[rows, :], shift_ref[rows, :]
        else:
            sc, sh = scale_ref[0], shift_ref[0]
        h = (y * (1.0 + sc) + sh).astype(BF16)
        h_scr[rows, :] = h
        g_ref[:, rows] = _dot_nt(wgt_ref[...], h) + bif_ref[...]
        return h

    def project(epilogue, first=False):
        w = w_ref[0]
        step = max(sub // 2, CHUNK) if first else sub
        for r0 in range(0, tm, step):
            rows = slice(r0, r0 + step)
            h = normed_rows(rows) if first else h_scr[rows, :]
            epilogue(rows, _dot(h, w))

    def store(fn):
        def epilogue(rows, acc):
            p_ref[rows, :] = fn(acc).astype(BF16)
        return epilogue

    def va_epilogue(rows, acc):
        gl = _gelu(acc)
        mu = jnp.mean(gl, axis=-1, keepdims=True)
        ctr = gl - mu
        var = jnp.mean(ctr * ctr, axis=-1, keepdims=True)
        y = ctr * lax.rsqrt(var + EPS) * lng_ref[...] + lnb_ref[...]
        p_ref[rows, :] = y.astype(BF16)
        if emit_f32:
            va_ref[rows, :] = y

    def p_epilogue(rows, acc):
        p_ref[rows, :] = acc.astype(BF16)
        if emit_f32:
            pf_ref[rows, :] = acc

    pl.when(j == SEG_GA)(lambda: project(store(_sigmoid), first=True))
    pl.when((j == SEG_GB) | (j == SEG_GC) | (j == SEG_OB))(lambda: project(store(_sigmoid)))
    pl.when(j == SEG_U)(lambda: project(store(_gelu)))
    pl.when(j == SEG_VA)(lambda: project(va_epilogue))
    pl.when((j == SEG_ZA) | (j == SEG_ZB) | (j == SEG_ZC))(lambda: project(store(_silu)))
    pl.when((j == SEG_Q) | (j == SEG_VB))(lambda: project(store(lambda a: a)))
    pl.when(j == SEG_K)(lambda: project(store(lambda a: a * HEAD_W ** -0.5)))
    pl.when(j == SEG_P)(lambda: project(p_epilogue))


def _inproj_call(x, shift, scale, norm_g, w_bf, w_gt, b_if, ln_g, ln_b, *, layer, tm, rows_per_mod, emit_f32):
    n = x.shape[0]
    per_token = rows_per_mod is None
    if per_token:
        mod_spec = pl.BlockSpec((tm, D_MODEL), lambda i, j: (i, 0))
    else:
        tiles_per_mod = rows_per_mod // tm
        mod_spec = pl.BlockSpec((1, 1, D_MODEL), lambda i, j: (i // tiles_per_mod, 0, 0))
    vec_spec = pl.BlockSpec((1, D_MODEL), lambda i, j: (0, 0))
    row_spec = pl.BlockSpec((tm, D_MODEL), lambda i, j: (i, 0))
    out_shape = [jax.ShapeDtypeStruct((n, N_SEG * D_MODEL), BF16), jax.ShapeDtypeStruct((8, n), F32)]
    out_specs = [pl.BlockSpec((tm, D_MODEL), lambda i, j: (i, j)), pl.BlockSpec((8, tm), lambda i, j: (0, i))]
    if emit_f32:
        out_shape += [jax.ShapeDtypeStruct((n, D_MODEL), F32)] * 2
        out_specs += [row_spec, row_spec]
    return pl.pallas_call(
        functools.partial(_inproj_kernel, per_token=per_token, emit_f32=emit_f32, tm=tm, sub=max(tm // 4, CHUNK)),
        grid=(n // tm, N_SEG),
        in_specs=[row_spec, mod_spec, mod_spec, vec_spec,
                  pl.BlockSpec((1, D_MODEL, D_MODEL), lambda i, j: (layer, 0, j)),
                  pl.BlockSpec((8, D_MODEL), lambda i, j: (0, 0)),
                  pl.BlockSpec((8, 1), lambda i, j: (0, 0)),
                  vec_spec, vec_spec],
        out_specs=out_specs,
        out_shape=out_shape,
        scratch_shapes=[pltpu.VMEM((tm, D_MODEL), BF16)],
        compiler_params=_params("arbitrary", "arbitrary"),
        name="inproj",
    )(x, shift, scale, norm_g, w_bf, w_gt, b_if, ln_g, ln_b)


def _block_causal_masks(block_len):
    ti = lax.broadcasted_iota(jnp.int32, (CHUNK, CHUNK), 0)
    si = lax.broadcasted_iota(jnp.int32, (CHUNK, CHUNK), 1)
    shift = block_len.bit_length() - 1
    same = (ti >> shift) == (si >> shift)
    return same, same & (si <= ti), si == ti, si


def _sgu_rows(gu_ref, va_ref, sza_ref, wsgu_ref, bs_ref, ya_ref, causal, row_lo, row_hi):
    for g in range(N_HEADS):
        cols = slice(g * HEAD_W, (g + 1) * HEAD_W)
        wg = jnp.where(causal, wsgu_ref[g], 0.0).astype(BF16)
        bias = bs_ref[g]
        for r0 in range(row_lo, row_hi, CHUNK):
            rows = slice(r0, r0 + CHUNK)
            s = _dot(wg, va_ref[rows, cols]) + bias
            ya = gu_ref[rows, cols].astype(F32) * s * sza_ref[rows, cols].astype(F32)
            ya_ref[rows, cols] = ya.astype(BF16)


def _head_out(hh, o_blk, z_blk, mng_blk):
    hb = o_blk.astype(F32) * hh
    mu = jnp.mean(hb, axis=-1, keepdims=True)
    ctr = hb - mu
    var = jnp.mean(ctr * ctr, axis=-1, keepdims=True)
    y = ctr * lax.rsqrt(var + EPS) * mng_blk
    return (y * z_blk.astype(F32)).astype(BF16)


def _row_to_col(row, eye):
    return jnp.sum(jnp.where(eye, row, 0.0), axis=1, keepdims=True)


def _col_to_row(col, eye):
    return jnp.sum(jnp.where(eye, col, 0.0), axis=0, keepdims=True)


def _mix_merge_prompt_kernel(gu_ref, va_ref, sza_ref, p_ref, szc_ref, ga_ref, gb_ref, gc_ref, yb_ref, x_ref,
                             gate_ref, wsgu_ref, bs_ref, wpool_ref, pscale_ref, wa_ref, wb_ref, wc_ref, wo_ref,
                             fng_ref, o_ref, pool_ref, fbuf, ya_ref, yc_ref, *, tc, sub, final):
    c = pl.program_id(1)

    @pl.when(c == 0)
    def _():
        fbuf[0:POOL_PAD, :] = jnp.zeros((POOL_PAD, D_MODEL), F32)

    @pl.when(c > 0)
    def _():
        fbuf[0:POOL_PAD, :] = fbuf[tc:tc + POOL_PAD, :]

    fbuf[POOL_PAD:, :] = p_ref[...].astype(F32)

    _, causal, _, _ = _block_causal_masks(CHUNK)
    for r0 in range(0, tc, sub):
        rows = slice(r0, r0 + sub)
        _sgu_rows(gu_ref, va_ref, sza_ref, wsgu_ref, bs_ref, ya_ref, causal, r0, r0 + sub)

        pos = c * tc + r0 + lax.broadcasted_iota(jnp.int32, (sub, 1), 0)
        base = POOL_PAD + r0
        for g, w in enumerate(POOL_WINDOWS):
            cols = slice(g * HEAD_W, (g + 1) * HEAD_W)
            cur = fbuf[base:base + sub, cols]
            acc = cur
            for k in range(1, w):
                acc = acc + fbuf[base - k:base - k + sub, cols]
            cnt = jnp.minimum(pos + 1, w).astype(F32)
            pooled = acc / cnt - cur
            pm = _dot(pooled.astype(BF16), wpool_ref[g])
            yc_ref[rows, cols] = (pm * pscale_ref[:, cols] * szc_ref[rows, cols].astype(F32)).astype(BF16)

        merged = (ga_ref[rows, :].astype(F32) * _dot(ya_ref[rows, :], wa_ref[...])
                  + gb_ref[rows, :].astype(F32) * _dot(yb_ref[rows, :], wb_ref[...])
                  + gc_ref[rows, :].astype(F32) * _dot(yc_ref[rows, :], wc_ref[...]))
        xn = x_ref[rows, :] + gate_ref[0] * _dot(merged.astype(BF16), wo_ref[...])
        if final:
            ms = jnp.mean(xn * xn, axis=-1, keepdims=True)
            xn = (xn * lax.rsqrt(ms + EPS)) * fng_ref[...]
        o_ref[rows, :] = xn

    @pl.when(c == pl.num_programs(1) - 1)
    def _():
        pool_ref[0] = fbuf[tc:tc + POOL_PAD, :]


def _mix_merge_prompt_call(p_act, yb, x, gate, w_sgu, bs_col, w_pool, pool_scale, wa, wb, wc, wo, fng,
                           *, batch, seq, tc, final):
    n = batch * seq
    nc = seq // tc

    def seg(s):
        return pl.BlockSpec((tc, D_MODEL), lambda b, c: (b * nc + c, s))

    tok_spec = pl.BlockSpec((tc, D_MODEL), lambda b, c: (b * nc + c, 0))
    vec_spec = pl.BlockSpec((1, D_MODEL), lambda b, c: (0, 0))
    w_spec = pl.BlockSpec((D_MODEL, D_MODEL), lambda b, c: (0, 0), pipeline_mode=pl.Buffered(1))
    return pl.pallas_call(
        functools.partial(_mix_merge_prompt_kernel, tc=tc, sub=min(tc, 256), final=final),
        grid=(batch, nc),
        in_specs=[seg(SEG_U), seg(SEG_VA), seg(SEG_ZA), seg(SEG_P), seg(SEG_ZC),
                  seg(SEG_GA), seg(SEG_GB), seg(SEG_GC), tok_spec, tok_spec,
                  pl.BlockSpec((1, 1, D_MODEL), lambda b, c: (b, 0, 0)),
                  pl.BlockSpec((N_HEADS, CHUNK, CHUNK), lambda b, c: (0, 0, 0)),
                  pl.BlockSpec((N_HEADS, CHUNK, 1), lambda b, c: (0, 0, 0)),
                  pl.BlockSpec((N_HEADS, HEAD_W, HEAD_W), lambda b, c: (0, 0, 0)),
                  vec_spec, w_spec, w_spec, w_spec, w_spec, vec_spec],
        out_specs=[tok_spec, pl.BlockSpec((1, POOL_PAD, D_MODEL), lambda b, c: (b, 0, 0))],
        out_shape=[jax.ShapeDtypeStruct((n, D_MODEL), F32), jax.ShapeDtypeStruct((batch, POOL_PAD, D_MODEL), F32)],
        scratch_shapes=[pltpu.VMEM((tc + POOL_PAD, D_MODEL), F32), pltpu.VMEM((tc, D_MODEL), BF16),
                        pltpu.VMEM((tc, D_MODEL), BF16)],
        compiler_params=_params("arbitrary", "arbitrary"),
        name="mix_merge_prompt",
    )(p_act, p_act, p_act, p_act, p_act, p_act, p_act, p_act, yb, x, gate, w_sgu, bs_col, w_pool, pool_scale,
      wa, wb, wc, wo, fng)


def _sgu_pool_sample_kernel(gu_ref, va_ref, sza_ref, szc_ref, pf_ref, buf_ref, wsgu_ref, bs_ref, wpool_ref,
                            pscale_ref, ya_ref, yc_ref, nbuf_ref, fscr, *, tb, ts, n_past):
    rows = tb * ts
    _, causal, _, _ = _block_causal_masks(ts)
    _sgu_rows(gu_ref, va_ref, sza_ref, wsgu_ref, bs_ref, ya_ref, causal, 0, rows)

    fscr[:, 0:POOL_PAD, :] = buf_ref[...]
    fscr[:, POOL_PAD:, :] = pf_ref[...]
    pos = n_past + lax.broadcasted_iota(jnp.int32, (1, ts, 1), 1)
    for g, w in enumerate(POOL_WINDOWS):
        cols = slice(g * HEAD_W, (g + 1) * HEAD_W)
        cur = fscr[:, POOL_PAD:POOL_PAD + ts, cols]
        acc = cur
        for k in range(1, w):
            acc = acc + fscr[:, POOL_PAD - k:POOL_PAD - k + ts, cols]
        cnt = jnp.minimum(pos + 1, w).astype(F32)
        pooled = (acc / cnt - cur).reshape(rows, HEAD_W)
        pm = _dot(pooled.astype(BF16), wpool_ref[g])
        yc_ref[:, cols] = (pm * pscale_ref[:, cols] * szc_ref[:, cols].astype(F32)).astype(BF16)
    nbuf_ref[...] = fscr[:, POOL_PAD + ts - POOL_BUF:POOL_PAD + ts, :]


def _sgu_pool_sample_call(p_act, p_f32, buf_pad, w_sgu_tiled, bs_col_tiled, w_pool, pool_scale, *, tb, ts):
    n = p_act.shape[0]
    db = n // ts
    rows = tb * ts

    def seg(s):
        return pl.BlockSpec((rows, D_MODEL), lambda i: (i, s))

    tok_spec = pl.BlockSpec((rows, D_MODEL), lambda i: (i, 0))
    return pl.pallas_call(
        functools.partial(_sgu_pool_sample_kernel, tb=tb, ts=ts, n_past=PAST_LEN),
        grid=(db // tb,),
        in_specs=[seg(SEG_U), seg(SEG_VA), seg(SEG_ZA), seg(SEG_ZC),
                  pl.BlockSpec((tb, ts, D_MODEL), lambda i: (i, 0, 0)),
                  pl.BlockSpec((tb, POOL_PAD, D_MODEL), lambda i: (i, 0, 0)),
                  pl.BlockSpec((N_HEADS, CHUNK, CHUNK), lambda i: (0, 0, 0)),
                  pl.BlockSpec((N_HEADS, CHUNK, 1), lambda i: (0, 0, 0)),
                  pl.BlockSpec((N_HEADS, HEAD_W, HEAD_W), lambda i: (0, 0, 0)),
                  pl.BlockSpec((1, D_MODEL), lambda i: (0, 0))],
        out_specs=[tok_spec, tok_spec, pl.BlockSpec((tb, POOL_BUF, D_MODEL), lambda i: (i, 0, 0))],
        out_shape=[jax.ShapeDtypeStruct((n, D_MODEL), BF16), jax.ShapeDtypeStruct((n, D_MODEL), BF16),
                   jax.ShapeDtypeStruct((db, POOL_BUF, D_MODEL), F32)],
        scratch_shapes=[pltpu.VMEM((tb, POOL_PAD + ts, D_MODEL), F32)],
        compiler_params=_params("arbitrary"),
        name="sgu_pool_sample",
    )(p_act, p_act, p_act, p_act, p_f32.reshape(db, ts, D_MODEL), buf_pad, w_sgu_tiled, bs_col_tiled, w_pool,
      pool_scale)


def _mlstm_prompt_kernel(q_ref, k_ref, v_ref, o_ref, z_ref, g_ref, mng_ref,
                         yb_ref, c_out, n_out, m_out, c_scr, n_scr, m_scr, *, n_chunks):
    step = pl.program_id(1)

    @pl.when(step == 0)
    def _():
        c_scr[...] = jnp.zeros(c_scr.shape, F32)
        n_scr[...] = jnp.zeros(n_scr.shape, F32)
        m_scr[...] = jnp.zeros(m_scr.shape, F32)

    _, causal, eye, _ = _block_causal_masks(CHUNK)
    heads = range(N_HEADS)
    probs = [(ci, h) for ci in range(n_chunks) for h in heads]

    def rows(p):
        return slice(p[0] * CHUNK, (p[0] + 1) * CHUNK)

    def cols(p):
        return slice(p[1] * HEAD_W, (p[1] + 1) * HEAD_W)

    def each(fn, keys=probs):
        return {p: fn(p) for p in keys}

    lower = jnp.where(causal, 1.0, 0.0).astype(BF16)
    upper = lower.T
    ones_sq = jnp.ones((CHUNK, CHUNK), BF16)
    mean_w = jnp.full((HEAD_W, CHUNK), 1.0 / HEAD_W, BF16)

    def split(x):
        hi = x.astype(BF16)
        return hi, (x - hi.astype(F32)).astype(BF16)

    def wide(x):
        return jnp.concatenate([x, x], axis=1)

    ig_all = g_ref[0:N_HEADS, :]
    lf_all = _log_sigmoid(g_ref[N_HEADS:2 * N_HEADS, :])
    b_col, r_col, num_intra, den_intra, b_end, r_end, c_upd, n_upd = ({} for _ in range(8))

    def independent(ps):
        lf_parts = each(lambda p: split(lf_all[p[1]:p[1] + 1, rows(p)]), ps)
        b_col.update(each(lambda p: sum(_dot_nt(lower, jnp.broadcast_to(part, (CHUNK, CHUNK)))
                                        for part in lf_parts[p]), ps))
        b_row = each(lambda p: sum(_dot(jnp.broadcast_to(part, (16, CHUNK)), upper)
                                   for part in lf_parts[p])[0:1, :], ps)
        a_row = each(lambda p: ig_all[p[1]:p[1] + 1, rows(p)] - b_row[p], ps)
        d_log = each(lambda p: jnp.where(causal, b_col[p] + a_row[p], -jnp.inf), ps)
        r_col.update(each(lambda p: jnp.max(d_log[p], axis=1, keepdims=True), ps))
        qk = each(lambda p: _dot_nt(q_ref[rows(p), cols(p)], k_ref[rows(p), cols(p)]), ps)
        s = each(lambda p: (qk[p] * jnp.exp(d_log[p] - r_col[p])).astype(BF16), ps)
        num_intra.update(each(lambda p: _dot(s[p], v_ref[rows(p), cols(p)]), ps))
        den_intra.update(each(lambda p: _dot(s[p], ones_sq), ps))
        b_end.update(each(lambda p: b_row[p][:, CHUNK - 1:CHUNK], ps))
        log_end = each(lambda p: b_end[p] + a_row[p], ps)
        r_end.update(each(lambda p: jnp.max(log_end[p], axis=1, keepdims=True), ps))
        w_row = each(lambda p: jnp.broadcast_to(jnp.exp(log_end[p] - r_end[p]).astype(BF16), (16, CHUNK)), ps)
        kw_t = each(lambda p: k_ref[rows(p), cols(p)].T * w_row[p][0:1, :], ps)
        c_upd.update(each(lambda p: _dot(kw_t[p], v_ref[rows(p), cols(p)]), ps))
        n_upd.update(each(lambda p: _dot(w_row[p], k_ref[rows(p), cols(p)])[0:1, :], ps))

    independent(probs)

    for ci in range(n_chunks):
        ps = [(ci, h) for h in heads]
        m_prev = each(lambda p: m_scr[p[1]:p[1] + 1, 0:1], ps)
        inter = each(lambda p: b_col[p] + m_prev[p], ps)
        m_t = each(lambda p: jnp.maximum(inter[p], r_col[p]), ps)
        w_intra = each(lambda p: jnp.exp(r_col[p] - m_t[p]), ps)
        w_inter = each(lambda p: jnp.exp(inter[p] - m_t[p]), ps)
        qc = each(lambda p: _dot(q_ref[rows(p), cols(p)], c_scr[p[1]].astype(BF16)), ps)
        qn = each(lambda p: _dot_nt(q_ref[rows(p), cols(p)],
                                    jnp.broadcast_to(n_scr[p[1]:p[1] + 1, :].astype(BF16), (CHUNK, HEAD_W))), ps)
        den = each(lambda p: w_inter[p] * qn[p] + w_intra[p] * den_intra[p], ps)
        inv = each(lambda p: 1.0 / jnp.maximum(jnp.abs(den[p]), jnp.exp(-m_t[p])), ps)
        hh = each(lambda p: wide(w_inter[p] * inv[p]) * qc[p] + wide(w_intra[p] * inv[p]) * num_intra[p], ps)
        hb = each(lambda p: o_ref[rows(p), cols(p)].astype(F32) * hh[p], ps)
        ctr = each(lambda p: hb[p] - wide(_dot(hb[p].astype(BF16), mean_w)), ps)
        var = each(lambda p: _dot((ctr[p] * ctr[p]).astype(BF16), mean_w), ps)
        for p in ps:
            y = ctr[p] * wide(lax.rsqrt(var[p] + EPS)) * mng_ref[:, cols(p)]
            yb_ref[rows(p), cols(p)] = (y * z_ref[rows(p), cols(p)].astype(F32)).astype(BF16)
        m_new = each(lambda p: jnp.maximum(b_end[p] + m_prev[p], r_end[p]), ps)
        decay = each(lambda p: jnp.exp(b_end[p] + m_prev[p] - m_new[p]), ps)
        w_end = each(lambda p: jnp.exp(r_end[p] - m_new[p]), ps)
        for p in ps:
            h = p[1]
            c_scr[h] = decay[p] * c_scr[h] + w_end[p] * c_upd[p]
            n_scr[h:h + 1, :] = decay[p] * n_scr[h:h + 1, :] + w_end[p] * n_upd[p]
            m_scr[h:h + 1, :] = jnp.broadcast_to(m_new[p], (1, CHUNK))

    @pl.when(step == pl.num_programs(1) - 1)
    def _():
        c_out[0] = c_scr[...]
        n_out[0] = n_scr[0:N_HEADS, :]
        m_out[0] = m_scr[...]


def _mlstm_prompt_call(p_act, gates, mng, *, batch, seq, n_chunks):
    n = batch * seq
    rows = n_chunks * CHUNK
    steps = seq // rows

    def seg(s):
        return pl.BlockSpec((rows, D_MODEL), lambda b, c: (b * steps + c, s))

    return pl.pallas_call(
        functools.partial(_mlstm_prompt_kernel, n_chunks=n_chunks),
        grid=(batch, steps),
        in_specs=[seg(SEG_Q), seg(SEG_K), seg(SEG_VB), seg(SEG_OB), seg(SEG_ZB),
                  pl.BlockSpec((8, rows), lambda b, c: (0, b * steps + c)),
                  pl.BlockSpec((1, D_MODEL), lambda b, c: (0, 0))],
        out_specs=[pl.BlockSpec((rows, D_MODEL), lambda b, c: (b * steps + c, 0)),
                   pl.BlockSpec((1, N_HEADS, HEAD_W, HEAD_W), lambda b, c: (b, 0, 0, 0)),
                   pl.BlockSpec((1, N_HEADS, HEAD_W), lambda b, c: (b, 0, 0)),
                   pl.BlockSpec((1, 8, CHUNK), lambda b, c: (b, 0, 0))],
        out_shape=[jax.ShapeDtypeStruct((n, D_MODEL), BF16),
                   jax.ShapeDtypeStruct((batch, N_HEADS, HEAD_W, HEAD_W), F32),
                   jax.ShapeDtypeStruct((batch, N_HEADS, HEAD_W), F32),
                   jax.ShapeDtypeStruct((batch, 8, CHUNK), F32)],
        scratch_shapes=[pltpu.VMEM((N_HEADS, HEAD_W, HEAD_W), F32), pltpu.VMEM((8, HEAD_W), F32),
                        pltpu.VMEM((8, CHUNK), F32)],
        compiler_params=_params("arbitrary", "arbitrary"),
        name="mlstm_prompt",
    )(p_act, p_act, p_act, p_act, p_act, gates, mng)


def _mlstm_sample_kernel(q_ref, k_ref, v_ref, o_ref, z_ref, g_ref, c0_ref, n0_ref, m0_ref, mng_ref, *rest, ts):
    yb_ref, c_out, n_out, m_out = rest[-4:]
    h = pl.program_id(1)
    n_seq = CHUNK // ts
    same, causal, eye, si = _block_causal_masks(ts)
    ig_row = g_ref[pl.ds(h, 1), :]
    lf_row = _log_sigmoid(g_ref[pl.ds(N_HEADS + h, 1), :])
    b_col = jnp.sum(jnp.where(causal, lf_row, 0.0), axis=1, keepdims=True)
    b_row = _col_to_row(b_col, eye)
    a_row = ig_row - b_row
    m_col = m0_ref[0]
    d_log = jnp.where(causal, b_col + a_row, -jnp.inf)
    inter = b_col + m_col
    m_t = jnp.maximum(inter, jnp.max(d_log, axis=1, keepdims=True))
    w_intra = jnp.exp(d_log - m_t)
    w_inter = jnp.exp(inter - m_t)
    q = q_ref[...]
    k = k_ref[...]
    v = v_ref[...]
    s = _dot_nt(q, k) * w_intra
    num_inter = jnp.concatenate(
        [_dot(q[b * ts:(b + 1) * ts, :], c0_ref[0, b, 0].astype(BF16)) for b in range(n_seq)], axis=0)
    n_tok = jnp.concatenate([jnp.broadcast_to(n0_ref[b, 0], (ts, HEAD_W)) for b in range(n_seq)], axis=0)
    num = w_inter * num_inter + _dot(s.astype(BF16), v)
    qn = jnp.sum(q.astype(F32) * n_tok, axis=1, keepdims=True)
    den = w_inter * qn + jnp.sum(s, axis=1, keepdims=True)
    hh = num / jnp.maximum(jnp.abs(den), jnp.exp(-m_t))
    yb_ref[...] = _head_out(hh, o_ref[...], z_ref[...], mng_ref[...])

    last_in_seq = same & ((si & (ts - 1)) == ts - 1)
    b_end_col = jnp.sum(jnp.where(last_in_seq, b_row, 0.0), axis=1, keepdims=True)
    b_end_row = _col_to_row(b_end_col, eye)
    log_end_row = b_end_row + a_row
    seq_max_col = jnp.max(jnp.where(same, log_end_row, -jnp.inf), axis=1, keepdims=True)
    m_new_col = jnp.maximum(b_end_col + m_col, seq_max_col)
    decay_col = jnp.exp(b_end_col + m_col - m_new_col)
    m_new_row = _col_to_row(m_new_col, eye)
    w_end_col = _row_to_col(jnp.exp(log_end_row - m_new_row), eye)
    kw = k.astype(F32) * w_end_col
    kw_t = kw.T
    lane_seq = lax.broadcasted_iota(jnp.int32, (1, CHUNK), 1) >> (ts.bit_length() - 1)
    for b in range(n_seq):
        decay = decay_col[b * ts:b * ts + 1, :]
        kw_b = jnp.where(lane_seq == b, kw_t, 0.0).astype(BF16)
        c_out[0, b, 0] = decay * c0_ref[0, b, 0] + _dot(kw_b, v)
        n_out[b, 0] = decay * n0_ref[b, 0] + jnp.sum(kw[b * ts:(b + 1) * ts, :], axis=0, keepdims=True)
        m_out[b, 0] = jnp.broadcast_to(m_new_col[b * ts:b * ts + 1, :], (1, CHUNK))


def _mlstm_sample_call(p_act, gates, c0_all, n0, m0_tok, mng, c_new_all, *, layer, ts):
    n = p_act.shape[0]
    db = n // ts
    n_seq = CHUNK // ts

    def seg(s):
        return pl.BlockSpec((CHUNK, HEAD_W), lambda i, h: (i, s * N_HEADS + h))

    state_c = pl.BlockSpec((1, n_seq, 1, HEAD_W, HEAD_W), lambda i, h: (layer, i, h, 0, 0))
    state_n = pl.BlockSpec((n_seq, 1, 1, HEAD_W), lambda i, h: (i, h, 0, 0))
    in_specs = [seg(SEG_Q), seg(SEG_K), seg(SEG_VB), seg(SEG_OB), seg(SEG_ZB),
                pl.BlockSpec((8, CHUNK), lambda i, h: (0, i)),
                state_c, state_n,
                pl.BlockSpec((1, CHUNK, 1), lambda i, h: (h, i, 0)),
                pl.BlockSpec((1, HEAD_W), lambda i, h: (0, h))]
    args = [p_act, p_act, p_act, p_act, p_act, gates, c0_all, n0, m0_tok, mng]
    aliases = {}
    if c_new_all is not None:
        aliases = {len(args): 1}
        in_specs.append(pl.BlockSpec(memory_space=pl.ANY))
        args.append(c_new_all)
    return pl.pallas_call(
        functools.partial(_mlstm_sample_kernel, ts=ts),
        grid=(n // CHUNK, N_HEADS),
        in_specs=in_specs,
        out_specs=[pl.BlockSpec((CHUNK, HEAD_W), lambda i, h: (i, h)),
                   state_c, state_n,
                   pl.BlockSpec((n_seq, 1, 1, CHUNK), lambda i, h: (i, h, 0, 0))],
        out_shape=[jax.ShapeDtypeStruct((n, D_MODEL), BF16),
                   jax.ShapeDtypeStruct(c0_all.shape, F32),
                   jax.ShapeDtypeStruct((db, N_HEADS, 1, HEAD_W), F32),
                   jax.ShapeDtypeStruct((db, N_HEADS, 1, CHUNK), F32)],
        input_output_aliases=aliases,
        compiler_params=_params("arbitrary", "arbitrary"),
        name="mlstm_sample",
    )(*args)


def _merge_kernel(ya_ref, yb_ref, yc_ref, ga_ref, gb_ref, gc_ref, x_ref, gate_ref, wa_ref, wb_ref, wc_ref,
                  wo_ref, fng_ref, o_ref, *, per_token, final):
    merged = (ga_ref[...].astype(F32) * _dot(ya_ref[...], wa_ref[...])
              + gb_ref[...].astype(F32) * _dot(yb_ref[...], wb_ref[...])
              + gc_ref[...].astype(F32) * _dot(yc_ref[...], wc_ref[...]))
    gate = gate_ref[...] if per_token else gate_ref[0]
    xn = x_ref[...] + gate * _dot(merged.astype(BF16), wo_ref[...])
    if final:
        ms = jnp.mean(xn * xn, axis=-1, keepdims=True)
        xn = (xn * lax.rsqrt(ms + EPS)) * fng_ref[...]
    o_ref[...] = xn


def _merge_call(ya, yb, yc, p_act, x, gate, wa, wb, wc, wo, fng, *, tm, rows_per_mod, final):
    n = x.shape[0]
    per_token = rows_per_mod is None
    tok_spec = pl.BlockSpec((tm, D_MODEL), lambda i: (i, 0))
    if per_token:
        gate_spec = tok_spec
    else:
        tiles_per_mod = rows_per_mod // tm
        gate_spec = pl.BlockSpec((1, 1, D_MODEL), lambda i: (i // tiles_per_mod, 0, 0))

    def seg(s):
        return pl.BlockSpec((tm, D_MODEL), lambda i: (i, s))

    w_spec = pl.BlockSpec((D_MODEL, D_MODEL), lambda i: (0, 0))
    return pl.pallas_call(
        functools.partial(_merge_kernel, per_token=per_token, final=final),
        grid=(n // tm,),
        in_specs=[tok_spec, tok_spec, tok_spec, seg(SEG_GA), seg(SEG_GB), seg(SEG_GC), tok_spec, gate_spec,
                  w_spec, w_spec, w_spec, w_spec, pl.BlockSpec((1, D_MODEL), lambda i: (0, 0))],
        out_specs=tok_spec,
        out_shape=jax.ShapeDtypeStruct((n, D_MODEL), F32),
        compiler_params=_params("arbitrary"),
        name="merge_out",
    )(ya, yb, yc, p_act, p_act, p_act, x, gate, wa, wb, wc, wo, fng)


def _row_tile(n, target):
    tile = min(n, target)
    assert n % tile == 0 and tile % CHUNK == 0, (n, tile)
    return tile


def kernel(x_prompt, x_sample, c_prompt, c_sample, state_mlstm_C, state_mlstm_n, state_mlstm_m, state_pool, w_mod, b_mod, norm_g, w_in, b_if, sgu_ln_g, sgu_ln_b, w_sgu, b_sgu, mlstm_norm_g, w_pool, pool_scale, w_br_a, w_br_b, w_br_c, w_out, final_norm_g):
    batch, seq, _ = x_prompt.shape
    db, ts, _ = x_sample.shape
    depth = w_mod.shape[0]
    assert seq % CHUNK == 0 and CHUNK % ts == 0 and ts & (ts - 1) == 0 and (db * ts) % CHUNK == 0
    n_p, n_s = batch * seq, db * ts

    n_mod = batch + db
    c_all = jnp.concatenate([c_prompt, c_sample, jnp.zeros((-n_mod % 8, D_MODEL), F32)], axis=0)
    mod = _mod_call(c_all, w_mod, b_mod)

    w_in_t = jnp.swapaxes(w_in, 1, 2)
    w_bf = _wcast_call(w_in_t)
    w_gt = w_in_t[:, GATE_COL:GATE_COL + N_GATE_COLS, :].astype(BF16)

    xp = x_prompt.reshape(n_p, D_MODEL)
    xs = x_sample.reshape(n_s, D_MODEL)
    tm_p = _row_tile(seq, 2048)
    tm_s = _row_tile(n_s, 512)
    tc_p = _row_tile(seq, 512)
    mlstm_chunks = min(seq // CHUNK, 8)
    assert (seq // CHUNK) % mlstm_chunks == 0
    tb_s = min(db, 32)
    fng = final_norm_g.reshape(1, D_MODEL)
    n_rep = CHUNK // ts

    outs = {k: [] for k in ("cp", "np", "mp", "bp", "ns", "ms", "bs", "vs")}
    c_s_all = None
    for l in range(depth):
        bif = b_if[l].reshape(8, 1)
        ng = norm_g[l].reshape(1, D_MODEL)
        lng = sgu_ln_g[l].reshape(1, D_MODEL)
        lnb = sgu_ln_b[l].reshape(1, D_MODEL)
        mng = mlstm_norm_g[l].reshape(1, D_MODEL)
        pscale = pool_scale[l].reshape(1, D_MODEL)
        wpool = w_pool[l].astype(BF16)
        wa, wb, wc, wo = (w[l].astype(BF16) for w in (w_br_a, w_br_b, w_br_c, w_out))
        final = l == depth - 1

        mod_p = mod[l, :batch].reshape(batch, 1, 3 * D_MODEL)
        shift_p, scale_p, gate_p = (mod_p[..., i * D_MODEL:(i + 1) * D_MODEL] for i in range(3))
        p_act, gates = _inproj_call(xp, shift_p, scale_p, ng, w_bf, w_gt[l], bif, lng, lnb,
                                    layer=l, tm=tm_p, rows_per_mod=seq, emit_f32=False)
        yb, c_p, n_p_state, m_p = _mlstm_prompt_call(p_act, gates, mng, batch=batch, seq=seq,
                                                     n_chunks=mlstm_chunks)
        xp, pool_p = _mix_merge_prompt_call(p_act, yb, xp, gate_p, w_sgu[l], b_sgu[l][:, :, None], wpool, pscale,
                                            wa, wb, wc, wo, fng, batch=batch, seq=seq, tc=tc_p, final=final)
        outs["cp"].append(c_p)
        outs["np"].append(n_p_state)
        outs["mp"].append(m_p[:, :N_HEADS, 0])
        outs["bp"].append(pool_p[:, 1:, :])

        mod_s = jnp.repeat(mod[l, batch:batch + db], ts, axis=0)
        shift_s, scale_s, gate_s = (mod_s[:, i * D_MODEL:(i + 1) * D_MODEL] for i in range(3))
        p_act, gates, va_f32, p_f32 = _inproj_call(xs, shift_s, scale_s, ng, w_bf, w_gt[l], bif, lng, lnb,
                                                   layer=l, tm=tm_s, rows_per_mod=None, emit_f32=True)
        w_sgu_tiled = jnp.tile(w_sgu[l][:, :ts, :ts], (1, n_rep, n_rep))
        bs_tiled = jnp.tile(b_sgu[l][:, :ts], (1, n_rep))[:, :, None]
        buf_pad = jnp.pad(state_pool[l], ((0, 0), (POOL_PAD - POOL_BUF, 0), (0, 0)))
        ya, yc, pool_s = _sgu_pool_sample_call(p_act, p_f32, buf_pad, w_sgu_tiled, bs_tiled, wpool, pscale,
                                               tb=tb_s, ts=ts)
        m0_tok = jnp.repeat(state_mlstm_m[l], ts, axis=0).T[:, :, None]
        yb, c_s_all, n_s_state, m_s = _mlstm_sample_call(p_act, gates, state_mlstm_C,
                                                         state_mlstm_n[l][:, :, None, :], m0_tok, mng, c_s_all,
                                                         layer=l, ts=ts)
        xs = _merge_call(ya, yb, yc, p_act, xs, gate_s, wa, wb, wc, wo, fng,
                         tm=_row_tile(n_s, 512), rows_per_mod=None, final=final)
        outs["ns"].append(n_s_state[:, :, 0, :])
        outs["ms"].append(m_s[:, :, 0, 0])
        outs["bs"].append(pool_s)
        outs["vs"].append(va_f32.reshape(db, ts, D_MODEL))

    y_prompt = xp.reshape(batch, seq, D_MODEL)
    y_sample = xs.reshape(db, ts, D_MODEL)
    return (y_prompt, y_sample,
            jnp.stack(outs["cp"]), jnp.stack(outs["np"]), jnp.stack(outs["mp"]), jnp.stack(outs["bp"]),
            c_s_all, jnp.stack(outs["ns"]), jnp.stack(outs["ms"]), jnp.stack(outs["bs"]),
            jnp.stack(outs["vs"]))
```

```python
import functools

import jax
import jax.numpy as jnp
from jax import lax
from jax.experimental import pallas as pl
from jax.experimental.pallas import tpu as pltpu

F32 = jnp.float32
BF16 = jnp.bfloat16

D_MODEL = 1024
N_HEADS = 4
HEAD_W = D_MODEL // N_HEADS
CHUNK = 128
POOL_WINDOWS = (2, 4, 8, 16)
POOL_BUF = 15
POOL_PAD = 16
POOL_HALO = 24
PAST_LEN = 16384
EPS = 1e-6
N_SEG = 13
SEG_GA, SEG_GB, SEG_GC, SEG_U, SEG_VA, SEG_ZA, SEG_Q, SEG_K, SEG_VB, SEG_OB, SEG_ZB, SEG_P, SEG_ZC = range(N_SEG)
N_ALIGNED_SEG = 11
N_GATE_COLS = 2 * N_HEADS
GATE_COL = N_ALIGNED_SEG * D_MODEL
VMEM_LIMIT_BYTES = 56 * 1024 * 1024


def _params(*semantics):
    return pltpu.CompilerParams(dimension_semantics=semantics, vmem_limit_bytes=VMEM_LIMIT_BYTES)


def _dot(a, b):
    return jnp.dot(a, b, preferred_element_type=F32)


def _dot_nt(a, b):
    return lax.dot_general(a, b, (((1,), (1,)), ((), ())), preferred_element_type=F32)


def _dot_tn(a, b):
    return jnp.dot(a.T, b, preferred_element_type=F32)


def _sigmoid(x):
    return 0.5 * jnp.tanh(0.5 * x) + 0.5


def _silu(x):
    return x * _sigmoid(x)


def _log_sigmoid(x):
    return jnp.minimum(x, 0.0) - jnp.log(1.0 + jnp.exp(-jnp.abs(x)))


def _gelu(x):
    return jax.nn.gelu(x, approximate=True)


def _mod_kernel(c_ref, w_ref, b_ref, o_ref):
    c = c_ref[...]
    o_ref[0] = _dot(_silu(c).astype(BF16), w_ref[0].astype(BF16)) + b_ref[0]


def _mod_call(c_all, w_mod, b_mod):
    depth = w_mod.shape[0]
    rows = c_all.shape[0]
    return pl.pallas_call(
        _mod_kernel,
        grid=(depth, 3),
        in_specs=[pl.BlockSpec((rows, D_MODEL), lambda l, j: (0, 0)),
                  pl.BlockSpec((1, D_MODEL, D_MODEL), lambda l, j: (l, 0, j)),
                  pl.BlockSpec((1, 1, D_MODEL), lambda l, j: (l, 0, j))],
        out_specs=pl.BlockSpec((1, rows, D_MODEL), lambda l, j: (l, 0, j)),
        out_shape=jax.ShapeDtypeStruct((depth, rows, 3 * D_MODEL), F32),
        compiler_params=_params("arbitrary", "arbitrary"),
        name="adaln_mod",
    )(c_all, w_mod, b_mod.reshape(depth, 1, 3 * D_MODEL))


def _wcast_kernel(w_ref, wnext_ref, o_ref, gate_ref):
    j = pl.program_id(1)

    @pl.when(j < N_ALIGNED_SEG)
    def _():
        o_ref[0] = w_ref[0].T.astype(BF16)

    @pl.when(j == N_ALIGNED_SEG - 1)
    def _():
        gate_ref[...] = wnext_ref[...].astype(BF16)

    @pl.when(j >= N_ALIGNED_SEG)
    def _():
        rows = jnp.concatenate([w_ref[0][N_GATE_COLS:, :], wnext_ref[0]], axis=0)
        o_ref[0] = rows.T.astype(BF16)


def _wcast_call(w_in_t):
    depth = w_in_t.shape[0]
    next_blocks = D_MODEL // N_GATE_COLS
    return pl.pallas_call(
        _wcast_kernel,
        grid=(depth, N_SEG),
        in_specs=[pl.BlockSpec((1, D_MODEL, D_MODEL), lambda l, j: (l, j, 0)),
                  pl.BlockSpec((1, N_GATE_COLS, D_MODEL), lambda l, j: (l, (j + 1) * next_blocks, 0))],
        out_specs=[pl.BlockSpec((1, D_MODEL, D_MODEL), lambda l, j: (l, 0, j)),
                   pl.BlockSpec((1, N_GATE_COLS, D_MODEL), lambda l, j: (l, 0, 0))],
        out_shape=[jax.ShapeDtypeStruct((depth, D_MODEL, N_SEG * D_MODEL), BF16),
                   jax.ShapeDtypeStruct((depth, N_GATE_COLS, D_MODEL), BF16)],
        compiler_params=_params("arbitrary", "arbitrary"),
        name="w_in_bf16",
    )(w_in_t, w_in_t)


def _inproj_kernel(x_ref, shift_ref, scale_ref, ng_ref, w_ref, wgt_ref, bif_ref, lng_ref, lnb_ref,
                   p_ref, g_ref, *rest, per_token, emit_f32, tm, sub, tiles_per_seq):
    if emit_f32:
        va_ref, pf_ref, h_scr = rest
    else:
        pool_ref, h_scr, pbuf, *level_bufs = rest
    j = pl.program_id(1)

    def normed_rows(rows):
        x = x_ref[rows, :]
        ms = jnp.mean(x * x, axis=-1, keepdims=True)
        y = (x * lax.rsqrt(ms + EPS)) * ng_ref[...]
        if per_token:
            sc, sh = scale_ref[rows, :], shift_ref[rows, :]
        else:
            sc, sh = scale_ref[0], shift_ref[0]
        h = (y * (1.0 + sc) + sh).astype(BF16)
        h_scr[rows, :] = h
        g_ref[:, rows] = _dot_nt(wgt_ref[...], h) + bif_ref[...]
        return h

    def project(epilogue, first=False):
        w = w_ref[0]
        step = max(sub // 2, CHUNK) if first else sub
        for r0 in range(0, tm, step):
            rows = slice(r0, r0 + step)
            h = normed_rows(rows) if first else h_scr[rows, :]
            epilogue(rows, _dot(h, w))

    def store(fn):
        def epilogue(rows, acc):
            p_ref[rows, :] = fn(acc).astype(BF16)
        return epilogue

    def va_epilogue(rows, acc):
        gl = _gelu(acc)
        mu = jnp.mean(gl, axis=-1, keepdims=True)
        ctr = gl - mu
        var = jnp.mean(ctr * ctr, axis=-1, keepdims=True)
        y = ctr * lax.rsqrt(var + EPS) * lng_ref[...] + lnb_ref[...]
        p_ref[rows, :] = y.astype(BF16)
        if emit_f32:
            va_ref[rows, :] = y

    def p_epilogue(rows, acc):
        p_ref[rows, :] = acc.astype(BF16)
        pf_ref[rows, :] = acc

    def pooled_epilogue(rows, acc):
        n_rows = rows.stop - rows.start
        tile_in_seq = pl.program_id(0) % tiles_per_seq
        sums = (pbuf,) + tuple(level_bufs)
        if rows.start == 0:
            @pl.when(tile_in_seq == 0)
            def _():
                for buf in sums:
                    buf[0:POOL_HALO, :] = jnp.zeros((POOL_HALO, buf.shape[1]), F32)
        pbuf[POOL_HALO:POOL_HALO + n_rows, :] = acc
        pos = tile_in_seq * tm + rows.start + lax.broadcasted_iota(jnp.int32, (n_rows, 1), 0)
        live = slice(POOL_HALO, POOL_HALO + n_rows)
        for g, w in enumerate(POOL_WINDOWS):
            prev = sums[g]
            shift = w // 2
            span = slice(8, POOL_HALO + n_rows)
            shifted = slice(8 - shift, POOL_HALO + n_rows - shift)
            if g + 1 < len(POOL_WINDOWS):
                sums[g + 1][span, :] = prev[span, HEAD_W:] + prev[shifted, HEAD_W:]
            total = prev[live, 0:HEAD_W] + prev[slice(live.start - shift, live.stop - shift), 0:HEAD_W]
            cnt = jnp.minimum(pos + 1, w).astype(F32)
            cols = slice(g * HEAD_W, (g + 1) * HEAD_W)
            p_ref[rows, cols] = (total / cnt - pbuf[live, cols]).astype(BF16)
        tail = pbuf[n_rows + POOL_HALO - POOL_PAD:n_rows + POOL_HALO, :]
        pool_ref[0] = tail
        pbuf[POOL_HALO - POOL_PAD:POOL_HALO, :] = tail

    pl.when(j == SEG_GA)(lambda: project(store(_sigmoid), first=True))
    pl.when((j == SEG_GB) | (j == SEG_GC) | (j == SEG_OB))(lambda: project(store(_sigmoid)))
    pl.when(j == SEG_U)(lambda: project(store(_gelu)))
    pl.when(j == SEG_VA)(lambda: project(va_epilogue))
    pl.when((j == SEG_ZA) | (j == SEG_ZB) | (j == SEG_ZC))(lambda: project(store(_silu)))
    pl.when((j == SEG_Q) | (j == SEG_VB))(lambda: project(store(lambda a: a)))
    pl.when(j == SEG_K)(lambda: project(store(lambda a: a * HEAD_W ** -0.5)))
    pl.when(j == SEG_P)(lambda: project(p_epilogue if emit_f32 else pooled_epilogue))


def _inproj_call(x, shift, scale, norm_g, w_bf, w_gt, b_if, ln_g, ln_b, *, layer, tm, rows_per_mod, emit_f32):
    n = x.shape[0]
    per_token = rows_per_mod is None
    assert per_token == emit_f32
    sub = min(tm, max(tm // 4, 2 * CHUNK))
    tiles_per_seq = None if per_token else rows_per_mod // tm
    vec_spec = pl.BlockSpec((1, D_MODEL), lambda i, j: (0, 0))
    row_spec = pl.BlockSpec((tm, D_MODEL), lambda i, j: (i, 0))
    out_shape = [jax.ShapeDtypeStruct((n, N_SEG * D_MODEL), BF16), jax.ShapeDtypeStruct((8, n), F32)]
    out_specs = [pl.BlockSpec((tm, D_MODEL), lambda i, j: (i, j)), pl.BlockSpec((8, tm), lambda i, j: (0, i))]
    scratch = [pltpu.VMEM((tm, D_MODEL), BF16)]
    if emit_f32:
        mod_spec = pl.BlockSpec((tm, D_MODEL), lambda i, j: (i, 0))
        out_shape += [jax.ShapeDtypeStruct((n, D_MODEL), F32)] * 2
        out_specs += [row_spec, row_spec]
    else:
        mod_spec = pl.BlockSpec((1, 1, D_MODEL), lambda i, j: (i // tiles_per_seq, 0, 0))
        out_shape.append(jax.ShapeDtypeStruct((n // rows_per_mod, POOL_PAD, D_MODEL), F32))
        out_specs.append(pl.BlockSpec((1, POOL_PAD, D_MODEL), lambda i, j: (i // tiles_per_seq, 0, 0)))
        scratch += [pltpu.VMEM((POOL_HALO + sub, D_MODEL - g * HEAD_W), F32) for g in range(len(POOL_WINDOWS))]
    return pl.pallas_call(
        functools.partial(_inproj_kernel, per_token=per_token, emit_f32=emit_f32, tm=tm, sub=sub,
                          tiles_per_seq=tiles_per_seq),
        grid=(n // tm, N_SEG),
        in_specs=[row_spec, mod_spec, mod_spec, vec_spec,
                  pl.BlockSpec((1, D_MODEL, D_MODEL), lambda i, j: (layer, 0, j)),
                  pl.BlockSpec((8, D_MODEL), lambda i, j: (0, 0)),
                  pl.BlockSpec((8, 1), lambda i, j: (0, 0)),
                  vec_spec, vec_spec],
        out_specs=out_specs,
        out_shape=out_shape,
        scratch_shapes=scratch,
        compiler_params=_params("arbitrary", "arbitrary"),
        name="inproj",
    )(x, shift, scale, norm_g, w_bf, w_gt, b_if, ln_g, ln_b)


def _block_causal_masks(block_len):
    ti = lax.broadcasted_iota(jnp.int32, (CHUNK, CHUNK), 0)
    si = lax.broadcasted_iota(jnp.int32, (CHUNK, CHUNK), 1)
    shift = block_len.bit_length() - 1
    same = (ti >> shift) == (si >> shift)
    return same, same & (si <= ti), si == ti, si


def _sgu_rows(gu_ref, va_ref, sza_ref, wsgu_ref, bs_ref, ya_ref, causal, row_lo, row_hi):
    for g in range(N_HEADS):
        cols = slice(g * HEAD_W, (g + 1) * HEAD_W)
        wg = jnp.where(causal, wsgu_ref[g], 0.0).astype(BF16)
        bias = bs_ref[g]
        for r0 in range(row_lo, row_hi, CHUNK):
            rows = slice(r0, r0 + CHUNK)
            s = _dot(wg, va_ref[rows, cols]) + bias
            ya = gu_ref[rows, cols].astype(F32) * s * sza_ref[rows, cols].astype(F32)
            ya_ref[rows, cols] = ya.astype(BF16)


def _head_out(hh, o_blk, z_blk, mng_blk):
    hb = o_blk.astype(F32) * hh
    mu = jnp.mean(hb, axis=-1, keepdims=True)
    ctr = hb - mu
    var = jnp.mean(ctr * ctr, axis=-1, keepdims=True)
    y = ctr * lax.rsqrt(var + EPS) * mng_blk
    return (y * z_blk.astype(F32)).astype(BF16)


def _row_to_col(row, eye):
    return jnp.sum(jnp.where(eye, row, 0.0), axis=1, keepdims=True)


def _col_to_row(col, eye):
    return jnp.sum(jnp.where(eye, col, 0.0), axis=0, keepdims=True)


def _mix_merge_prompt_kernel(gu_ref, va_ref, sza_ref, pooled_ref, szc_ref, ga_ref, gb_ref, gc_ref, yb_ref, x_ref,
                             gate_ref, wsgu_ref, bs_ref, wpool_ref, pscale_ref, wa_ref, wb_ref, wc_ref, wo_ref,
                             fng_ref, o_ref, ya_ref, yc_ref, *, tc, sub, final):
    _, causal, _, _ = _block_causal_masks(CHUNK)

    def pool_project(rows):
        for g in range(N_HEADS):
            cols = slice(g * HEAD_W, (g + 1) * HEAD_W)
            pm = _dot(pooled_ref[rows, cols], wpool_ref[g])
            yc_ref[rows, cols] = (pm * pscale_ref[:, cols] * szc_ref[rows, cols].astype(F32)).astype(BF16)

    def gated(g_ref, y_ref, w_ref, rows):
        return g_ref[rows, :].astype(F32) * _dot(y_ref[rows, :], w_ref[...])

    for r0 in range(0, tc, sub):
        rows = slice(r0, r0 + sub)
        part_b = gated(gb_ref, yb_ref, wb_ref, rows)
        _sgu_rows(gu_ref, va_ref, sza_ref, wsgu_ref, bs_ref, ya_ref, causal, r0, r0 + sub)
        part_ab = gated(ga_ref, ya_ref, wa_ref, rows) + part_b
        pool_project(rows)
        merged = part_ab + gated(gc_ref, yc_ref, wc_ref, rows)
        xn = x_ref[rows, :] + gate_ref[0] * _dot(merged.astype(BF16), wo_ref[...])
        if final:
            ms = jnp.mean(xn * xn, axis=-1, keepdims=True)
            xn = (xn * lax.rsqrt(ms + EPS)) * fng_ref[...]
        o_ref[rows, :] = xn


def _mix_merge_prompt_call(p_act, yb, x, gate, w_sgu, bs_col, w_pool, pool_scale, wa, wb, wc, wo, fng,
                           *, seq, tc, final):
    n = x.shape[0]
    tiles_per_seq = seq // tc

    def seg(s):
        return pl.BlockSpec((tc, D_MODEL), lambda i: (i, s))

    tok_spec = pl.BlockSpec((tc, D_MODEL), lambda i: (i, 0))
    vec_spec = pl.BlockSpec((1, D_MODEL), lambda i: (0, 0))
    w_spec = pl.BlockSpec((D_MODEL, D_MODEL), lambda i: (0, 0), pipeline_mode=pl.Buffered(1))
    return pl.pallas_call(
        functools.partial(_mix_merge_prompt_kernel, tc=tc, sub=min(tc, 256), final=final),
        grid=(n // tc,),
        in_specs=[seg(SEG_U), seg(SEG_VA), seg(SEG_ZA), seg(SEG_P), seg(SEG_ZC),
                  seg(SEG_GA), seg(SEG_GB), seg(SEG_GC), tok_spec, tok_spec,
                  pl.BlockSpec((1, 1, D_MODEL), lambda i: (i // tiles_per_seq, 0, 0)),
                  pl.BlockSpec((N_HEADS, CHUNK, CHUNK), lambda i: (0, 0, 0)),
                  pl.BlockSpec((N_HEADS, CHUNK, 1), lambda i: (0, 0, 0)),
                  pl.BlockSpec((N_HEADS, HEAD_W, HEAD_W), lambda i: (0, 0, 0)),
                  vec_spec, w_spec, w_spec, w_spec, w_spec, vec_spec],
        out_specs=tok_spec,
        out_shape=jax.ShapeDtypeStruct((n, D_MODEL), F32),
        scratch_shapes=[pltpu.VMEM((tc, D_MODEL), BF16), pltpu.VMEM((tc, D_MODEL), BF16)],
        compiler_params=_params("arbitrary"),
        name="mix_merge_prompt",
    )(p_act, p_act, p_act, p_act, p_act, p_act, p_act, p_act, yb, x, gate, w_sgu, bs_col, w_pool, pool_scale,
      wa, wb, wc, wo, fng)


def _sgu_pool_sample_kernel(gu_ref, va_ref, sza_ref, szc_ref, pf_ref, buf_ref, wsgu_ref, bs_ref, wpool_ref,
                            pscale_ref, ya_ref, yc_ref, nbuf_ref, fscr, *, tb, ts, n_past):
    rows = tb * ts
    _, causal, _, _ = _block_causal_masks(ts)
    _sgu_rows(gu_ref, va_ref, sza_ref, wsgu_ref, bs_ref, ya_ref, causal, 0, rows)

    fscr[:, 0:POOL_PAD, :] = buf_ref[...]
    fscr[:, POOL_PAD:, :] = pf_ref[...]
    pos = n_past + lax.broadcasted_iota(jnp.int32, (1, ts, 1), 1)
    for g, w in enumerate(POOL_WINDOWS):
        cols = slice(g * HEAD_W, (g + 1) * HEAD_W)
        cur = fscr[:, POOL_PAD:POOL_PAD + ts, cols]
        acc = cur
        for k in range(1, w):
            acc = acc + fscr[:, POOL_PAD - k:POOL_PAD - k + ts, cols]
        cnt = jnp.minimum(pos + 1, w).astype(F32)
        pooled = (acc / cnt - cur).reshape(rows, HEAD_W)
        pm = _dot(pooled.astype(BF16), wpool_ref[g])
        yc_ref[:, cols] = (pm * pscale_ref[:, cols] * szc_ref[:, cols].astype(F32)).astype(BF16)
    nbuf_ref[...] = fscr[:, POOL_PAD + ts - POOL_BUF:POOL_PAD + ts, :]


def _sgu_pool_sample_call(p_act, p_f32, buf_pad, w_sgu_tiled, bs_col_tiled, w_pool, pool_scale, *, tb, ts):
    n = p_act.shape[0]
    db = n // ts
    rows = tb * ts

    def seg(s):
        return pl.BlockSpec((rows, D_MODEL), lambda i: (i, s))

    tok_spec = pl.BlockSpec((rows, D_MODEL), lambda i: (i, 0))
    return pl.pallas_call(
        functools.partial(_sgu_pool_sample_kernel, tb=tb, ts=ts, n_past=PAST_LEN),
        grid=(db // tb,),
        in_specs=[seg(SEG_U), seg(SEG_VA), seg(SEG_ZA), seg(SEG_ZC),
                  pl.BlockSpec((tb, ts, D_MODEL), lambda i: (i, 0, 0)),
                  pl.BlockSpec((tb, POOL_PAD, D_MODEL), lambda i: (i, 0, 0)),
                  pl.BlockSpec((N_HEADS, CHUNK, CHUNK), lambda i: (0, 0, 0)),
                  pl.BlockSpec((N_HEADS, CHUNK, 1), lambda i: (0, 0, 0)),
                  pl.BlockSpec((N_HEADS, HEAD_W, HEAD_W), lambda i: (0, 0, 0)),
                  pl.BlockSpec((1, D_MODEL), lambda i: (0, 0))],
        out_specs=[tok_spec, tok_spec, pl.BlockSpec((tb, POOL_BUF, D_MODEL), lambda i: (i, 0, 0))],
        out_shape=[jax.ShapeDtypeStruct((n, D_MODEL), BF16), jax.ShapeDtypeStruct((n, D_MODEL), BF16),
                   jax.ShapeDtypeStruct((db, POOL_BUF, D_MODEL), F32)],
        scratch_shapes=[pltpu.VMEM((tb, POOL_PAD + ts, D_MODEL), F32)],
        compiler_params=_params("arbitrary"),
        name="sgu_pool_sample",
    )(p_act, p_act, p_act, p_act, p_f32.reshape(db, ts, D_MODEL), buf_pad, w_sgu_tiled, bs_col_tiled, w_pool,
      pool_scale)


def _mlstm_prompt_kernel(q_ref, k_ref, v_ref, o_ref, z_ref, g_ref, mng_ref,
                         yb_ref, c_out, n_out, m_out, c_scr, n_scr, m_scr, *, n_chunks):
    step = pl.program_id(1)

    @pl.when(step == 0)
    def _():
        c_scr[...] = jnp.zeros(c_scr.shape, F32)
        n_scr[...] = jnp.zeros(n_scr.shape, F32)
        m_scr[...] = jnp.zeros(m_scr.shape, F32)

    _, causal, eye, _ = _block_causal_masks(CHUNK)
    heads = range(N_HEADS)
    probs = [(ci, h) for ci in range(n_chunks) for h in heads]

    def rows(p):
        return slice(p[0] * CHUNK, (p[0] + 1) * CHUNK)

    def cols(p):
        return slice(p[1] * HEAD_W, (p[1] + 1) * HEAD_W)

    def each(fn, keys=probs):
        return {p: fn(p) for p in keys}

    lower = jnp.where(causal, 1.0, 0.0).astype(BF16)
    upper = lower.T
    ones_sq = jnp.ones((CHUNK, CHUNK), BF16)
    mean_w = jnp.full((HEAD_W, CHUNK), 1.0 / HEAD_W, BF16)

    def split(x):
        hi = x.astype(BF16)
        return hi, (x - hi.astype(F32)).astype(BF16)

    def wide(x):
        return jnp.concatenate([x, x], axis=1)

    ig_all = g_ref[0:N_HEADS, :]
    lf_all = _log_sigmoid(g_ref[N_HEADS:2 * N_HEADS, :])
    b_col, r_col, num_intra, den_intra, b_end, r_end, c_upd, n_upd = ({} for _ in range(8))

    def independent(ps):
        lf_parts = each(lambda p: split(lf_all[p[1]:p[1] + 1, rows(p)]), ps)
        b_col.update(each(lambda p: sum(_dot_nt(lower, jnp.broadcast_to(part, (CHUNK, CHUNK)))
                                        for part in lf_parts[p]), ps))
        b_row = each(lambda p: sum(_dot(jnp.broadcast_to(part, (16, CHUNK)), upper)
                                   for part in lf_parts[p])[0:1, :], ps)
        a_row = each(lambda p: ig_all[p[1]:p[1] + 1, rows(p)] - b_row[p], ps)
        d_log = each(lambda p: jnp.where(causal, b_col[p] + a_row[p], -jnp.inf), ps)
        r_col.update(each(lambda p: jnp.max(d_log[p], axis=1, keepdims=True), ps))
        qk = each(lambda p: _dot_nt(q_ref[rows(p), cols(p)], k_ref[rows(p), cols(p)]), ps)
        s = each(lambda p: (qk[p] * jnp.exp(d_log[p] - r_col[p])).astype(BF16), ps)
        num_intra.update(each(lambda p: _dot(s[p], v_ref[rows(p), cols(p)]), ps))
        den_intra.update(each(lambda p: _dot(s[p], ones_sq), ps))
        b_end.update(each(lambda p: b_row[p][:, CHUNK - 1:CHUNK], ps))
        log_end = each(lambda p: b_end[p] + a_row[p], ps)
        r_end.update(each(lambda p: jnp.max(log_end[p], axis=1, keepdims=True), ps))
        w_row = each(lambda p: jnp.broadcast_to(jnp.exp(log_end[p] - r_end[p]).astype(BF16), (16, CHUNK)), ps)
        kw_t = each(lambda p: k_ref[rows(p), cols(p)].T * w_row[p][0:1, :], ps)
        c_upd.update(each(lambda p: _dot(kw_t[p], v_ref[rows(p), cols(p)]), ps))
        n_upd.update(each(lambda p: _dot(w_row[p], k_ref[rows(p), cols(p)])[0:1, :], ps))

    independent(probs)

    for ci in range(n_chunks):
        ps = [(ci, h) for h in heads]
        m_prev = each(lambda p: m_scr[p[1]:p[1] + 1, 0:1], ps)
        inter = each(lambda p: b_col[p] + m_prev[p], ps)
        m_t = each(lambda p: jnp.maximum(inter[p], r_col[p]), ps)
        w_intra = each(lambda p: jnp.exp(r_col[p] - m_t[p]), ps)
        w_inter = each(lambda p: jnp.exp(inter[p] - m_t[p]), ps)
        qc = each(lambda p: _dot(q_ref[rows(p), cols(p)], c_scr[p[1]].astype(BF16)), ps)
        qn = each(lambda p: _dot_nt(q_ref[rows(p), cols(p)],
                                    jnp.broadcast_to(n_scr[p[1]:p[1] + 1, :].astype(BF16), (CHUNK, HEAD_W))), ps)
        den = each(lambda p: w_inter[p] * qn[p] + w_intra[p] * den_intra[p], ps)
        inv = each(lambda p: 1.0 / jnp.maximum(jnp.abs(den[p]), jnp.exp(-m_t[p])), ps)
        hh = each(lambda p: wide(w_inter[p] * inv[p]) * qc[p] + wide(w_intra[p] * inv[p]) * num_intra[p], ps)
        hb = each(lambda p: o_ref[rows(p), cols(p)].astype(F32) * hh[p], ps)
        ctr = each(lambda p: hb[p] - wide(_dot(hb[p].astype(BF16), mean_w)), ps)
        var = each(lambda p: _dot((ctr[p] * ctr[p]).astype(BF16), mean_w), ps)
        for p in ps:
            y = ctr[p] * wide(lax.rsqrt(var[p] + EPS)) * mng_ref[:, cols(p)]
            yb_ref[rows(p), cols(p)] = (y * z_ref[rows(p), cols(p)].astype(F32)).astype(BF16)
        m_new = each(lambda p: jnp.maximum(b_end[p] + m_prev[p], r_end[p]), ps)
        decay = each(lambda p: jnp.exp(b_end[p] + m_prev[p] - m_new[p]), ps)
        w_end = each(lambda p: jnp.exp(r_end[p] - m_new[p]), ps)
        for p in ps:
            h = p[1]
            c_scr[h] = decay[p] * c_scr[h] + w_end[p] * c_upd[p]
            n_scr[h:h + 1, :] = decay[p] * n_scr[h:h + 1, :] + w_end[p] * n_upd[p]
            m_scr[h:h + 1, :] = jnp.broadcast_to(m_new[p], (1, CHUNK))

    @pl.when(step == pl.num_programs(1) - 1)
    def _():
        c_out[0] = c_scr[...]
        n_out[0] = n_scr[0:N_HEADS, :]
        m_out[0] = m_scr[...]


def _mlstm_prompt_call(p_act, gates, mng, *, batch, seq, n_chunks):
    n = batch * seq
    rows = n_chunks * CHUNK
    steps = seq // rows

    def seg(s):
        return pl.BlockSpec((rows, D_MODEL), lambda b, c: (b * steps + c, s))

    return pl.pallas_call(
        functools.partial(_mlstm_prompt_kernel, n_chunks=n_chunks),
        grid=(batch, steps),
        in_specs=[seg(SEG_Q), seg(SEG_K), seg(SEG_VB), seg(SEG_OB), seg(SEG_ZB),
                  pl.BlockSpec((8, rows), lambda b, c: (0, b * steps + c)),
                  pl.BlockSpec((1, D_MODEL), lambda b, c: (0, 0))],
        out_specs=[pl.BlockSpec((rows, D_MODEL), lambda b, c: (b * steps + c, 0)),
                   pl.BlockSpec((1, N_HEADS, HEAD_W, HEAD_W), lambda b, c: (b, 0, 0, 0)),
                   pl.BlockSpec((1, N_HEADS, HEAD_W), lambda b, c: (b, 0, 0)),
                   pl.BlockSpec((1, 8, CHUNK), lambda b, c: (b, 0, 0))],
        out_shape=[jax.ShapeDtypeStruct((n, D_MODEL), BF16),
                   jax.ShapeDtypeStruct((batch, N_HEADS, HEAD_W, HEAD_W), F32),
                   jax.ShapeDtypeStruct((batch, N_HEADS, HEAD_W), F32),
                   jax.ShapeDtypeStruct((batch, 8, CHUNK), F32)],
        scratch_shapes=[pltpu.VMEM((N_HEADS, HEAD_W, HEAD_W), F32), pltpu.VMEM((8, HEAD_W), F32),
                        pltpu.VMEM((8, CHUNK), F32)],
        compiler_params=_params("arbitrary", "arbitrary"),
        name="mlstm_prompt",
    )(p_act, p_act, p_act, p_act, p_act, gates, mng)


def _mlstm_sample_kernel(q_ref, k_ref, v_ref, o_ref, z_ref, g_ref, c0_ref, n0_ref, m0_ref, mng_ref, *rest, ts):
    yb_ref, c_out, n_out, m_out = rest[-4:]
    h = pl.program_id(1)
    n_seq = CHUNK // ts
    same, causal, eye, si = _block_causal_masks(ts)
    ig_row = g_ref[pl.ds(h, 1), :]
    lf_row = _log_sigmoid(g_ref[pl.ds(N_HEADS + h, 1), :])
    b_col = jnp.sum(jnp.where(causal, lf_row, 0.0), axis=1, keepdims=True)
    b_row = _col_to_row(b_col, eye)
    a_row = ig_row - b_row
    m_col = m0_ref[0]
    d_log = jnp.where(causal, b_col + a_row, -jnp.inf)
    inter = b_col + m_col
    m_t = jnp.maximum(inter, jnp.max(d_log, axis=1, keepdims=True))
    w_intra = jnp.exp(d_log - m_t)
    w_inter = jnp.exp(inter - m_t)
    q = q_ref[...]
    k = k_ref[...]
    v = v_ref[...]
    s = _dot_nt(q, k) * w_intra
    num_inter = jnp.concatenate(
        [_dot(q[b * ts:(b + 1) * ts, :], c0_ref[0, b, 0].astype(BF16)) for b in range(n_seq)], axis=0)
    n_tok = jnp.concatenate([jnp.broadcast_to(n0_ref[b, 0], (ts, HEAD_W)) for b in range(n_seq)], axis=0)
    num = w_inter * num_inter + _dot(s.astype(BF16), v)
    qn = jnp.sum(q.astype(F32) * n_tok, axis=1, keepdims=True)
    den = w_inter * qn + jnp.sum(s, axis=1, keepdims=True)
    hh = num / jnp.maximum(jnp.abs(den), jnp.exp(-m_t))
    yb_ref[...] = _head_out(hh, o_ref[...], z_ref[...], mng_ref[...])

    last_in_seq = same & ((si & (ts - 1)) == ts - 1)
    b_end_col = jnp.sum(jnp.where(last_in_seq, b_row, 0.0), axis=1, keepdims=True)
    b_end_row = _col_to_row(b_end_col, eye)
    log_end_row = b_end_row + a_row
    seq_max_col = jnp.max(jnp.where(same, log_end_row, -jnp.inf), axis=1, keepdims=True)
    m_new_col = jnp.maximum(b_end_col + m_col, seq_max_col)
    decay_col = jnp.exp(b_end_col + m_col - m_new_col)
    m_new_row = _col_to_row(m_new_col, eye)
    w_end_col = _row_to_col(jnp.exp(log_end_row - m_new_row), eye)
    kw = k.astype(F32) * w_end_col
    kw_t = kw.T
    lane_seq = lax.broadcasted_iota(jnp.int32, (1, CHUNK), 1) >> (ts.bit_length() - 1)
    for b in range(n_seq):
        decay = decay_col[b * ts:b * ts + 1, :]
        kw_b = jnp.where(lane_seq == b, kw_t, 0.0).astype(BF16)
        c_out[0, b, 0] = decay * c0_ref[0, b, 0] + _dot(kw_b, v)
        n_out[b, 0] = decay * n0_ref[b, 0] + jnp.sum(kw[b * ts:(b + 1) * ts, :], axis=0, keepdims=True)
        m_out[b, 0] = jnp.broadcast_to(m_new_col[b * ts:b * ts + 1, :], (1, CHUNK))


def _mlstm_sample_call(p_act, gates, c0_all, n0, m0_tok, mng, c_new_all, *, layer, ts):
    n = p_act.shape[0]
    db = n // ts
    n_seq = CHUNK // ts

    def seg(s):
        return pl.BlockSpec((CHUNK, HEAD_W), lambda i, h: (i, s * N_HEADS + h))

    state_c = pl.BlockSpec((1, n_seq, 1, HEAD_W, HEAD_W), lambda i, h: (layer, i, h, 0, 0))
    state_n = pl.BlockSpec((n_seq, 1, 1, HEAD_W), lambda i, h: (i, h, 0, 0))
    in_specs = [seg(SEG_Q), seg(SEG_K), seg(SEG_VB), seg(SEG_OB), seg(SEG_ZB),
                pl.BlockSpec((8, CHUNK), lambda i, h: (0, i)),
                state_c, state_n,
                pl.BlockSpec((1, CHUNK, 1), lambda i, h: (h, i, 0)),
                pl.BlockSpec((1, HEAD_W), lambda i, h: (0, h))]
    args = [p_act, p_act, p_act, p_act, p_act, gates, c0_all, n0, m0_tok, mng]
    aliases = {}
    if c_new_all is not None:
        aliases = {len(args): 1}
        in_specs.append(pl.BlockSpec(memory_space=pl.ANY))
        args.append(c_new_all)
    return pl.pallas_call(
        functools.partial(_mlstm_sample_kernel, ts=ts),
        grid=(n // CHUNK, N_HEADS),
        in_specs=in_specs,
        out_specs=[pl.BlockSpec((CHUNK, HEAD_W), lambda i, h: (i, h)),
                   state_c, state_n,
                   pl.BlockSpec((n_seq, 1, 1, CHUNK), lambda i, h: (i, h, 0, 0))],
        out_shape=[jax.ShapeDtypeStruct((n, D_MODEL), BF16),
                   jax.ShapeDtypeStruct(c0_all.shape, F32),
                   jax.ShapeDtypeStruct((db, N_HEADS, 1, HEAD_W), F32),
                   jax.ShapeDtypeStruct((db, N_HEADS, 1, CHUNK), F32)],
        input_output_aliases=aliases,
        compiler_params=_params("arbitrary", "arbitrary"),
        name="mlstm_sample",
    )(*args)


def _merge_kernel(ya_ref, yb_ref, yc_ref, ga_ref, gb_ref, gc_ref, x_ref, gate_ref, wa_ref, wb_ref, wc_ref,
                  wo_ref, fng_ref, o_ref, *, per_token, final):
    merged = (ga_ref[...].astype(F32) * _dot(ya_ref[...], wa_ref[...])
              + gb_ref[...].astype(F32) * _dot(yb_ref[...], wb_ref[...])
              + gc_ref[...].astype(F32) * _dot(yc_ref[...], wc_ref[...]))
    gate = gate_ref[...] if per_token else gate_ref[0]
    xn = x_ref[...] + gate * _dot(merged.astype(BF16), wo_ref[...])
    if final:
        ms = jnp.mean(xn * xn, axis=-1, keepdims=True)
        xn = (xn * lax.rsqrt(ms + EPS)) * fng_ref[...]
    o_ref[...] = xn


def _merge_call(ya, yb, yc, p_act, x, gate, wa, wb, wc, wo, fng, *, tm, rows_per_mod, final):
    n = x.shape[0]
    per_token = rows_per_mod is None
    tok_spec = pl.BlockSpec((tm, D_MODEL), lambda i: (i, 0))
    if per_token:
        gate_spec = tok_spec
    else:
        tiles_per_mod = rows_per_mod // tm
        gate_spec = pl.BlockSpec((1, 1, D_MODEL), lambda i: (i // tiles_per_mod, 0, 0))

    def seg(s):
        return pl.BlockSpec((tm, D_MODEL), lambda i: (i, s))

    w_spec = pl.BlockSpec((D_MODEL, D_MODEL), lambda i: (0, 0))
    return pl.pallas_call(
        functools.partial(_merge_kernel, per_token=per_token, final=final),
        grid=(n // tm,),
        in_specs=[tok_spec, tok_spec, tok_spec, seg(SEG_GA), seg(SEG_GB), seg(SEG_GC), tok_spec, gate_spec,
                  w_spec, w_spec, w_spec, w_spec, pl.BlockSpec((1, D_MODEL), lambda i: (0, 0))],
        out_specs=tok_spec,
        out_shape=jax.ShapeDtypeStruct((n, D_MODEL), F32),
        compiler_params=_params("arbitrary"),
        name="merge_out",
    )(ya, yb, yc, p_act, p_act, p_act, x, gate, wa, wb, wc, wo, fng)


def _row_tile(n, target):
    tile = min(n, target)
    assert n % tile == 0 and tile % CHUNK == 0, (n, tile)
    return tile


def kernel(x_prompt, x_sample, c_prompt, c_sample, state_mlstm_C, state_mlstm_n, state_mlstm_m, state_pool, w_mod, b_mod, norm_g, w_in, b_if, sgu_ln_g, sgu_ln_b, w_sgu, b_sgu, mlstm_norm_g, w_pool, pool_scale, w_br_a, w_br_b, w_br_c, w_out, final_norm_g):
    batch, seq, _ = x_prompt.shape
    db, ts, _ = x_sample.shape
    depth = w_mod.shape[0]
    assert seq % CHUNK == 0 and CHUNK % ts == 0 and ts & (ts - 1) == 0 and (db * ts) % CHUNK == 0
    n_p, n_s = batch * seq, db * ts

    n_mod = batch + db
    c_all = jnp.concatenate([c_prompt, c_sample, jnp.zeros((-n_mod % 8, D_MODEL), F32)], axis=0)
    mod = _mod_call(c_all, w_mod, b_mod)

    w_in_t = jnp.swapaxes(w_in, 1, 2)
    w_bf, w_gt = _wcast_call(w_in_t)

    xp = x_prompt.reshape(n_p, D_MODEL)
    xs = x_sample.reshape(n_s, D_MODEL)
    tm_p = _row_tile(seq, 2048)
    tm_s = _row_tile(n_s, 512)
    tc_p = _row_tile(seq, 512)
    mlstm_chunks = min(seq // CHUNK, 8)
    assert (seq // CHUNK) % mlstm_chunks == 0
    tb_s = min(db, 32)
    fng = final_norm_g.reshape(1, D_MODEL)
    n_rep = CHUNK // ts

    outs = {k: [] for k in ("cp", "np", "mp", "bp", "ns", "ms", "bs", "vs")}
    c_s_all = None
    for l in range(depth):
        bif = b_if[l].reshape(8, 1)
        ng = norm_g[l].reshape(1, D_MODEL)
        lng = sgu_ln_g[l].reshape(1, D_MODEL)
        lnb = sgu_ln_b[l].reshape(1, D_MODEL)
        mng = mlstm_norm_g[l].reshape(1, D_MODEL)
        pscale = pool_scale[l].reshape(1, D_MODEL)
        wpool = w_pool[l].astype(BF16)
        wa, wb, wc, wo = (w[l].astype(BF16) for w in (w_br_a, w_br_b, w_br_c, w_out))
        final = l == depth - 1

        mod_p = mod[l, :batch].reshape(batch, 1, 3 * D_MODEL)
        shift_p, scale_p, gate_p = (mod_p[..., i * D_MODEL:(i + 1) * D_MODEL] for i in range(3))
        p_act, gates, pool_p = _inproj_call(xp, shift_p, scale_p, ng, w_bf, w_gt[l], bif, lng, lnb,
                                            layer=l, tm=tm_p, rows_per_mod=seq, emit_f32=False)
        yb, c_p, n_p_state, m_p = _mlstm_prompt_call(p_act, gates, mng, batch=batch, seq=seq,
                                                     n_chunks=mlstm_chunks)
        xp = _mix_merge_prompt_call(p_act, yb, xp, gate_p, w_sgu[l], b_sgu[l][:, :, None], wpool, pscale,
                                    wa, wb, wc, wo, fng, seq=seq, tc=tc_p, final=final)
        outs["cp"].append(c_p)
        outs["np"].append(n_p_state)
        outs["mp"].append(m_p[:, :N_HEADS, 0])
        outs["bp"].append(pool_p[:, 1:, :])

        mod_s = jnp.repeat(mod[l, batch:batch + db], ts, axis=0)
        shift_s, scale_s, gate_s = (mod_s[:, i * D_MODEL:(i + 1) * D_MODEL] for i in range(3))
        p_act, gates, va_f32, p_f32 = _inproj_call(xs, shift_s, scale_s, ng, w_bf, w_gt[l], bif, lng, lnb,
                                                   layer=l, tm=tm_s, rows_per_mod=None, emit_f32=True)
        w_sgu_tiled = jnp.tile(w_sgu[l][:, :ts, :ts], (1, n_rep, n_rep))
        bs_tiled = jnp.tile(b_sgu[l][:, :ts], (1, n_rep))[:, :, None]
        buf_pad = jnp.pad(state_pool[l], ((0, 0), (POOL_PAD - POOL_BUF, 0), (0, 0)))
        ya, yc, pool_s = _sgu_pool_sample_call(p_act, p_f32, buf_pad, w_sgu_tiled, bs_tiled, wpool, pscale,
                                               tb=tb_s, ts=ts)
        m0_tok = jnp.repeat(state_mlstm_m[l], ts, axis=0).T[:, :, None]
        yb, c_s_all, n_s_state, m_s = _mlstm_sample_call(p_act, gates, state_mlstm_C,
                                                         state_mlstm_n[l][:, :, None, :], m0_tok, mng, c_s_all,
                                                         layer=l, ts=ts)
        xs = _merge_call(ya, yb, yc, p_act, xs, gate_s, wa, wb, wc, wo, fng,
                         tm=_row_tile(n_s, 512), rows_per_mod=None, final=final)
        outs["ns"].append(n_s_state[:, :, 0, :])
        outs["ms"].append(m_s[:, :, 0, 0])
        outs["bs"].append(pool_s)
        outs["vs"].append(va_f32.reshape(db, ts, D_MODEL))

    y_prompt = xp.reshape(batch, seq, D_MODEL)
    y_sample = xs.reshape(db, ts, D_MODEL)
    return (y_prompt, y_sample,
            jnp.stack(outs["cp"]), jnp.stack(outs["np"]), jnp.stack(outs["mp"]), jnp.stack(outs["bp"]),
            c_s_all, jnp.stack(outs["ns"]), jnp.stack(outs["ms"]), jnp.stack(outs["bs"]),
            jnp.stack(outs["vs"]))
```

```python
import functools

import jax
import jax.numpy as jnp
from jax import lax
from jax.experimental import pallas as pl
from jax.experimental.pallas import tpu as pltpu

F32 = jnp.float32
BF16 = jnp.bfloat16

D_MODEL = 1024
N_HEADS = 4
HEAD_W = D_MODEL // N_HEADS
CHUNK = 128
POOL_WINDOWS = (2, 4, 8, 16)
POOL_BUF = 15
POOL_PAD = 16
POOL_HALO = 24
PAST_LEN = 16384
EPS = 1e-6
N_SEG = 13
SEG_GA, SEG_GB, SEG_GC, SEG_U, SEG_VA, SEG_ZA, SEG_Q, SEG_K, SEG_VB, SEG_OB, SEG_ZB, SEG_P, SEG_ZC = range(N_SEG)
N_ALIGNED_SEG = 11
N_GATE_COLS = 2 * N_HEADS
VMEM_LIMIT_BYTES = 56 * 1024 * 1024


def _params(*semantics):
    return pltpu.CompilerParams(dimension_semantics=semantics, vmem_limit_bytes=VMEM_LIMIT_BYTES)


def _dot(a, b):
    return jnp.dot(a, b, preferred_element_type=F32)


def _dot_nt(a, b):
    return lax.dot_general(a, b, (((1,), (1,)), ((), ())), preferred_element_type=F32)


def _sigmoid(x):
    return 0.5 * jnp.tanh(0.5 * x) + 0.5


def _silu(x):
    return x * _sigmoid(x)


def _log_sigmoid(x):
    return jnp.minimum(x, 0.0) - jnp.log(1.0 + jnp.exp(-jnp.abs(x)))


def _gelu(x):
    return jax.nn.gelu(x, approximate=True)


def _mod_kernel(c_ref, w_ref, b_ref, o_ref):
    c = c_ref[...]
    o_ref[0] = _dot(_silu(c).astype(BF16), w_ref[0].astype(BF16)) + b_ref[0]


def _mod_call(c_all, w_mod, b_mod):
    depth = w_mod.shape[0]
    rows = c_all.shape[0]
    return pl.pallas_call(
        _mod_kernel,
        grid=(depth, 3),
        in_specs=[pl.BlockSpec((rows, D_MODEL), lambda l, j: (0, 0)),
                  pl.BlockSpec((1, D_MODEL, D_MODEL), lambda l, j: (l, 0, j)),
                  pl.BlockSpec((1, 1, D_MODEL), lambda l, j: (l, 0, j))],
        out_specs=pl.BlockSpec((1, rows, D_MODEL), lambda l, j: (l, 0, j)),
        out_shape=jax.ShapeDtypeStruct((depth, rows, 3 * D_MODEL), F32),
        compiler_params=_params("arbitrary", "arbitrary"),
        name="adaln_mod",
    )(c_all, w_mod, b_mod.reshape(depth, 1, 3 * D_MODEL))


def _wcast_kernel(w_ref, wnext_ref, o_ref, gate_ref):
    j = pl.program_id(1)

    @pl.when(j < N_ALIGNED_SEG)
    def _():
        o_ref[0] = w_ref[0].T.astype(BF16)

    @pl.when(j == N_ALIGNED_SEG - 1)
    def _():
        gate_ref[...] = wnext_ref[...].astype(BF16)

    @pl.when(j >= N_ALIGNED_SEG)
    def _():
        rows = jnp.concatenate([w_ref[0][N_GATE_COLS:, :], wnext_ref[0]], axis=0)
        o_ref[0] = rows.T.astype(BF16)


def _wcast_call(w_in_t):
    depth = w_in_t.shape[0]
    next_blocks = D_MODEL // N_GATE_COLS
    return pl.pallas_call(
        _wcast_kernel,
        grid=(depth, N_SEG),
        in_specs=[pl.BlockSpec((1, D_MODEL, D_MODEL), lambda l, j: (l, j, 0)),
                  pl.BlockSpec((1, N_GATE_COLS, D_MODEL), lambda l, j: (l, (j + 1) * next_blocks, 0))],
        out_specs=[pl.BlockSpec((1, D_MODEL, D_MODEL), lambda l, j: (l, 0, j)),
                   pl.BlockSpec((1, N_GATE_COLS, D_MODEL), lambda l, j: (l, 0, 0))],
        out_shape=[jax.ShapeDtypeStruct((depth, D_MODEL, N_SEG * D_MODEL), BF16),
                   jax.ShapeDtypeStruct((depth, N_GATE_COLS, D_MODEL), BF16)],
        compiler_params=_params("arbitrary", "arbitrary"),
        name="w_in_bf16",
    )(w_in_t, w_in_t)


def _square_cast_kernel(a_ref, b_ref, c_ref, d_ref, ao_ref, bo_ref, co_ref, do_ref):
    for src, dst in ((a_ref, ao_ref), (b_ref, bo_ref), (c_ref, co_ref), (d_ref, do_ref)):
        dst[...] = src[...].astype(BF16)


def _square_cast_call(*weights):
    depth = weights[0].shape[0]
    rows = D_MODEL // 4
    spec = pl.BlockSpec((1, rows, D_MODEL), lambda l, r: (l, r, 0))
    return pl.pallas_call(
        _square_cast_kernel,
        grid=(depth, D_MODEL // rows),
        in_specs=[spec] * 4,
        out_specs=[spec] * 4,
        out_shape=[jax.ShapeDtypeStruct(w.shape, BF16) for w in weights],
        compiler_params=_params("arbitrary", "arbitrary"),
        name="branch_w_bf16",
    )(*weights)


def _inproj_kernel(x_ref, shift_ref, scale_ref, ng_ref, w_ref, wgt_ref, bif_ref, lng_ref, lnb_ref,
                   p_ref, g_ref, *rest, mod_rows, emit_f32, tm, sub, tiles_per_seq):
    if emit_f32:
        va_ref, pf_ref, h_scr = rest
    else:
        pool_ref, h_scr, pbuf, *level_bufs = rest
    j = pl.program_id(1)

    def normed_rows(rows):
        x = x_ref[rows, :]
        ms = jnp.mean(x * x, axis=-1, keepdims=True)
        y = (x * lax.rsqrt(ms + EPS)) * ng_ref[...]
        if mod_rows >= tm:
            sc, sh = scale_ref[0], shift_ref[0]
        else:
            n_rows = rows.stop - rows.start
            groups = slice(rows.start // mod_rows, rows.stop // mod_rows)
            sc, sh = (jnp.broadcast_to(ref[groups], (n_rows // mod_rows, mod_rows, D_MODEL)).reshape(n_rows, D_MODEL)
                      for ref in (scale_ref, shift_ref))
        h = (y * (1.0 + sc) + sh).astype(BF16)
        h_scr[rows, :] = h
        g_ref[:, rows] = _dot_nt(wgt_ref[...], h) + bif_ref[...]
        return h

    def project(epilogue, first=False):
        w = w_ref[0]
        step = max(sub // 2, CHUNK) if first else sub
        for r0 in range(0, tm, step):
            rows = slice(r0, r0 + step)
            h = normed_rows(rows) if first else h_scr[rows, :]
            epilogue(rows, _dot(h, w))

    def store(fn):
        def epilogue(rows, acc):
            p_ref[rows, :] = fn(acc).astype(BF16)
        return epilogue

    def va_epilogue(rows, acc):
        gl = _gelu(acc)
        mu = jnp.mean(gl, axis=-1, keepdims=True)
        ctr = gl - mu
        var = jnp.mean(ctr * ctr, axis=-1, keepdims=True)
        y = ctr * lax.rsqrt(var + EPS) * lng_ref[...] + lnb_ref[...]
        p_ref[rows, :] = y.astype(BF16)
        if emit_f32:
            va_ref[rows, :] = y

    def p_epilogue(rows, acc):
        p_ref[rows, :] = acc.astype(BF16)
        pf_ref[rows, :] = acc

    def pooled_epilogue(rows, acc):
        n_rows = rows.stop - rows.start
        tile_in_seq = pl.program_id(0) % tiles_per_seq
        sums = (pbuf,) + tuple(level_bufs)
        if rows.start == 0:
            @pl.when(tile_in_seq == 0)
            def _():
                for buf in sums:
                    buf[0:POOL_HALO, :] = jnp.zeros((POOL_HALO, buf.shape[1]), F32)
        pbuf[POOL_HALO:POOL_HALO + n_rows, :] = acc
        pos = tile_in_seq * tm + rows.start + lax.broadcasted_iota(jnp.int32, (n_rows, 1), 0)
        live = slice(POOL_HALO, POOL_HALO + n_rows)
        for g, w in enumerate(POOL_WINDOWS):
            prev = sums[g]
            shift = w // 2
            span = slice(8, POOL_HALO + n_rows)
            shifted = slice(8 - shift, POOL_HALO + n_rows - shift)
            if g + 1 < len(POOL_WINDOWS):
                sums[g + 1][span, :] = prev[span, HEAD_W:] + prev[shifted, HEAD_W:]
            total = prev[live, 0:HEAD_W] + prev[slice(live.start - shift, live.stop - shift), 0:HEAD_W]
            cnt = jnp.minimum(pos + 1, w).astype(F32)
            cols = slice(g * HEAD_W, (g + 1) * HEAD_W)
            p_ref[rows, cols] = (total / cnt - pbuf[live, cols]).astype(BF16)
        tail = pbuf[n_rows + POOL_HALO - POOL_PAD:n_rows + POOL_HALO, :]
        pool_ref[0] = tail
        pbuf[POOL_HALO - POOL_PAD:POOL_HALO, :] = tail

    pl.when(j == SEG_GA)(lambda: project(store(_sigmoid), first=True))
    pl.when((j == SEG_GB) | (j == SEG_GC) | (j == SEG_OB))(lambda: project(store(_sigmoid)))
    pl.when(j == SEG_U)(lambda: project(store(_gelu)))
    pl.when(j == SEG_VA)(lambda: project(va_epilogue))
    pl.when((j == SEG_ZA) | (j == SEG_ZB) | (j == SEG_ZC))(lambda: project(store(_silu)))
    pl.when((j == SEG_Q) | (j == SEG_VB))(lambda: project(store(lambda a: a)))
    pl.when(j == SEG_K)(lambda: project(store(lambda a: a * HEAD_W ** -0.5)))
    pl.when(j == SEG_P)(lambda: project(p_epilogue if emit_f32 else pooled_epilogue))


def _inproj_call(x, shift, scale, norm_g, w_bf, w_gt, b_if, ln_g, ln_b, *, layer, tm, rows_per_mod, emit_f32):
    n = x.shape[0]
    sub = min(tm, max(tm // 4, 2 * CHUNK))
    tiles_per_seq = None if emit_f32 else rows_per_mod // tm
    vec_spec = pl.BlockSpec((1, D_MODEL), lambda i, j: (0, 0))
    row_spec = pl.BlockSpec((tm, D_MODEL), lambda i, j: (i, 0))
    out_shape = [jax.ShapeDtypeStruct((n, N_SEG * D_MODEL), BF16), jax.ShapeDtypeStruct((8, n), F32)]
    out_specs = [pl.BlockSpec((tm, D_MODEL), lambda i, j: (i, j)), pl.BlockSpec((8, tm), lambda i, j: (0, i))]
    scratch = [pltpu.VMEM((tm, D_MODEL), BF16)]
    if emit_f32:
        mod_spec = pl.BlockSpec((tm // rows_per_mod, 1, D_MODEL), lambda i, j: (i, 0, 0))
        out_shape += [jax.ShapeDtypeStruct((n, D_MODEL), F32)] * 2
        out_specs += [row_spec, row_spec]
    else:
        mod_spec = pl.BlockSpec((1, 1, D_MODEL), lambda i, j: (i // tiles_per_seq, 0, 0))
        out_shape.append(jax.ShapeDtypeStruct((n // rows_per_mod, POOL_PAD, D_MODEL), F32))
        out_specs.append(pl.BlockSpec((1, POOL_PAD, D_MODEL), lambda i, j: (i // tiles_per_seq, 0, 0)))
        scratch += [pltpu.VMEM((POOL_HALO + sub, D_MODEL - g * HEAD_W), F32) for g in range(len(POOL_WINDOWS))]
    return pl.pallas_call(
        functools.partial(_inproj_kernel, mod_rows=rows_per_mod, emit_f32=emit_f32, tm=tm, sub=sub,
                          tiles_per_seq=tiles_per_seq),
        grid=(n // tm, N_SEG),
        in_specs=[row_spec, mod_spec, mod_spec, vec_spec,
                  pl.BlockSpec((1, D_MODEL, D_MODEL), lambda i, j: (layer, 0, j)),
                  pl.BlockSpec((8, D_MODEL), lambda i, j: (0, 0)),
                  pl.BlockSpec((8, 1), lambda i, j: (0, 0)),
                  vec_spec, vec_spec],
        out_specs=out_specs,
        out_shape=out_shape,
        scratch_shapes=scratch,
        compiler_params=_params("arbitrary", "arbitrary"),
        name="inproj",
    )(x, shift, scale, norm_g, w_bf, w_gt, b_if, ln_g, ln_b)


def _block_causal_masks(block_len):
    ti = lax.broadcasted_iota(jnp.int32, (CHUNK, CHUNK), 0)
    si = lax.broadcasted_iota(jnp.int32, (CHUNK, CHUNK), 1)
    shift = block_len.bit_length() - 1
    same = (ti >> shift) == (si >> shift)
    return same, same & (si <= ti), si == ti, si


def _sgu_rows(gu_ref, va_ref, sza_ref, wsgu_ref, bs_ref, ya_ref, causal, row_lo, row_hi):
    for g in range(N_HEADS):
        cols = slice(g * HEAD_W, (g + 1) * HEAD_W)
        wg = jnp.where(causal, wsgu_ref[g], 0.0).astype(BF16)
        bias = bs_ref[g]
        for r0 in range(row_lo, row_hi, CHUNK):
            rows = slice(r0, r0 + CHUNK)
            s = _dot(wg, va_ref[rows, cols]) + bias
            ya = gu_ref[rows, cols].astype(F32) * s * sza_ref[rows, cols].astype(F32)
            ya_ref[rows, cols] = ya.astype(BF16)


def _head_out(hh, o_blk, z_blk, mng_blk):
    hb = o_blk.astype(F32) * hh
    mu = jnp.mean(hb, axis=-1, keepdims=True)
    ctr = hb - mu
    var = jnp.mean(ctr * ctr, axis=-1, keepdims=True)
    y = ctr * lax.rsqrt(var + EPS) * mng_blk
    return (y * z_blk.astype(F32)).astype(BF16)


def _row_to_col(row, eye):
    return jnp.sum(jnp.where(eye, row, 0.0), axis=1, keepdims=True)


def _col_to_row(col, eye):
    return jnp.sum(jnp.where(eye, col, 0.0), axis=0, keepdims=True)


def _mix_merge_prompt_kernel(gu_ref, va_ref, sza_ref, pooled_ref, szc_ref, ga_ref, gb_ref, gc_ref, yb_ref, x_ref,
                             gate_ref, wsgu_ref, bs_ref, wpool_ref, pscale_ref, wa_ref, wb_ref, wc_ref, wo_ref,
                             fng_ref, o_ref, ya_ref, yc_ref, *, tc, sub, final):
    _, causal, _, _ = _block_causal_masks(CHUNK)

    def pool_project(rows):
        for g in range(N_HEADS):
            cols = slice(g * HEAD_W, (g + 1) * HEAD_W)
            pm = _dot(pooled_ref[rows, cols], wpool_ref[g])
            yc_ref[rows, cols] = (pm * pscale_ref[:, cols] * szc_ref[rows, cols].astype(F32)).astype(BF16)

    def gated(g_ref, y_ref, w_ref, rows):
        return g_ref[rows, :].astype(F32) * _dot(y_ref[rows, :], w_ref[...])

    for r0 in range(0, tc, sub):
        rows = slice(r0, r0 + sub)
        part_b = gated(gb_ref, yb_ref, wb_ref, rows)
        _sgu_rows(gu_ref, va_ref, sza_ref, wsgu_ref, bs_ref, ya_ref, causal, r0, r0 + sub)
        part_ab = gated(ga_ref, ya_ref, wa_ref, rows) + part_b
        pool_project(rows)
        merged = part_ab + gated(gc_ref, yc_ref, wc_ref, rows)
        xn = x_ref[rows, :] + gate_ref[0] * _dot(merged.astype(BF16), wo_ref[...])
        if final:
            ms = jnp.mean(xn * xn, axis=-1, keepdims=True)
            xn = (xn * lax.rsqrt(ms + EPS)) * fng_ref[...]
        o_ref[rows, :] = xn


def _mix_merge_prompt_call(p_act, yb, x, gate, w_sgu, bs_col, w_pool, pool_scale, wa, wb, wc, wo, fng,
                           *, seq, tc, final):
    n = x.shape[0]
    tiles_per_seq = seq // tc

    def seg(s):
        return pl.BlockSpec((tc, D_MODEL), lambda i: (i, s))

    tok_spec = pl.BlockSpec((tc, D_MODEL), lambda i: (i, 0))
    vec_spec = pl.BlockSpec((1, D_MODEL), lambda i: (0, 0))
    w_spec = pl.BlockSpec((D_MODEL, D_MODEL), lambda i: (0, 0), pipeline_mode=pl.Buffered(1))
    return pl.pallas_call(
        functools.partial(_mix_merge_prompt_kernel, tc=tc, sub=min(tc, 256), final=final),
        grid=(n // tc,),
        in_specs=[seg(SEG_U), seg(SEG_VA), seg(SEG_ZA), seg(SEG_P), seg(SEG_ZC),
                  seg(SEG_GA), seg(SEG_GB), seg(SEG_GC), tok_spec, tok_spec,
                  pl.BlockSpec((1, 1, D_MODEL), lambda i: (i // tiles_per_seq, 0, 0)),
                  pl.BlockSpec((N_HEADS, CHUNK, CHUNK), lambda i: (0, 0, 0)),
                  pl.BlockSpec((N_HEADS, CHUNK, 1), lambda i: (0, 0, 0)),
                  pl.BlockSpec((N_HEADS, HEAD_W, HEAD_W), lambda i: (0, 0, 0)),
                  vec_spec, w_spec, w_spec, w_spec, w_spec, vec_spec],
        out_specs=tok_spec,
        out_shape=jax.ShapeDtypeStruct((n, D_MODEL), F32),
        scratch_shapes=[pltpu.VMEM((tc, D_MODEL), BF16), pltpu.VMEM((tc, D_MODEL), BF16)],
        compiler_params=_params("arbitrary"),
        name="mix_merge_prompt",
    )(p_act, p_act, p_act, p_act, p_act, p_act, p_act, p_act, yb, x, gate, w_sgu, bs_col, w_pool, pool_scale,
      wa, wb, wc, wo, fng)


def _sgu_pool_sample_kernel(gu_ref, va_ref, sza_ref, szc_ref, pf_ref, buf_ref, wsgu_ref, bs_ref, wpool_ref,
                            pscale_ref, ya_ref, yc_ref, nbuf_ref, fscr, *, tb, ts, n_past):
    rows = tb * ts
    _, causal, _, _ = _block_causal_masks(ts)
    _sgu_rows(gu_ref, va_ref, sza_ref, wsgu_ref, bs_ref, ya_ref, causal, 0, rows)

    fscr[:, POOL_PAD - POOL_BUF:POOL_PAD, :] = buf_ref[0]
    fscr[:, POOL_PAD:, :] = pf_ref[...]
    pos = n_past + lax.broadcasted_iota(jnp.int32, (1, ts, 1), 1)
    for g, w in enumerate(POOL_WINDOWS):
        cols = slice(g * HEAD_W, (g + 1) * HEAD_W)
        cur = fscr[:, POOL_PAD:POOL_PAD + ts, cols]
        acc = cur
        for k in range(1, w):
            acc = acc + fscr[:, POOL_PAD - k:POOL_PAD - k + ts, cols]
        cnt = jnp.minimum(pos + 1, w).astype(F32)
        pooled = (acc / cnt - cur).reshape(rows, HEAD_W)
        pm = _dot(pooled.astype(BF16), wpool_ref[g])
        yc_ref[:, cols] = (pm * pscale_ref[:, cols] * szc_ref[:, cols].astype(F32)).astype(BF16)
    nbuf_ref[...] = fscr[:, POOL_PAD + ts - POOL_BUF:POOL_PAD + ts, :]


def _sgu_pool_sample_call(p_act, p_f32, pool_buf_all, w_sgu_tiled, bs_col_tiled, w_pool, pool_scale,
                          *, layer, tb, ts):
    n = p_act.shape[0]
    db = n // ts
    rows = tb * ts

    def seg(s):
        return pl.BlockSpec((rows, D_MODEL), lambda i: (i, s))

    tok_spec = pl.BlockSpec((rows, D_MODEL), lambda i: (i, 0))
    return pl.pallas_call(
        functools.partial(_sgu_pool_sample_kernel, tb=tb, ts=ts, n_past=PAST_LEN),
        grid=(db // tb,),
        in_specs=[seg(SEG_U), seg(SEG_VA), seg(SEG_ZA), seg(SEG_ZC),
                  pl.BlockSpec((tb, ts, D_MODEL), lambda i: (i, 0, 0)),
                  pl.BlockSpec((1, tb, POOL_BUF, D_MODEL), lambda i: (layer, i, 0, 0)),
                  pl.BlockSpec((N_HEADS, CHUNK, CHUNK), lambda i: (0, 0, 0)),
                  pl.BlockSpec((N_HEADS, CHUNK, 1), lambda i: (0, 0, 0)),
                  pl.BlockSpec((N_HEADS, HEAD_W, HEAD_W), lambda i: (0, 0, 0)),
                  pl.BlockSpec((1, D_MODEL), lambda i: (0, 0))],
        out_specs=[tok_spec, tok_spec, pl.BlockSpec((tb, POOL_BUF, D_MODEL), lambda i: (i, 0, 0))],
        out_shape=[jax.ShapeDtypeStruct((n, D_MODEL), BF16), jax.ShapeDtypeStruct((n, D_MODEL), BF16),
                   jax.ShapeDtypeStruct((db, POOL_BUF, D_MODEL), F32)],
        scratch_shapes=[pltpu.VMEM((tb, POOL_PAD + ts, D_MODEL), F32)],
        compiler_params=_params("arbitrary"),
        name="sgu_pool_sample",
    )(p_act, p_act, p_act, p_act, p_f32.reshape(db, ts, D_MODEL), pool_buf_all, w_sgu_tiled, bs_col_tiled, w_pool,
      pool_scale)


def _mlstm_prompt_kernel(q_ref, k_ref, v_ref, o_ref, z_ref, g_ref, mng_ref,
                         yb_ref, c_out, n_out, m_out, c_scr, n_scr, m_scr, *, n_chunks):
    step = pl.program_id(1)

    @pl.when(step == 0)
    def _():
        c_scr[...] = jnp.zeros(c_scr.shape, F32)
        n_scr[...] = jnp.zeros(n_scr.shape, F32)
        m_scr[...] = jnp.zeros(m_scr.shape, F32)

    _, causal, eye, _ = _block_causal_masks(CHUNK)
    heads = range(N_HEADS)
    probs = [(ci, h) for ci in range(n_chunks) for h in heads]

    def rows(p):
        return slice(p[0] * CHUNK, (p[0] + 1) * CHUNK)

    def cols(p):
        return slice(p[1] * HEAD_W, (p[1] + 1) * HEAD_W)

    def each(fn, keys=probs):
        return {p: fn(p) for p in keys}

    lower = jnp.where(causal, 1.0, 0.0).astype(BF16)
    upper = lower.T
    ones_sq = jnp.ones((CHUNK, CHUNK), BF16)
    mean_w = jnp.full((HEAD_W, CHUNK), 1.0 / HEAD_W, BF16)

    def split(x):
        hi = x.astype(BF16)
        return hi, (x - hi.astype(F32)).astype(BF16)

    def wide(x):
        return jnp.concatenate([x, x], axis=1)

    ig_all = g_ref[0:N_HEADS, :]
    lf_all = _log_sigmoid(g_ref[N_HEADS:2 * N_HEADS, :])
    b_col, r_col, num_intra, den_intra, b_end, r_end, c_upd, n_upd = ({} for _ in range(8))

    def independent(ps):
        lf_parts = each(lambda p: split(lf_all[p[1]:p[1] + 1, rows(p)]), ps)
        b_col.update(each(lambda p: sum(_dot_nt(lower, jnp.broadcast_to(part, (CHUNK, CHUNK)))
                                        for part in lf_parts[p]), ps))
        b_row = each(lambda p: sum(_dot(jnp.broadcast_to(part, (16, CHUNK)), upper)
                                   for part in lf_parts[p])[0:1, :], ps)
        a_row = each(lambda p: ig_all[p[1]:p[1] + 1, rows(p)] - b_row[p], ps)
        d_log = each(lambda p: jnp.where(causal, b_col[p] + a_row[p], -jnp.inf), ps)
        r_col.update(each(lambda p: jnp.max(d_log[p], axis=1, keepdims=True), ps))
        qk = each(lambda p: _dot_nt(q_ref[rows(p), cols(p)], k_ref[rows(p), cols(p)]), ps)
        s = each(lambda p: (qk[p] * jnp.exp(d_log[p] - r_col[p])).astype(BF16), ps)
        num_intra.update(each(lambda p: _dot(s[p], v_ref[rows(p), cols(p)]), ps))
        den_intra.update(each(lambda p: _dot(s[p], ones_sq), ps))
        b_end.update(each(lambda p: b_row[p][:, CHUNK - 1:CHUNK], ps))
        log_end = each(lambda p: b_end[p] + a_row[p], ps)
        r_end.update(each(lambda p: jnp.max(log_end[p], axis=1, keepdims=True), ps))
        w_row = each(lambda p: jnp.broadcast_to(jnp.exp(log_end[p] - r_end[p]).astype(BF16), (16, CHUNK)), ps)
        kw_t = each(lambda p: k_ref[rows(p), cols(p)].T * w_row[p][0:1, :], ps)
        c_upd.update(each(lambda p: _dot(kw_t[p], v_ref[rows(p), cols(p)]), ps))
        n_upd.update(each(lambda p: _dot(w_row[p], k_ref[rows(p), cols(p)])[0:1, :], ps))

    independent(probs)

    for ci in range(n_chunks):
        ps = [(ci, h) for h in heads]
        m_prev = each(lambda p: m_scr[p[1]:p[1] + 1, 0:1], ps)
        inter = each(lambda p: b_col[p] + m_prev[p], ps)
        m_t = each(lambda p: jnp.maximum(inter[p], r_col[p]), ps)
        w_intra = each(lambda p: jnp.exp(r_col[p] - m_t[p]), ps)
        w_inter = each(lambda p: jnp.exp(inter[p] - m_t[p]), ps)
        qc = each(lambda p: _dot(q_ref[rows(p), cols(p)], c_scr[p[1]].astype(BF16)), ps)
        qn = each(lambda p: _dot_nt(q_ref[rows(p), cols(p)],
                                    jnp.broadcast_to(n_scr[p[1]:p[1] + 1, :].astype(BF16), (CHUNK, HEAD_W))), ps)
        den = each(lambda p: w_inter[p] * qn[p] + w_intra[p] * den_intra[p], ps)
        inv = each(lambda p: 1.0 / jnp.maximum(jnp.abs(den[p]), jnp.exp(-m_t[p])), ps)
        hh = each(lambda p: wide(w_inter[p] * inv[p]) * qc[p] + wide(w_intra[p] * inv[p]) * num_intra[p], ps)
        hb = each(lambda p: o_ref[rows(p), cols(p)].astype(F32) * hh[p], ps)
        ctr = each(lambda p: hb[p] - wide(_dot(hb[p].astype(BF16), mean_w)), ps)
        var = each(lambda p: _dot((ctr[p] * ctr[p]).astype(BF16), mean_w), ps)
        for p in ps:
            y = ctr[p] * wide(lax.rsqrt(var[p] + EPS)) * mng_ref[:, cols(p)]
            yb_ref[rows(p), cols(p)] = (y * z_ref[rows(p), cols(p)].astype(F32)).astype(BF16)
        m_new = each(lambda p: jnp.maximum(b_end[p] + m_prev[p], r_end[p]), ps)
        decay = each(lambda p: jnp.exp(b_end[p] + m_prev[p] - m_new[p]), ps)
        w_end = each(lambda p: jnp.exp(r_end[p] - m_new[p]), ps)
        for p in ps:
            h = p[1]
            c_scr[h] = decay[p] * c_scr[h] + w_end[p] * c_upd[p]
            n_scr[h:h + 1, :] = decay[p] * n_scr[h:h + 1, :] + w_end[p] * n_upd[p]
            m_scr[h:h + 1, :] = jnp.broadcast_to(m_new[p], (1, CHUNK))

    @pl.when(step == pl.num_programs(1) - 1)
    def _():
        c_out[0] = c_scr[...]
        n_out[0] = n_scr[0:N_HEADS, :]
        m_out[0] = m_scr[...]


def _mlstm_prompt_call(p_act, gates, mng, *, batch, seq, n_chunks):
    n = batch * seq
    rows = n_chunks * CHUNK
    steps = seq // rows

    def seg(s):
        return pl.BlockSpec((rows, D_MODEL), lambda b, c: (b * steps + c, s))

    return pl.pallas_call(
        functools.partial(_mlstm_prompt_kernel, n_chunks=n_chunks),
        grid=(batch, steps),
        in_specs=[seg(SEG_Q), seg(SEG_K), seg(SEG_VB), seg(SEG_OB), seg(SEG_ZB),
                  pl.BlockSpec((8, rows), lambda b, c: (0, b * steps + c)),
                  pl.BlockSpec((1, D_MODEL), lambda b, c: (0, 0))],
        out_specs=[pl.BlockSpec((rows, D_MODEL), lambda b, c: (b * steps + c, 0)),
                   pl.BlockSpec((1, N_HEADS, HEAD_W, HEAD_W), lambda b, c: (b, 0, 0, 0)),
                   pl.BlockSpec((1, N_HEADS, HEAD_W), lambda b, c: (b, 0, 0)),
                   pl.BlockSpec((1, 8, CHUNK), lambda b, c: (b, 0, 0))],
        out_shape=[jax.ShapeDtypeStruct((n, D_MODEL), BF16),
                   jax.ShapeDtypeStruct((batch, N_HEADS, HEAD_W, HEAD_W), F32),
                   jax.ShapeDtypeStruct((batch, N_HEADS, HEAD_W), F32),
                   jax.ShapeDtypeStruct((batch, 8, CHUNK), F32)],
        scratch_shapes=[pltpu.VMEM((N_HEADS, HEAD_W, HEAD_W), F32), pltpu.VMEM((8, HEAD_W), F32),
                        pltpu.VMEM((8, CHUNK), F32)],
        compiler_params=_params("arbitrary", "arbitrary"),
        name="mlstm_prompt",
    )(p_act, p_act, p_act, p_act, p_act, gates, mng)


def _mlstm_sample_kernel(q_ref, k_ref, v_ref, o_ref, z_ref, g_ref, c0_ref, n0_ref, m0_ref, mng_ref, *rest, ts):
    yb_ref, c_out, n_out, m_out = rest[-4:]
    h = pl.program_id(1)
    n_seq = CHUNK // ts
    same, causal, eye, si = _block_causal_masks(ts)
    ig_row = g_ref[pl.ds(h, 1), :]
    lf_row = _log_sigmoid(g_ref[pl.ds(N_HEADS + h, 1), :])
    b_col = jnp.sum(jnp.where(causal, lf_row, 0.0), axis=1, keepdims=True)
    b_row = _col_to_row(b_col, eye)
    a_row = ig_row - b_row
    m_col = m0_ref[0]
    d_log = jnp.where(causal, b_col + a_row, -jnp.inf)
    inter = b_col + m_col
    m_t = jnp.maximum(inter, jnp.max(d_log, axis=1, keepdims=True))
    w_intra = jnp.exp(d_log - m_t)
    w_inter = jnp.exp(inter - m_t)
    q = q_ref[...]
    k = k_ref[...]
    v = v_ref[...]
    s = _dot_nt(q, k) * w_intra
    num_inter = jnp.concatenate(
        [_dot(q[b * ts:(b + 1) * ts, :], c0_ref[0, b, 0].astype(BF16)) for b in range(n_seq)], axis=0)
    n_tok = jnp.concatenate([jnp.broadcast_to(n0_ref[b, 0], (ts, HEAD_W)) for b in range(n_seq)], axis=0)
    num = w_inter * num_inter + _dot(s.astype(BF16), v)
    qn = jnp.sum(q.astype(F32) * n_tok, axis=1, keepdims=True)
    den = w_inter * qn + jnp.sum(s, axis=1, keepdims=True)
    hh = num / jnp.maximum(jnp.abs(den), jnp.exp(-m_t))
    yb_ref[...] = _head_out(hh, o_ref[...], z_ref[...], mng_ref[...])

    last_in_seq = same & ((si & (ts - 1)) == ts - 1)
    b_end_col = jnp.sum(jnp.where(last_in_seq, b_row, 0.0), axis=1, keepdims=True)
    b_end_row = _col_to_row(b_end_col, eye)
    log_end_row = b_end_row + a_row
    seq_max_col = jnp.max(jnp.where(same, log_end_row, -jnp.inf), axis=1, keepdims=True)
    m_new_col = jnp.maximum(b_end_col + m_col, seq_max_col)
    decay_col = jnp.exp(b_end_col + m_col - m_new_col)
    m_new_row = _col_to_row(m_new_col, eye)
    w_end_col = _row_to_col(jnp.exp(log_end_row - m_new_row), eye)
    kw = k.astype(F32) * w_end_col
    kw_t = kw.T
    lane_seq = lax.broadcasted_iota(jnp.int32, (1, CHUNK), 1) >> (ts.bit_length() - 1)
    for b in range(n_seq):
        decay = decay_col[b * ts:b * ts + 1, :]
        kw_b = jnp.where(lane_seq == b, kw_t, 0.0).astype(BF16)
        c_out[0, b, 0] = decay * c0_ref[0, b, 0] + _dot(kw_b, v)
        n_out[b, 0] = decay * n0_ref[b, 0] + jnp.sum(kw[b * ts:(b + 1) * ts, :], axis=0, keepdims=True)
        m_out[b, 0] = jnp.broadcast_to(m_new_col[b * ts:b * ts + 1, :], (1, CHUNK))


def _mlstm_sample_call(p_act, gates, c0_all, n0, m0_tok, mng, c_new_all, *, layer, ts):
    n = p_act.shape[0]
    db = n // ts
    n_seq = CHUNK // ts

    def seg(s):
        return pl.BlockSpec((CHUNK, HEAD_W), lambda i, h: (i, s * N_HEADS + h))

    state_c = pl.BlockSpec((1, n_seq, 1, HEAD_W, HEAD_W), lambda i, h: (layer, i, h, 0, 0))
    state_n = pl.BlockSpec((n_seq, 1, 1, HEAD_W), lambda i, h: (i, h, 0, 0))
    in_specs = [seg(SEG_Q), seg(SEG_K), seg(SEG_VB), seg(SEG_OB), seg(SEG_ZB),
                pl.BlockSpec((8, CHUNK), lambda i, h: (0, i)),
                state_c, state_n,
                pl.BlockSpec((1, CHUNK, 1), lambda i, h: (h, i, 0)),
                pl.BlockSpec((1, HEAD_W), lambda i, h: (0, h))]
    args = [p_act, p_act, p_act, p_act, p_act, gates, c0_all, n0, m0_tok, mng]
    aliases = {}
    if c_new_all is not None:
        aliases = {len(args): 1}
        in_specs.append(pl.BlockSpec(memory_space=pl.ANY))
        args.append(c_new_all)
    return pl.pallas_call(
        functools.partial(_mlstm_sample_kernel, ts=ts),
        grid=(n // CHUNK, N_HEADS),
        in_specs=in_specs,
        out_specs=[pl.BlockSpec((CHUNK, HEAD_W), lambda i, h: (i, h)),
                   state_c, state_n,
                   pl.BlockSpec((n_seq, 1, 1, CHUNK), lambda i, h: (i, h, 0, 0))],
        out_shape=[jax.ShapeDtypeStruct((n, D_MODEL), BF16),
                   jax.ShapeDtypeStruct(c0_all.shape, F32),
                   jax.ShapeDtypeStruct((db, N_HEADS, 1, HEAD_W), F32),
                   jax.ShapeDtypeStruct((db, N_HEADS, 1, CHUNK), F32)],
        input_output_aliases=aliases,
        compiler_params=_params("arbitrary", "arbitrary"),
        name="mlstm_sample",
    )(*args)


def _merge_kernel(ya_ref, yb_ref, yc_ref, ga_ref, gb_ref, gc_ref, x_ref, gate_ref, wa_ref, wb_ref, wc_ref,
                  wo_ref, fng_ref, o_ref, *, mod_rows, final):
    merged = (ga_ref[...].astype(F32) * _dot(ya_ref[...], wa_ref[...])
              + gb_ref[...].astype(F32) * _dot(yb_ref[...], wb_ref[...])
              + gc_ref[...].astype(F32) * _dot(yc_ref[...], wc_ref[...]))
    n_rows = x_ref.shape[0]
    gate = jnp.broadcast_to(gate_ref[...], (n_rows // mod_rows, mod_rows, D_MODEL)).reshape(n_rows, D_MODEL)
    xn = x_ref[...] + gate * _dot(merged.astype(BF16), wo_ref[...])
    if final:
        ms = jnp.mean(xn * xn, axis=-1, keepdims=True)
        xn = (xn * lax.rsqrt(ms + EPS)) * fng_ref[...]
    o_ref[...] = xn


def _merge_call(ya, yb, yc, p_act, x, gate, wa, wb, wc, wo, fng, *, tm, rows_per_mod, final):
    n = x.shape[0]
    tok_spec = pl.BlockSpec((tm, D_MODEL), lambda i: (i, 0))
    gate_spec = pl.BlockSpec((tm // rows_per_mod, 1, D_MODEL), lambda i: (i, 0, 0))

    def seg(s):
        return pl.BlockSpec((tm, D_MODEL), lambda i: (i, s))

    w_spec = pl.BlockSpec((D_MODEL, D_MODEL), lambda i: (0, 0))
    return pl.pallas_call(
        functools.partial(_merge_kernel, mod_rows=rows_per_mod, final=final),
        grid=(n // tm,),
        in_specs=[tok_spec, tok_spec, tok_spec, seg(SEG_GA), seg(SEG_GB), seg(SEG_GC), tok_spec, gate_spec,
                  w_spec, w_spec, w_spec, w_spec, pl.BlockSpec((1, D_MODEL), lambda i: (0, 0))],
        out_specs=tok_spec,
        out_shape=jax.ShapeDtypeStruct((n, D_MODEL), F32),
        compiler_params=_params("arbitrary"),
        name="merge_out",
    )(ya, yb, yc, p_act, p_act, p_act, x, gate, wa, wb, wc, wo, fng)


def _row_tile(n, target):
    tile = min(n, target)
    assert n % tile == 0 and tile % CHUNK == 0, (n, tile)
    return tile


def kernel(x_prompt, x_sample, c_prompt, c_sample, state_mlstm_C, state_mlstm_n, state_mlstm_m, state_pool, w_mod, b_mod, norm_g, w_in, b_if, sgu_ln_g, sgu_ln_b, w_sgu, b_sgu, mlstm_norm_g, w_pool, pool_scale, w_br_a, w_br_b, w_br_c, w_out, final_norm_g):
    batch, seq, _ = x_prompt.shape
    db, ts, _ = x_sample.shape
    depth = w_mod.shape[0]
    assert seq % CHUNK == 0 and CHUNK % ts == 0 and ts & (ts - 1) == 0 and (db * ts) % CHUNK == 0
    n_p, n_s = batch * seq, db * ts

    n_mod = batch + db
    c_all = jnp.concatenate([c_prompt, c_sample, jnp.zeros((-n_mod % 8, D_MODEL), F32)], axis=0)
    mod = _mod_call(c_all, w_mod, b_mod)

    w_in_t = jnp.swapaxes(w_in, 1, 2)
    w_bf, w_gt = _wcast_call(w_in_t)
    square_w = _square_cast_call(w_br_a, w_br_b, w_br_c, w_out)

    xp = x_prompt.reshape(n_p, D_MODEL)
    xs = x_sample.reshape(n_s, D_MODEL)
    tm_p = _row_tile(seq, 2048)
    tm_s = _row_tile(n_s, 1024)
    tc_p = _row_tile(seq, 512)
    mlstm_chunks = min(seq // CHUNK, 8)
    assert (seq // CHUNK) % mlstm_chunks == 0
    tb_s = min(db, 32)
    fng = final_norm_g.reshape(1, D_MODEL)
    n_rep = CHUNK // ts

    outs = {k: [] for k in ("cp", "np", "mp", "bp", "ns", "ms", "bs", "vs")}
    c_s_all = None
    for l in range(depth):
        bif = b_if[l].reshape(8, 1)
        ng = norm_g[l].reshape(1, D_MODEL)
        lng = sgu_ln_g[l].reshape(1, D_MODEL)
        lnb = sgu_ln_b[l].reshape(1, D_MODEL)
        mng = mlstm_norm_g[l].reshape(1, D_MODEL)
        pscale = pool_scale[l].reshape(1, D_MODEL)
        wpool = w_pool[l].astype(BF16)
        wa, wb, wc, wo = (w[l] for w in square_w)
        final = l == depth - 1

        mod_p = mod[l, :batch].reshape(batch, 1, 3 * D_MODEL)
        shift_p, scale_p, gate_p = (mod_p[..., i * D_MODEL:(i + 1) * D_MODEL] for i in range(3))
        p_act, gates, pool_p = _inproj_call(xp, shift_p, scale_p, ng, w_bf, w_gt[l], bif, lng, lnb,
                                            layer=l, tm=tm_p, rows_per_mod=seq, emit_f32=False)
        yb, c_p, n_p_state, m_p = _mlstm_prompt_call(p_act, gates, mng, batch=batch, seq=seq,
                                                     n_chunks=mlstm_chunks)
        xp = _mix_merge_prompt_call(p_act, yb, xp, gate_p, w_sgu[l], b_sgu[l][:, :, None], wpool, pscale,
                                    wa, wb, wc, wo, fng, seq=seq, tc=tc_p, final=final)
        outs["cp"].append(c_p)
        outs["np"].append(n_p_state)
        outs["mp"].append(m_p[:, :N_HEADS, 0])
        outs["bp"].append(pool_p[:, 1:, :])

        mod_s = mod[l, batch:batch + db].reshape(db, 1, 3 * D_MODEL)
        shift_s, scale_s, gate_s = (mod_s[..., i * D_MODEL:(i + 1) * D_MODEL] for i in range(3))
        p_act, gates, va_f32, p_f32 = _inproj_call(xs, shift_s, scale_s, ng, w_bf, w_gt[l], bif, lng, lnb,
                                                   layer=l, tm=tm_s, rows_per_mod=ts, emit_f32=True)
        w_sgu_tiled = jnp.tile(w_sgu[l][:, :ts, :ts], (1, n_rep, n_rep))
        bs_tiled = jnp.tile(b_sgu[l][:, :ts], (1, n_rep))[:, :, None]
        ya, yc, pool_s = _sgu_pool_sample_call(p_act, p_f32, state_pool, w_sgu_tiled, bs_tiled, wpool, pscale,
                                               layer=l, tb=tb_s, ts=ts)
        m0_tok = jnp.repeat(state_mlstm_m[l], ts, axis=0).T[:, :, None]
        yb, c_s_all, n_s_state, m_s = _mlstm_sample_call(p_act, gates, state_mlstm_C,
                                                         state_mlstm_n[l][:, :, None, :], m0_tok, mng, c_s_all,
                                                         layer=l, ts=ts)
        xs = _merge_call(ya, yb, yc, p_act, xs, gate_s, wa, wb, wc, wo, fng,
                         tm=_row_tile(n_s, 512), rows_per_mod=ts, final=final)
        outs["ns"].append(n_s_state[:, :, 0, :])
        outs["ms"].append(m_s[:, :, 0, 0])
        outs["bs"].append(pool_s)
        outs["vs"].append(va_f32.reshape(db, ts, D_MODEL))

    y_prompt = xp.reshape(batch, seq, D_MODEL)
    y_sample = xs.reshape(db, ts, D_MODEL)
    return (y_prompt, y_sample,
            jnp.stack(outs["cp"]), jnp.stack(outs["np"]), jnp.stack(outs["mp"]), jnp.stack(outs["bp"]),
            c_s_all, jnp.stack(outs["ns"]), jnp.stack(outs["ms"]), jnp.stack(outs["bs"]),
            jnp.stack(outs["vs"]))
```

```python
import functools

import jax
import jax.numpy as jnp
from jax import lax
from jax.experimental import pallas as pl
from jax.experimental.pallas import tpu as pltpu

F32 = jnp.float32
BF16 = jnp.bfloat16

D_MODEL = 1024
N_HEADS = 4
HEAD_W = D_MODEL // N_HEADS
CHUNK = 128
POOL_WINDOWS = (2, 4, 8, 16)
POOL_BUF = 15
POOL_PAD = 16
POOL_HALO = 24
PAST_LEN = 16384
EPS = 1e-6
N_SEG = 13
SEG_GA, SEG_GB, SEG_GC, SEG_U, SEG_VA, SEG_ZA, SEG_Q, SEG_K, SEG_VB, SEG_OB, SEG_ZB, SEG_P, SEG_ZC = range(N_SEG)
N_ALIGNED_SEG = 11
N_GATE_COLS = 2 * N_HEADS
VMEM_LIMIT_BYTES = 56 * 1024 * 1024


def _params(*semantics):
    return pltpu.CompilerParams(dimension_semantics=semantics, vmem_limit_bytes=VMEM_LIMIT_BYTES)


def _dot(a, b):
    return jnp.dot(a, b, preferred_element_type=F32)


def _dot_nt(a, b):
    return lax.dot_general(a, b, (((1,), (1,)), ((), ())), preferred_element_type=F32)


def _sigmoid(x):
    return 0.5 * jnp.tanh(0.5 * x) + 0.5


def _silu(x):
    return x * _sigmoid(x)


def _log_sigmoid(x):
    return jnp.minimum(x, 0.0) - jnp.log(1.0 + jnp.exp(-jnp.abs(x)))


def _gelu(x):
    return jax.nn.gelu(x, approximate=True)


def _mod_kernel(c_ref, w_ref, b_ref, o_ref):
    c = c_ref[...]
    o_ref[0] = _dot(_silu(c).astype(BF16), w_ref[0].astype(BF16)) + b_ref[0]


def _mod_call(c_all, w_mod, b_mod):
    depth = w_mod.shape[0]
    rows = c_all.shape[0]
    return pl.pallas_call(
        _mod_kernel,
        grid=(depth, 3),
        in_specs=[pl.BlockSpec((rows, D_MODEL), lambda l, j: (0, 0)),
                  pl.BlockSpec((1, D_MODEL, D_MODEL), lambda l, j: (l, 0, j)),
                  pl.BlockSpec((1, 1, D_MODEL), lambda l, j: (l, 0, j))],
        out_specs=pl.BlockSpec((1, rows, D_MODEL), lambda l, j: (l, 0, j)),
        out_shape=jax.ShapeDtypeStruct((depth, rows, 3 * D_MODEL), F32),
        compiler_params=_params("arbitrary", "arbitrary"),
        name="adaln_mod",
    )(c_all, w_mod, b_mod.reshape(depth, 1, 3 * D_MODEL))


def _wcast_kernel(w_ref, wnext_ref, o_ref, gate_ref):
    j = pl.program_id(1)

    @pl.when(j < N_ALIGNED_SEG)
    def _():
        o_ref[0] = w_ref[0].T.astype(BF16)

    @pl.when(j == N_ALIGNED_SEG - 1)
    def _():
        gate_ref[...] = wnext_ref[...].astype(BF16)

    @pl.when(j >= N_ALIGNED_SEG)
    def _():
        rows = jnp.concatenate([w_ref[0][N_GATE_COLS:, :], wnext_ref[0]], axis=0)
        o_ref[0] = rows.T.astype(BF16)


def _wcast_call(w_in_t):
    depth = w_in_t.shape[0]
    next_blocks = D_MODEL // N_GATE_COLS
    return pl.pallas_call(
        _wcast_kernel,
        grid=(depth, N_SEG),
        in_specs=[pl.BlockSpec((1, D_MODEL, D_MODEL), lambda l, j: (l, j, 0)),
                  pl.BlockSpec((1, N_GATE_COLS, D_MODEL), lambda l, j: (l, (j + 1) * next_blocks, 0))],
        out_specs=[pl.BlockSpec((1, D_MODEL, D_MODEL), lambda l, j: (l, 0, j)),
                   pl.BlockSpec((1, N_GATE_COLS, D_MODEL), lambda l, j: (l, 0, 0))],
        out_shape=[jax.ShapeDtypeStruct((depth, D_MODEL, N_SEG * D_MODEL), BF16),
                   jax.ShapeDtypeStruct((depth, N_GATE_COLS, D_MODEL), BF16)],
        compiler_params=_params("arbitrary", "arbitrary"),
        name="w_in_bf16",
    )(w_in_t, w_in_t)


def _square_cast_kernel(a_ref, b_ref, c_ref, d_ref, ao_ref, bo_ref, co_ref, do_ref):
    for src, dst in ((a_ref, ao_ref), (b_ref, bo_ref), (c_ref, co_ref), (d_ref, do_ref)):
        dst[...] = src[...].astype(BF16)


def _square_cast_call(*weights):
    depth = weights[0].shape[0]
    rows = D_MODEL // 4
    spec = pl.BlockSpec((1, rows, D_MODEL), lambda l, r: (l, r, 0))
    return pl.pallas_call(
        _square_cast_kernel,
        grid=(depth, D_MODEL // rows),
        in_specs=[spec] * 4,
        out_specs=[spec] * 4,
        out_shape=[jax.ShapeDtypeStruct(w.shape, BF16) for w in weights],
        compiler_params=_params("arbitrary", "arbitrary"),
        name="branch_w_bf16",
    )(*weights)


def _inproj_kernel(x_ref, shift_ref, scale_ref, ng_ref, w_ref, wgt_ref, bif_ref, lng_ref, lnb_ref,
                   p_ref, g_ref, *rest, mod_rows, emit_f32, tm, sub, tiles_per_seq):
    if emit_f32:
        va_ref, pf_ref, h_scr = rest
    else:
        pool_ref, h_scr, pbuf, *level_bufs = rest
    j = pl.program_id(1)

    def normed_rows(rows):
        x = x_ref[rows, :]
        ms = jnp.mean(x * x, axis=-1, keepdims=True)
        y = (x * lax.rsqrt(ms + EPS)) * ng_ref[...]
        if mod_rows >= tm:
            sc, sh = scale_ref[0], shift_ref[0]
        else:
            n_rows = rows.stop - rows.start
            groups = slice(rows.start // mod_rows, rows.stop // mod_rows)
            sc, sh = (jnp.broadcast_to(ref[groups], (n_rows // mod_rows, mod_rows, D_MODEL)).reshape(n_rows, D_MODEL)
                      for ref in (scale_ref, shift_ref))
        h = (y * (1.0 + sc) + sh).astype(BF16)
        h_scr[rows, :] = h
        g_ref[:, rows] = _dot_nt(wgt_ref[...], h) + bif_ref[...]
        return h

    def project(epilogue, first=False):
        w = w_ref[0]
        step = max(sub // 2, 2 * CHUNK) if first else sub
        for r0 in range(0, tm, step):
            rows = slice(r0, r0 + step)
            h = normed_rows(rows) if first else h_scr[rows, :]
            epilogue(rows, _dot(h, w))

    def store(fn):
        def epilogue(rows, acc):
            p_ref[rows, :] = fn(acc).astype(BF16)
        return epilogue

    def va_epilogue(rows, acc):
        gl = _gelu(acc)
        mu = jnp.mean(gl, axis=-1, keepdims=True)
        ctr = gl - mu
        var = jnp.mean(ctr * ctr, axis=-1, keepdims=True)
        y = ctr * lax.rsqrt(var + EPS) * lng_ref[...] + lnb_ref[...]
        p_ref[rows, :] = y.astype(BF16)
        if emit_f32:
            va_ref[rows, :] = y

    def p_epilogue(rows, acc):
        p_ref[rows, :] = acc.astype(BF16)
        pf_ref[rows, :] = acc

    def pooled_epilogue(rows, acc):
        n_rows = rows.stop - rows.start
        tile_in_seq = pl.program_id(0) % tiles_per_seq
        sums = (pbuf,) + tuple(level_bufs)
        if rows.start == 0:
            @pl.when(tile_in_seq == 0)
            def _():
                for buf in sums:
                    buf[0:POOL_HALO, :] = jnp.zeros((POOL_HALO, buf.shape[1]), F32)
        pbuf[POOL_HALO:POOL_HALO + n_rows, :] = acc
        pos = tile_in_seq * tm + rows.start + lax.broadcasted_iota(jnp.int32, (n_rows, 1), 0)
        live = slice(POOL_HALO, POOL_HALO + n_rows)
        for g, w in enumerate(POOL_WINDOWS):
            prev = sums[g]
            shift = w // 2
            span = slice(8, POOL_HALO + n_rows)
            shifted = slice(8 - shift, POOL_HALO + n_rows - shift)
            if g + 1 < len(POOL_WINDOWS):
                sums[g + 1][span, :] = prev[span, HEAD_W:] + prev[shifted, HEAD_W:]
            total = prev[live, 0:HEAD_W] + prev[slice(live.start - shift, live.stop - shift), 0:HEAD_W]
            cnt = jnp.minimum(pos + 1, w).astype(F32)
            cols = slice(g * HEAD_W, (g + 1) * HEAD_W)
            p_ref[rows, cols] = (total / cnt - pbuf[live, cols]).astype(BF16)
        tail = pbuf[n_rows + POOL_HALO - POOL_PAD:n_rows + POOL_HALO, :]
        pool_ref[0] = tail
        pbuf[POOL_HALO - POOL_PAD:POOL_HALO, :] = tail

    pl.when(j == SEG_GA)(lambda: project(store(_sigmoid), first=True))
    pl.when((j == SEG_GB) | (j == SEG_GC) | (j == SEG_OB))(lambda: project(store(_sigmoid)))
    pl.when(j == SEG_U)(lambda: project(store(_gelu)))
    pl.when(j == SEG_VA)(lambda: project(va_epilogue))
    pl.when((j == SEG_ZA) | (j == SEG_ZB) | (j == SEG_ZC))(lambda: project(store(_silu)))
    pl.when((j == SEG_Q) | (j == SEG_VB))(lambda: project(store(lambda a: a)))
    pl.when(j == SEG_K)(lambda: project(store(lambda a: a * HEAD_W ** -0.5)))
    pl.when(j == SEG_P)(lambda: project(p_epilogue if emit_f32 else pooled_epilogue))


def _inproj_call(x, shift, scale, norm_g, w_bf, w_gt, b_if, ln_g, ln_b, *, layer, tm, rows_per_mod, emit_f32):
    n = x.shape[0]
    sub = min(tm, max(tm // 8, 2 * CHUNK))
    tiles_per_seq = None if emit_f32 else rows_per_mod // tm
    vec_spec = pl.BlockSpec((1, D_MODEL), lambda i, j: (0, 0))
    row_spec = pl.BlockSpec((tm, D_MODEL), lambda i, j: (i, 0))
    out_shape = [jax.ShapeDtypeStruct((n, N_SEG * D_MODEL), BF16), jax.ShapeDtypeStruct((8, n), F32)]
    out_specs = [pl.BlockSpec((tm, D_MODEL), lambda i, j: (i, j)), pl.BlockSpec((8, tm), lambda i, j: (0, i))]
    scratch = [pltpu.VMEM((tm, D_MODEL), BF16)]
    if emit_f32:
        mod_spec = pl.BlockSpec((tm // rows_per_mod, 1, D_MODEL), lambda i, j: (i, 0, 0))
        out_shape += [jax.ShapeDtypeStruct((n, D_MODEL), F32)] * 2
        out_specs += [row_spec, row_spec]
    else:
        mod_spec = pl.BlockSpec((1, 1, D_MODEL), lambda i, j: (i // tiles_per_seq, 0, 0))
        out_shape.append(jax.ShapeDtypeStruct((n // rows_per_mod, POOL_PAD, D_MODEL), F32))
        out_specs.append(pl.BlockSpec((1, POOL_PAD, D_MODEL), lambda i, j: (i // tiles_per_seq, 0, 0)))
        scratch += [pltpu.VMEM((POOL_HALO + sub, D_MODEL - g * HEAD_W), F32) for g in range(len(POOL_WINDOWS))]
    return pl.pallas_call(
        functools.partial(_inproj_kernel, mod_rows=rows_per_mod, emit_f32=emit_f32, tm=tm, sub=sub,
                          tiles_per_seq=tiles_per_seq),
        grid=(n // tm, N_SEG),
        in_specs=[row_spec, mod_spec, mod_spec, vec_spec,
                  pl.BlockSpec((1, D_MODEL, D_MODEL), lambda i, j: (layer, 0, j)),
                  pl.BlockSpec((8, D_MODEL), lambda i, j: (0, 0)),
                  pl.BlockSpec((8, 1), lambda i, j: (0, 0)),
                  vec_spec, vec_spec],
        out_specs=out_specs,
        out_shape=out_shape,
        scratch_shapes=scratch,
        compiler_params=_params("arbitrary", "arbitrary"),
        name="inproj",
    )(x, shift, scale, norm_g, w_bf, w_gt, b_if, ln_g, ln_b)


def _block_causal_masks(block_len):
    ti = lax.broadcasted_iota(jnp.int32, (CHUNK, CHUNK), 0)
    si = lax.broadcasted_iota(jnp.int32, (CHUNK, CHUNK), 1)
    shift = block_len.bit_length() - 1
    same = (ti >> shift) == (si >> shift)
    return same, same & (si <= ti), si == ti, si


def _sgu_rows(gu_ref, va_ref, sza_ref, wsgu_ref, bs_ref, ya_ref, causal, row_lo, row_hi):
    for g in range(N_HEADS):
        cols = slice(g * HEAD_W, (g + 1) * HEAD_W)
        wg = jnp.where(causal, wsgu_ref[g], 0.0).astype(BF16)
        bias = bs_ref[g]
        for r0 in range(row_lo, row_hi, CHUNK):
            rows = slice(r0, r0 + CHUNK)
            s = _dot(wg, va_ref[rows, cols]) + bias
            ya = gu_ref[rows, cols].astype(F32) * s * sza_ref[rows, cols].astype(F32)
            ya_ref[rows, cols] = ya.astype(BF16)


def _head_out(hh, o_blk, z_blk, mng_blk):
    hb = o_blk.astype(F32) * hh
    mu = jnp.mean(hb, axis=-1, keepdims=True)
    ctr = hb - mu
    var = jnp.mean(ctr * ctr, axis=-1, keepdims=True)
    y = ctr * lax.rsqrt(var + EPS) * mng_blk
    return (y * z_blk.astype(F32)).astype(BF16)


def _row_to_col(row, eye):
    return jnp.sum(jnp.where(eye, row, 0.0), axis=1, keepdims=True)


def _col_to_row(col, eye):
    return jnp.sum(jnp.where(eye, col, 0.0), axis=0, keepdims=True)


def _mix_merge_prompt_kernel(gu_ref, va_ref, sza_ref, pooled_ref, szc_ref, ga_ref, gb_ref, gc_ref, yb_ref, x_ref,
                             gate_ref, wsgu_ref, bs_ref, wpool_ref, pscale_ref, wa_ref, wb_ref, wc_ref, wo_ref,
                             fng_ref, o_ref, ya_ref, yc_ref, *, tc, sub, final):
    _, causal, _, _ = _block_causal_masks(CHUNK)

    def pool_project(rows):
        for g in range(N_HEADS):
            cols = slice(g * HEAD_W, (g + 1) * HEAD_W)
            pm = _dot(pooled_ref[rows, cols], wpool_ref[g])
            yc_ref[rows, cols] = (pm * pscale_ref[:, cols] * szc_ref[rows, cols].astype(F32)).astype(BF16)

    def gated(g_ref, y_ref, w_ref, rows):
        return g_ref[rows, :].astype(F32) * _dot(y_ref[rows, :], w_ref[...])

    for r0 in range(0, tc, sub):
        rows = slice(r0, r0 + sub)
        part_b = gated(gb_ref, yb_ref, wb_ref, rows)
        _sgu_rows(gu_ref, va_ref, sza_ref, wsgu_ref, bs_ref, ya_ref, causal, r0, r0 + sub)
        part_ab = gated(ga_ref, ya_ref, wa_ref, rows) + part_b
        pool_project(rows)
        merged = part_ab + gated(gc_ref, yc_ref, wc_ref, rows)
        xn = x_ref[rows, :] + gate_ref[0] * _dot(merged.astype(BF16), wo_ref[...])
        if final:
            ms = jnp.mean(xn * xn, axis=-1, keepdims=True)
            xn = (xn * lax.rsqrt(ms + EPS)) * fng_ref[...]
        o_ref[rows, :] = xn


def _mix_merge_prompt_call(p_act, yb, x, gate, w_sgu, bs_col, w_pool, pool_scale, wa, wb, wc, wo, fng,
                           *, seq, tc, final):
    n = x.shape[0]
    tiles_per_seq = seq // tc

    def seg(s):
        return pl.BlockSpec((tc, D_MODEL), lambda i: (i, s))

    tok_spec = pl.BlockSpec((tc, D_MODEL), lambda i: (i, 0))
    vec_spec = pl.BlockSpec((1, D_MODEL), lambda i: (0, 0))
    w_spec = pl.BlockSpec((D_MODEL, D_MODEL), lambda i: (0, 0), pipeline_mode=pl.Buffered(1))
    return pl.pallas_call(
        functools.partial(_mix_merge_prompt_kernel, tc=tc, sub=min(tc, 512), final=final),
        grid=(n // tc,),
        in_specs=[seg(SEG_U), seg(SEG_VA), seg(SEG_ZA), seg(SEG_P), seg(SEG_ZC),
                  seg(SEG_GA), seg(SEG_GB), seg(SEG_GC), tok_spec, tok_spec,
                  pl.BlockSpec((1, 1, D_MODEL), lambda i: (i // tiles_per_seq, 0, 0)),
                  pl.BlockSpec((N_HEADS, CHUNK, CHUNK), lambda i: (0, 0, 0)),
                  pl.BlockSpec((N_HEADS, CHUNK, 1), lambda i: (0, 0, 0)),
                  pl.BlockSpec((N_HEADS, HEAD_W, HEAD_W), lambda i: (0, 0, 0)),
                  vec_spec, w_spec, w_spec, w_spec, w_spec, vec_spec],
        out_specs=tok_spec,
        out_shape=jax.ShapeDtypeStruct((n, D_MODEL), F32),
        scratch_shapes=[pltpu.VMEM((tc, D_MODEL), BF16), pltpu.VMEM((tc, D_MODEL), BF16)],
        compiler_params=_params("arbitrary"),
        name="mix_merge_prompt",
    )(p_act, p_act, p_act, p_act, p_act, p_act, p_act, p_act, yb, x, gate, w_sgu, bs_col, w_pool, pool_scale,
      wa, wb, wc, wo, fng)


def _sgu_pool_sample_kernel(gu_ref, va_ref, sza_ref, szc_ref, pf_ref, buf_ref, wsgu_ref, bs_ref, wpool_ref,
                            pscale_ref, ya_ref, yc_ref, nbuf_ref, fscr, *, tb, ts, n_past):
    rows = tb * ts
    _, causal, _, _ = _block_causal_masks(ts)
    _sgu_rows(gu_ref, va_ref, sza_ref, wsgu_ref, bs_ref, ya_ref, causal, 0, rows)

    fscr[:, POOL_PAD - POOL_BUF:POOL_PAD, :] = buf_ref[0]
    fscr[:, POOL_PAD:, :] = pf_ref[...]
    pos = n_past + lax.broadcasted_iota(jnp.int32, (1, ts, 1), 1)
    for g, w in enumerate(POOL_WINDOWS):
        cols = slice(g * HEAD_W, (g + 1) * HEAD_W)
        cur = fscr[:, POOL_PAD:POOL_PAD + ts, cols]
        acc = cur
        for k in range(1, w):
            acc = acc + fscr[:, POOL_PAD - k:POOL_PAD - k + ts, cols]
        cnt = jnp.minimum(pos + 1, w).astype(F32)
        pooled = (acc / cnt - cur).reshape(rows, HEAD_W)
        pm = _dot(pooled.astype(BF16), wpool_ref[g])
        yc_ref[:, cols] = (pm * pscale_ref[:, cols] * szc_ref[:, cols].astype(F32)).astype(BF16)
    nbuf_ref[...] = fscr[:, POOL_PAD + ts - POOL_BUF:POOL_PAD + ts, :]


def _sgu_pool_sample_call(p_act, p_f32, pool_buf_all, w_sgu_tiled, bs_col_tiled, w_pool, pool_scale,
                          *, layer, tb, ts):
    n = p_act.shape[0]
    db = n // ts
    rows = tb * ts

    def seg(s):
        return pl.BlockSpec((rows, D_MODEL), lambda i: (i, s))

    tok_spec = pl.BlockSpec((rows, D_MODEL), lambda i: (i, 0))
    return pl.pallas_call(
        functools.partial(_sgu_pool_sample_kernel, tb=tb, ts=ts, n_past=PAST_LEN),
        grid=(db // tb,),
        in_specs=[seg(SEG_U), seg(SEG_VA), seg(SEG_ZA), seg(SEG_ZC),
                  pl.BlockSpec((tb, ts, D_MODEL), lambda i: (i, 0, 0)),
                  pl.BlockSpec((1, tb, POOL_BUF, D_MODEL), lambda i: (layer, i, 0, 0)),
                  pl.BlockSpec((N_HEADS, CHUNK, CHUNK), lambda i: (0, 0, 0)),
                  pl.BlockSpec((N_HEADS, CHUNK, 1), lambda i: (0, 0, 0)),
                  pl.BlockSpec((N_HEADS, HEAD_W, HEAD_W), lambda i: (0, 0, 0)),
                  pl.BlockSpec((1, D_MODEL), lambda i: (0, 0))],
        out_specs=[tok_spec, tok_spec, pl.BlockSpec((tb, POOL_BUF, D_MODEL), lambda i: (i, 0, 0))],
        out_shape=[jax.ShapeDtypeStruct((n, D_MODEL), BF16), jax.ShapeDtypeStruct((n, D_MODEL), BF16),
                   jax.ShapeDtypeStruct((db, POOL_BUF, D_MODEL), F32)],
        scratch_shapes=[pltpu.VMEM((tb, POOL_PAD + ts, D_MODEL), F32)],
        compiler_params=_params("arbitrary"),
        name="sgu_pool_sample",
    )(p_act, p_act, p_act, p_act, p_f32.reshape(db, ts, D_MODEL), pool_buf_all, w_sgu_tiled, bs_col_tiled, w_pool,
      pool_scale)


def _mlstm_prompt_kernel(q_ref, k_ref, v_ref, o_ref, z_ref, g_ref, mng_ref,
                         yb_ref, c_out, n_out, m_out, c_scr, n_scr, m_scr, *, n_chunks):
    step = pl.program_id(1)

    @pl.when(step == 0)
    def _():
        c_scr[...] = jnp.zeros(c_scr.shape, F32)
        n_scr[...] = jnp.zeros(n_scr.shape, F32)
        m_scr[...] = jnp.zeros(m_scr.shape, F32)

    _, causal, eye, _ = _block_causal_masks(CHUNK)
    heads = range(N_HEADS)
    probs = [(ci, h) for ci in range(n_chunks) for h in heads]

    def rows(p):
        return slice(p[0] * CHUNK, (p[0] + 1) * CHUNK)

    def cols(p):
        return slice(p[1] * HEAD_W, (p[1] + 1) * HEAD_W)

    def each(fn, keys=probs):
        return {p: fn(p) for p in keys}

    lower = jnp.where(causal, 1.0, 0.0).astype(BF16)
    upper = lower.T
    ones_sq = jnp.ones((CHUNK, CHUNK), BF16)
    mean_w = jnp.full((HEAD_W, CHUNK), 1.0 / HEAD_W, BF16)

    def split(x):
        hi = x.astype(BF16)
        return hi, (x - hi.astype(F32)).astype(BF16)

    def wide(x):
        return jnp.concatenate([x, x], axis=1)

    ig_all = g_ref[0:N_HEADS, :]
    lf_all = _log_sigmoid(g_ref[N_HEADS:2 * N_HEADS, :])
    b_col, r_col, num_intra, den_intra, b_end, r_end, c_upd, n_upd = ({} for _ in range(8))

    def independent(ps):
        lf_parts = each(lambda p: split(lf_all[p[1]:p[1] + 1, rows(p)]), ps)
        b_col.update(each(lambda p: sum(_dot_nt(lower, jnp.broadcast_to(part, (CHUNK, CHUNK)))
                                        for part in lf_parts[p]), ps))
        b_row = each(lambda p: sum(_dot(jnp.broadcast_to(part, (16, CHUNK)), upper)
                                   for part in lf_parts[p])[0:1, :], ps)
        a_row = each(lambda p: ig_all[p[1]:p[1] + 1, rows(p)] - b_row[p], ps)
        d_log = each(lambda p: jnp.where(causal, b_col[p] + a_row[p], -jnp.inf), ps)
        r_col.update(each(lambda p: jnp.max(d_log[p], axis=1, keepdims=True), ps))
        qk = each(lambda p: _dot_nt(q_ref[rows(p), cols(p)], k_ref[rows(p), cols(p)]), ps)
        s = each(lambda p: (qk[p] * jnp.exp(d_log[p] - r_col[p])).astype(BF16), ps)
        num_intra.update(each(lambda p: _dot(s[p], v_ref[rows(p), cols(p)]), ps))
        den_intra.update(each(lambda p: _dot(s[p], ones_sq), ps))
        b_end.update(each(lambda p: b_row[p][:, CHUNK - 1:CHUNK], ps))
        log_end = each(lambda p: b_end[p] + a_row[p], ps)
        r_end.update(each(lambda p: jnp.max(log_end[p], axis=1, keepdims=True), ps))
        w_row = each(lambda p: jnp.broadcast_to(jnp.exp(log_end[p] - r_end[p]).astype(BF16), (16, CHUNK)), ps)
        kw_t = each(lambda p: k_ref[rows(p), cols(p)].T * w_row[p][0:1, :], ps)
        c_upd.update(each(lambda p: _dot(kw_t[p], v_ref[rows(p), cols(p)]), ps))
        n_upd.update(each(lambda p: _dot(w_row[p], k_ref[rows(p), cols(p)])[0:1, :], ps))

    independent(probs)

    for ci in range(n_chunks):
        ps = [(ci, h) for h in heads]
        m_prev = each(lambda p: m_scr[p[1]:p[1] + 1, 0:1], ps)
        inter = each(lambda p: b_col[p] + m_prev[p], ps)
        m_t = each(lambda p: jnp.maximum(inter[p], r_col[p]), ps)
        w_intra = each(lambda p: jnp.exp(r_col[p] - m_t[p]), ps)
        w_inter = each(lambda p: jnp.exp(inter[p] - m_t[p]), ps)
        qc = each(lambda p: _dot(q_ref[rows(p), cols(p)], c_scr[p[1]].astype(BF16)), ps)
        qn = each(lambda p: _dot_nt(q_ref[rows(p), cols(p)],
                                    jnp.broadcast_to(n_scr[p[1]:p[1] + 1, :].astype(BF16), (CHUNK, HEAD_W))), ps)
        den = each(lambda p: w_inter[p] * qn[p] + w_intra[p] * den_intra[p], ps)
        inv = each(lambda p: 1.0 / jnp.maximum(jnp.abs(den[p]), jnp.exp(-m_t[p])), ps)
        hh = each(lambda p: wide(w_inter[p] * inv[p]) * qc[p] + wide(w_intra[p] * inv[p]) * num_intra[p], ps)
        hb = each(lambda p: o_ref[rows(p), cols(p)].astype(F32) * hh[p], ps)
        ctr = each(lambda p: hb[p] - wide(_dot(hb[p].astype(BF16), mean_w)), ps)
        var = each(lambda p: _dot((ctr[p] * ctr[p]).astype(BF16), mean_w), ps)
        for p in ps:
            y = ctr[p] * wide(lax.rsqrt(var[p] + EPS)) * mng_ref[:, cols(p)]
            yb_ref[rows(p), cols(p)] = (y * z_ref[rows(p), cols(p)].astype(F32)).astype(BF16)
        m_new = each(lambda p: jnp.maximum(b_end[p] + m_prev[p], r_end[p]), ps)
        decay = each(lambda p: jnp.exp(b_end[p] + m_prev[p] - m_new[p]), ps)
        w_end = each(lambda p: jnp.exp(r_end[p] - m_new[p]), ps)
        for p in ps:
            h = p[1]
            c_scr[h] = decay[p] * c_scr[h] + w_end[p] * c_upd[p]
            n_scr[h:h + 1, :] = decay[p] * n_scr[h:h + 1, :] + w_end[p] * n_upd[p]
            m_scr[h:h + 1, :] = jnp.broadcast_to(m_new[p], (1, CHUNK))

    @pl.when(step == pl.num_programs(1) - 1)
    def _():
        c_out[0] = c_scr[...]
        n_out[0] = n_scr[0:N_HEADS, :]
        m_out[0] = m_scr[...]


def _mlstm_prompt_call(p_act, gates, mng, *, batch, seq, n_chunks):
    n = batch * seq
    rows = n_chunks * CHUNK
    steps = seq // rows

    def seg(s):
        return pl.BlockSpec((rows, D_MODEL), lambda b, c: (b * steps + c, s))

    return pl.pallas_call(
        functools.partial(_mlstm_prompt_kernel, n_chunks=n_chunks),
        grid=(batch, steps),
        in_specs=[seg(SEG_Q), seg(SEG_K), seg(SEG_VB), seg(SEG_OB), seg(SEG_ZB),
                  pl.BlockSpec((8, rows), lambda b, c: (0, b * steps + c)),
                  pl.BlockSpec((1, D_MODEL), lambda b, c: (0, 0))],
        out_specs=[pl.BlockSpec((rows, D_MODEL), lambda b, c: (b * steps + c, 0)),
                   pl.BlockSpec((1, N_HEADS, HEAD_W, HEAD_W), lambda b, c: (b, 0, 0, 0)),
                   pl.BlockSpec((1, N_HEADS, HEAD_W), lambda b, c: (b, 0, 0)),
                   pl.BlockSpec((1, 8, CHUNK), lambda b, c: (b, 0, 0))],
        out_shape=[jax.ShapeDtypeStruct((n, D_MODEL), BF16),
                   jax.ShapeDtypeStruct((batch, N_HEADS, HEAD_W, HEAD_W), F32),
                   jax.ShapeDtypeStruct((batch, N_HEADS, HEAD_W), F32),
                   jax.ShapeDtypeStruct((batch, 8, CHUNK), F32)],
        scratch_shapes=[pltpu.VMEM((N_HEADS, HEAD_W, HEAD_W), F32), pltpu.VMEM((8, HEAD_W), F32),
                        pltpu.VMEM((8, CHUNK), F32)],
        compiler_params=_params("arbitrary", "arbitrary"),
        name="mlstm_prompt",
    )(p_act, p_act, p_act, p_act, p_act, gates, mng)


def _mlstm_sample_kernel(q_ref, k_ref, v_ref, o_ref, z_ref, g_ref, c0_ref, n0_ref, m0_ref, mng_ref, *rest, ts):
    yb_ref, c_out, n_out, m_out = rest[-4:]
    h = pl.program_id(1)
    n_seq = CHUNK // ts
    same, causal, eye, si = _block_causal_masks(ts)
    ig_row = g_ref[pl.ds(h, 1), :]
    lf_row = _log_sigmoid(g_ref[pl.ds(N_HEADS + h, 1), :])
    b_col = jnp.sum(jnp.where(causal, lf_row, 0.0), axis=1, keepdims=True)
    b_row = _col_to_row(b_col, eye)
    a_row = ig_row - b_row
    m_col = m0_ref[0]
    d_log = jnp.where(causal, b_col + a_row, -jnp.inf)
    inter = b_col + m_col
    m_t = jnp.maximum(inter, jnp.max(d_log, axis=1, keepdims=True))
    w_intra = jnp.exp(d_log - m_t)
    w_inter = jnp.exp(inter - m_t)
    q = q_ref[...]
    k = k_ref[...]
    v = v_ref[...]
    s = _dot_nt(q, k) * w_intra
    num_inter = jnp.concatenate(
        [_dot(q[b * ts:(b + 1) * ts, :], c0_ref[0, b, 0].astype(BF16)) for b in range(n_seq)], axis=0)
    n_tok = jnp.concatenate([jnp.broadcast_to(n0_ref[b, 0], (ts, HEAD_W)) for b in range(n_seq)], axis=0)
    num = w_inter * num_inter + _dot(s.astype(BF16), v)
    qn = jnp.sum(q.astype(F32) * n_tok, axis=1, keepdims=True)
    den = w_inter * qn + jnp.sum(s, axis=1, keepdims=True)
    hh = num / jnp.maximum(jnp.abs(den), jnp.exp(-m_t))
    yb_ref[...] = _head_out(hh, o_ref[...], z_ref[...], mng_ref[...])

    last_in_seq = same & ((si & (ts - 1)) == ts - 1)
    b_end_col = jnp.sum(jnp.where(last_in_seq, b_row, 0.0), axis=1, keepdims=True)
    b_end_row = _col_to_row(b_end_col, eye)
    log_end_row = b_end_row + a_row
    seq_max_col = jnp.max(jnp.where(same, log_end_row, -jnp.inf), axis=1, keepdims=True)
    m_new_col = jnp.maximum(b_end_col + m_col, seq_max_col)
    decay_col = jnp.exp(b_end_col + m_col - m_new_col)
    m_new_row = _col_to_row(m_new_col, eye)
    w_end_col = _row_to_col(jnp.exp(log_end_row - m_new_row), eye)
    kw = k.astype(F32) * w_end_col
    kw_t = kw.T
    lane_seq = lax.broadcasted_iota(jnp.int32, (1, CHUNK), 1) >> (ts.bit_length() - 1)
    for b in range(n_seq):
        decay = decay_col[b * ts:b * ts + 1, :]
        kw_b = jnp.where(lane_seq == b, kw_t, 0.0).astype(BF16)
        c_out[0, b, 0] = decay * c0_ref[0, b, 0] + _dot(kw_b, v)
        n_out[b, 0] = decay * n0_ref[b, 0] + jnp.sum(kw[b * ts:(b + 1) * ts, :], axis=0, keepdims=True)
        m_out[b, 0] = jnp.broadcast_to(m_new_col[b * ts:b * ts + 1, :], (1, CHUNK))


def _mlstm_sample_call(p_act, gates, c0_all, n0, m0_tok, mng, c_new_all, *, layer, ts):
    n = p_act.shape[0]
    db = n // ts
    n_seq = CHUNK // ts

    def seg(s):
        return pl.BlockSpec((CHUNK, HEAD_W), lambda i, h: (i, s * N_HEADS + h))

    state_c = pl.BlockSpec((1, n_seq, 1, HEAD_W, HEAD_W), lambda i, h: (layer, i, h, 0, 0))
    state_n = pl.BlockSpec((n_seq, 1, 1, HEAD_W), lambda i, h: (i, h, 0, 0))
    in_specs = [seg(SEG_Q), seg(SEG_K), seg(SEG_VB), seg(SEG_OB), seg(SEG_ZB),
                pl.BlockSpec((8, CHUNK), lambda i, h: (0, i)),
                state_c, state_n,
                pl.BlockSpec((1, CHUNK, 1), lambda i, h: (h, i, 0)),
                pl.BlockSpec((1, HEAD_W), lambda i, h: (0, h))]
    args = [p_act, p_act, p_act, p_act, p_act, gates, c0_all, n0, m0_tok, mng]
    aliases = {}
    if c_new_all is not None:
        aliases = {len(args): 1}
        in_specs.append(pl.BlockSpec(memory_space=pl.ANY))
        args.append(c_new_all)
    return pl.pallas_call(
        functools.partial(_mlstm_sample_kernel, ts=ts),
        grid=(n // CHUNK, N_HEADS),
        in_specs=in_specs,
        out_specs=[pl.BlockSpec((CHUNK, HEAD_W), lambda i, h: (i, h)),
                   state_c, state_n,
                   pl.BlockSpec((n_seq, 1, 1, CHUNK), lambda i, h: (i, h, 0, 0))],
        out_shape=[jax.ShapeDtypeStruct((n, D_MODEL), BF16),
                   jax.ShapeDtypeStruct(c0_all.shape, F32),
                   jax.ShapeDtypeStruct((db, N_HEADS, 1, HEAD_W), F32),
                   jax.ShapeDtypeStruct((db, N_HEADS, 1, CHUNK), F32)],
        input_output_aliases=aliases,
        compiler_params=_params("arbitrary", "arbitrary"),
        name="mlstm_sample",
    )(*args)


def _merge_kernel(ya_ref, yb_ref, yc_ref, ga_ref, gb_ref, gc_ref, x_ref, gate_ref, wa_ref, wb_ref, wc_ref,
                  wo_ref, fng_ref, o_ref, *, mod_rows, final):
    merged = (ga_ref[...].astype(F32) * _dot(ya_ref[...], wa_ref[...])
              + gb_ref[...].astype(F32) * _dot(yb_ref[...], wb_ref[...])
              + gc_ref[...].astype(F32) * _dot(yc_ref[...], wc_ref[...]))
    n_rows = x_ref.shape[0]
    gate = jnp.broadcast_to(gate_ref[...], (n_rows // mod_rows, mod_rows, D_MODEL)).reshape(n_rows, D_MODEL)
    xn = x_ref[...] + gate * _dot(merged.astype(BF16), wo_ref[...])
    if final:
        ms = jnp.mean(xn * xn, axis=-1, keepdims=True)
        xn = (xn * lax.rsqrt(ms + EPS)) * fng_ref[...]
    o_ref[...] = xn


def _merge_call(ya, yb, yc, p_act, x, gate, wa, wb, wc, wo, fng, *, tm, rows_per_mod, final):
    n = x.shape[0]
    tok_spec = pl.BlockSpec((tm, D_MODEL), lambda i: (i, 0))
    gate_spec = pl.BlockSpec((tm // rows_per_mod, 1, D_MODEL), lambda i: (i, 0, 0))

    def seg(s):
        return pl.BlockSpec((tm, D_MODEL), lambda i: (i, s))

    w_spec = pl.BlockSpec((D_MODEL, D_MODEL), lambda i: (0, 0))
    return pl.pallas_call(
        functools.partial(_merge_kernel, mod_rows=rows_per_mod, final=final),
        grid=(n // tm,),
        in_specs=[tok_spec, tok_spec, tok_spec, seg(SEG_GA), seg(SEG_GB), seg(SEG_GC), tok_spec, gate_spec,
                  w_spec, w_spec, w_spec, w_spec, pl.BlockSpec((1, D_MODEL), lambda i: (0, 0))],
        out_specs=tok_spec,
        out_shape=jax.ShapeDtypeStruct((n, D_MODEL), F32),
        compiler_params=_params("arbitrary"),
        name="merge_out",
    )(ya, yb, yc, p_act, p_act, p_act, x, gate, wa, wb, wc, wo, fng)


def _row_tile(n, target):
    tile = min(n, target)
    assert n % tile == 0 and tile % CHUNK == 0, (n, tile)
    return tile


def kernel(x_prompt, x_sample, c_prompt, c_sample, state_mlstm_C, state_mlstm_n, state_mlstm_m, state_pool, w_mod, b_mod, norm_g, w_in, b_if, sgu_ln_g, sgu_ln_b, w_sgu, b_sgu, mlstm_norm_g, w_pool, pool_scale, w_br_a, w_br_b, w_br_c, w_out, final_norm_g):
    batch, seq, _ = x_prompt.shape
    db, ts, _ = x_sample.shape
    depth = w_mod.shape[0]
    assert seq % CHUNK == 0 and CHUNK % ts == 0 and ts & (ts - 1) == 0 and (db * ts) % CHUNK == 0
    n_p, n_s = batch * seq, db * ts

    n_mod = batch + db
    c_all = jnp.concatenate([c_prompt, c_sample, jnp.zeros((-n_mod % 8, D_MODEL), F32)], axis=0)
    mod = _mod_call(c_all, w_mod, b_mod)

    w_in_t = jnp.swapaxes(w_in, 1, 2)
    w_bf, w_gt = _wcast_call(w_in_t)
    square_w = _square_cast_call(w_br_a, w_br_b, w_br_c, w_out)

    xp = x_prompt.reshape(n_p, D_MODEL)
    xs = x_sample.reshape(n_s, D_MODEL)
    tm_p = _row_tile(seq, 2048)
    tm_s = _row_tile(n_s, 1024)
    tc_p = _row_tile(seq, 512)
    mlstm_chunks = min(seq // CHUNK, 8)
    assert (seq // CHUNK) % mlstm_chunks == 0
    tb_s = min(db, 32)
    fng = final_norm_g.reshape(1, D_MODEL)
    n_rep = CHUNK // ts

    outs = {k: [] for k in ("cp", "np", "mp", "bp", "ns", "ms", "bs", "vs")}
    c_s_all = None
    for l in range(depth):
        bif = b_if[l].reshape(8, 1)
        ng = norm_g[l].reshape(1, D_MODEL)
        lng = sgu_ln_g[l].reshape(1, D_MODEL)
        lnb = sgu_ln_b[l].reshape(1, D_MODEL)
        mng = mlstm_norm_g[l].reshape(1, D_MODEL)
        pscale = pool_scale[l].reshape(1, D_MODEL)
        wpool = w_pool[l].astype(BF16)
        wa, wb, wc, wo = (w[l] for w in square_w)
        final = l == depth - 1

        mod_p = mod[l, :batch].reshape(batch, 1, 3 * D_MODEL)
        shift_p, scale_p, gate_p = (mod_p[..., i * D_MODEL:(i + 1) * D_MODEL] for i in range(3))
        p_act, gates, pool_p = _inproj_call(xp, shift_p, scale_p, ng, w_bf, w_gt[l], bif, lng, lnb,
                                            layer=l, tm=tm_p, rows_per_mod=seq, emit_f32=False)
        yb, c_p, n_p_state, m_p = _mlstm_prompt_call(p_act, gates, mng, batch=batch, seq=seq,
                                                     n_chunks=mlstm_chunks)
        xp = _mix_merge_prompt_call(p_act, yb, xp, gate_p, w_sgu[l], b_sgu[l][:, :, None], wpool, pscale,
                                    wa, wb, wc, wo, fng, seq=seq, tc=tc_p, final=final)
        outs["cp"].append(c_p)
        outs["np"].append(n_p_state)
        outs["mp"].append(m_p[:, :N_HEADS, 0])
        outs["bp"].append(pool_p[:, 1:, :])

        mod_s = mod[l, batch:batch + db].reshape(db, 1, 3 * D_MODEL)
        shift_s, scale_s, gate_s = (mod_s[..., i * D_MODEL:(i + 1) * D_MODEL] for i in range(3))
        p_act, gates, va_f32, p_f32 = _inproj_call(xs, shift_s, scale_s, ng, w_bf, w_gt[l], bif, lng, lnb,
                                                   layer=l, tm=tm_s, rows_per_mod=ts, emit_f32=True)
        w_sgu_tiled = jnp.tile(w_sgu[l][:, :ts, :ts], (1, n_rep, n_rep))
        bs_tiled = jnp.tile(b_sgu[l][:, :ts], (1, n_rep))[:, :, None]
        ya, yc, pool_s = _sgu_pool_sample_call(p_act, p_f32, state_pool, w_sgu_tiled, bs_tiled, wpool, pscale,
                                               layer=l, tb=tb_s, ts=ts)
        m0_tok = jnp.repeat(state_mlstm_m[l], ts, axis=0).T[:, :, None]
        yb, c_s_all, n_s_state, m_s = _mlstm_sample_call(p_act, gates, state_mlstm_C,
                                                         state_mlstm_n[l][:, :, None, :], m0_tok, mng, c_s_all,
                                                         layer=l, ts=ts)
        xs = _merge_call(ya, yb, yc, p_act, xs, gate_s, wa, wb, wc, wo, fng,
                         tm=_row_tile(n_s, 512), rows_per_mod=ts, final=final)
        outs["ns"].append(n_s_state[:, :, 0, :])
        outs["ms"].append(m_s[:, :, 0, 0])
        outs["bs"].append(pool_s)
        outs["vs"].append(va_f32.reshape(db, ts, D_MODEL))

    y_prompt = xp.reshape(batch, seq, D_MODEL)
    y_sample = xs.reshape(db, ts, D_MODEL)
    return (y_prompt, y_sample,
            jnp.stack(outs["cp"]), jnp.stack(outs["np"]), jnp.stack(outs["mp"]), jnp.stack(outs["bp"]),
            c_s_all, jnp.stack(outs["ns"]), jnp.stack(outs["ms"]), jnp.stack(outs["bs"]),
            jnp.stack(outs["vs"]))
```

```python
import functools

import jax
import jax.numpy as jnp
from jax import lax
from jax.experimental import pallas as pl
from jax.experimental.pallas import tpu as pltpu

F32 = jnp.float32
BF16 = jnp.bfloat16

D_MODEL = 1024
N_HEADS = 4
HEAD_W = D_MODEL // N_HEADS
CHUNK = 128
POOL_WINDOWS = (2, 4, 8, 16)
POOL_BUF = 15
POOL_PAD = 16
POOL_HALO = 24
PAST_LEN = 16384
EPS = 1e-6
N_SEG = 13
SEG_GA, SEG_GB, SEG_GC, SEG_U, SEG_VA, SEG_ZA, SEG_Q, SEG_K, SEG_VB, SEG_OB, SEG_ZB, SEG_P, SEG_ZC = range(N_SEG)
N_ALIGNED_SEG = 11
N_GATE_COLS = 2 * N_HEADS
VMEM_LIMIT_BYTES = 56 * 1024 * 1024


def _params(*semantics):
    return pltpu.CompilerParams(dimension_semantics=semantics, vmem_limit_bytes=VMEM_LIMIT_BYTES)


def _dot(a, b):
    return jnp.dot(a, b, preferred_element_type=F32)


def _dot_nt(a, b):
    return lax.dot_general(a, b, (((1,), (1,)), ((), ())), preferred_element_type=F32)


def _sigmoid(x):
    return 0.5 * jnp.tanh(0.5 * x) + 0.5


def _silu(x):
    return x * _sigmoid(x)


def _log_sigmoid(x):
    return jnp.minimum(x, 0.0) - jnp.log(1.0 + jnp.exp(-jnp.abs(x)))


def _gelu(x):
    return jax.nn.gelu(x, approximate=True)


def _mod_kernel(c_ref, w_ref, b_ref, o_ref):
    c = c_ref[...]
    o_ref[0] = _dot(_silu(c).astype(BF16), w_ref[0].astype(BF16)) + b_ref[0]


def _mod_call(c_all, w_mod, b_mod):
    depth = w_mod.shape[0]
    rows = c_all.shape[0]
    return pl.pallas_call(
        _mod_kernel,
        grid=(depth, 3),
        in_specs=[pl.BlockSpec((rows, D_MODEL), lambda l, j: (0, 0)),
                  pl.BlockSpec((1, D_MODEL, D_MODEL), lambda l, j: (l, 0, j)),
                  pl.BlockSpec((1, 1, D_MODEL), lambda l, j: (l, 0, j))],
        out_specs=pl.BlockSpec((1, rows, D_MODEL), lambda l, j: (l, 0, j)),
        out_shape=jax.ShapeDtypeStruct((depth, rows, 3 * D_MODEL), F32),
        compiler_params=_params("arbitrary", "arbitrary"),
        name="adaln_mod",
    )(c_all, w_mod, b_mod.reshape(depth, 1, 3 * D_MODEL))


def _wcast_kernel(w_ref, wnext_ref, o_ref, gate_ref):
    j = pl.program_id(1)

    @pl.when(j < N_ALIGNED_SEG)
    def _():
        o_ref[0] = w_ref[0].T.astype(BF16)

    @pl.when(j == N_ALIGNED_SEG - 1)
    def _():
        gate_ref[...] = wnext_ref[...].astype(BF16)

    @pl.when(j >= N_ALIGNED_SEG)
    def _():
        rows = jnp.concatenate([w_ref[0][N_GATE_COLS:, :], wnext_ref[0]], axis=0)
        o_ref[0] = rows.T.astype(BF16)


def _wcast_call(w_in_t):
    depth = w_in_t.shape[0]
    next_blocks = D_MODEL // N_GATE_COLS
    return pl.pallas_call(
        _wcast_kernel,
        grid=(depth, N_SEG),
        in_specs=[pl.BlockSpec((1, D_MODEL, D_MODEL), lambda l, j: (l, j, 0)),
                  pl.BlockSpec((1, N_GATE_COLS, D_MODEL), lambda l, j: (l, (j + 1) * next_blocks, 0))],
        out_specs=[pl.BlockSpec((1, D_MODEL, D_MODEL), lambda l, j: (l, 0, j)),
                   pl.BlockSpec((1, N_GATE_COLS, D_MODEL), lambda l, j: (l, 0, 0))],
        out_shape=[jax.ShapeDtypeStruct((depth, D_MODEL, N_SEG * D_MODEL), BF16),
                   jax.ShapeDtypeStruct((depth, N_GATE_COLS, D_MODEL), BF16)],
        compiler_params=_params("arbitrary", "arbitrary"),
        name="w_in_bf16",
    )(w_in_t, w_in_t)


def _square_cast_kernel(a_ref, b_ref, c_ref, d_ref, ao_ref, bo_ref, co_ref, do_ref):
    for src, dst in ((a_ref, ao_ref), (b_ref, bo_ref), (c_ref, co_ref), (d_ref, do_ref)):
        dst[...] = src[...].astype(BF16)


def _square_cast_call(*weights):
    depth = weights[0].shape[0]
    rows = D_MODEL // 4
    spec = pl.BlockSpec((1, rows, D_MODEL), lambda l, r: (l, r, 0))
    return pl.pallas_call(
        _square_cast_kernel,
        grid=(depth, D_MODEL // rows),
        in_specs=[spec] * 4,
        out_specs=[spec] * 4,
        out_shape=[jax.ShapeDtypeStruct(w.shape, BF16) for w in weights],
        compiler_params=_params("arbitrary", "arbitrary"),
        name="branch_w_bf16",
    )(*weights)


def _inproj_kernel(x_ref, shift_ref, scale_ref, ng_ref, w_ref, wgt_ref, bif_ref, lng_ref, lnb_ref,
                   p_ref, g_ref, *rest, mod_rows, emit_f32, tm, sub, tiles_per_seq):
    if emit_f32:
        va_ref, pf_ref, h_scr = rest
    else:
        pool_ref, h_scr, pbuf, *level_bufs = rest
    j = pl.program_id(1)

    def normed_rows(rows):
        x = x_ref[rows, :]
        ms = jnp.mean(x * x, axis=-1, keepdims=True)
        y = (x * lax.rsqrt(ms + EPS)) * ng_ref[...]
        if mod_rows >= tm:
            sc, sh = scale_ref[0], shift_ref[0]
        else:
            n_rows = rows.stop - rows.start
            groups = slice(rows.start // mod_rows, rows.stop // mod_rows)
            sc, sh = (jnp.broadcast_to(ref[groups], (n_rows // mod_rows, mod_rows, D_MODEL)).reshape(n_rows, D_MODEL)
                      for ref in (scale_ref, shift_ref))
        h = (y * (1.0 + sc) + sh).astype(BF16)
        h_scr[rows, :] = h
        g_ref[:, rows] = _dot_nt(wgt_ref[...], h) + bif_ref[...]
        return h

    def project(epilogue, first=False):
        w = w_ref[0]
        step = max(sub // 2, 2 * CHUNK) if first else sub
        for r0 in range(0, tm, step):
            rows = slice(r0, r0 + step)
            h = normed_rows(rows) if first else h_scr[rows, :]
            epilogue(rows, _dot(h, w))

    def store(fn):
        def epilogue(rows, acc):
            p_ref[rows, :] = fn(acc).astype(BF16)
        return epilogue

    def va_epilogue(rows, acc):
        gl = _gelu(acc)
        mu = jnp.mean(gl, axis=-1, keepdims=True)
        ctr = gl - mu
        var = jnp.mean(ctr * ctr, axis=-1, keepdims=True)
        y = ctr * lax.rsqrt(var + EPS) * lng_ref[...] + lnb_ref[...]
        p_ref[rows, :] = y.astype(BF16)
        if emit_f32:
            va_ref[rows, :] = y

    def p_epilogue(rows, acc):
        p_ref[rows, :] = acc.astype(BF16)
        pf_ref[rows, :] = acc

    def pooled_epilogue(rows, acc):
        n_rows = rows.stop - rows.start
        tile_in_seq = pl.program_id(0) % tiles_per_seq
        sums = (pbuf,) + tuple(level_bufs)
        if rows.start == 0:
            @pl.when(tile_in_seq == 0)
            def _():
                for buf in sums:
                    buf[0:POOL_HALO, :] = jnp.zeros((POOL_HALO, buf.shape[1]), F32)
        pbuf[POOL_HALO:POOL_HALO + n_rows, :] = acc
        pos = tile_in_seq * tm + rows.start + lax.broadcasted_iota(jnp.int32, (n_rows, 1), 0)
        live = slice(POOL_HALO, POOL_HALO + n_rows)
        for g, w in enumerate(POOL_WINDOWS):
            prev = sums[g]
            shift = w // 2
            span = slice(8, POOL_HALO + n_rows)
            shifted = slice(8 - shift, POOL_HALO + n_rows - shift)
            if g + 1 < len(POOL_WINDOWS):
                sums[g + 1][span, :] = prev[span, HEAD_W:] + prev[shifted, HEAD_W:]
            total = prev[live, 0:HEAD_W] + prev[slice(live.start - shift, live.stop - shift), 0:HEAD_W]
            cnt = jnp.minimum(pos + 1, w).astype(F32)
            cols = slice(g * HEAD_W, (g + 1) * HEAD_W)
            p_ref[rows, cols] = (total / cnt - pbuf[live, cols]).astype(BF16)
        tail = pbuf[n_rows + POOL_HALO - POOL_PAD:n_rows + POOL_HALO, :]
        pool_ref[0] = tail
        pbuf[POOL_HALO - POOL_PAD:POOL_HALO, :] = tail

    pl.when(j == SEG_GA)(lambda: project(store(_sigmoid), first=True))
    pl.when((j == SEG_GB) | (j == SEG_GC) | (j == SEG_OB))(lambda: project(store(_sigmoid)))
    pl.when(j == SEG_U)(lambda: project(store(_gelu)))
    pl.when(j == SEG_VA)(lambda: project(va_epilogue))
    pl.when((j == SEG_ZA) | (j == SEG_ZB) | (j == SEG_ZC))(lambda: project(store(_silu)))
    pl.when((j == SEG_Q) | (j == SEG_VB))(lambda: project(store(lambda a: a)))
    pl.when(j == SEG_K)(lambda: project(store(lambda a: a * HEAD_W ** -0.5)))
    pl.when(j == SEG_P)(lambda: project(p_epilogue if emit_f32 else pooled_epilogue))


def _inproj_call(x, shift, scale, norm_g, w_bf, w_gt, b_if, ln_g, ln_b, *, layer, tm, rows_per_mod, emit_f32):
    n = x.shape[0]
    sub = min(tm, max(tm // 4, 2 * CHUNK))
    tiles_per_seq = None if emit_f32 else rows_per_mod // tm
    vec_spec = pl.BlockSpec((1, D_MODEL), lambda i, j: (0, 0))
    row_spec = pl.BlockSpec((tm, D_MODEL), lambda i, j: (i, 0))
    out_shape = [jax.ShapeDtypeStruct((n, N_SEG * D_MODEL), BF16), jax.ShapeDtypeStruct((8, n), F32)]
    out_specs = [pl.BlockSpec((tm, D_MODEL), lambda i, j: (i, j)), pl.BlockSpec((8, tm), lambda i, j: (0, i))]
    scratch = [pltpu.VMEM((tm, D_MODEL), BF16)]
    if emit_f32:
        mod_spec = pl.BlockSpec((tm // rows_per_mod, 1, D_MODEL), lambda i, j: (i, 0, 0))
        out_shape += [jax.ShapeDtypeStruct((n, D_MODEL), F32)] * 2
        out_specs += [row_spec, row_spec]
    else:
        mod_spec = pl.BlockSpec((1, 1, D_MODEL), lambda i, j: (i // tiles_per_seq, 0, 0))
        out_shape.append(jax.ShapeDtypeStruct((n // rows_per_mod, POOL_PAD, D_MODEL), F32))
        out_specs.append(pl.BlockSpec((1, POOL_PAD, D_MODEL), lambda i, j: (i // tiles_per_seq, 0, 0)))
        scratch += [pltpu.VMEM((POOL_HALO + sub, D_MODEL - g * HEAD_W), F32) for g in range(len(POOL_WINDOWS))]
    return pl.pallas_call(
        functools.partial(_inproj_kernel, mod_rows=rows_per_mod, emit_f32=emit_f32, tm=tm, sub=sub,
                          tiles_per_seq=tiles_per_seq),
        grid=(n // tm, N_SEG),
        in_specs=[row_spec, mod_spec, mod_spec, vec_spec,
                  pl.BlockSpec((1, D_MODEL, D_MODEL), lambda i, j: (layer, 0, j)),
                  pl.BlockSpec((8, D_MODEL), lambda i, j: (0, 0)),
                  pl.BlockSpec((8, 1), lambda i, j: (0, 0)),
                  vec_spec, vec_spec],
        out_specs=out_specs,
        out_shape=out_shape,
        scratch_shapes=scratch,
        compiler_params=_params("arbitrary", "arbitrary"),
        name="inproj",
    )(x, shift, scale, norm_g, w_bf, w_gt, b_if, ln_g, ln_b)


def _block_causal_masks(block_len):
    ti = lax.broadcasted_iota(jnp.int32, (CHUNK, CHUNK), 0)
    si = lax.broadcasted_iota(jnp.int32, (CHUNK, CHUNK), 1)
    shift = block_len.bit_length() - 1
    same = (ti >> shift) == (si >> shift)
    return same, same & (si <= ti), si == ti, si


def _sgu_rows(gu_ref, va_ref, sza_ref, wsgu_ref, bs_ref, ya_ref, causal, row_lo, row_hi):
    for g in range(N_HEADS):
        cols = slice(g * HEAD_W, (g + 1) * HEAD_W)
        wg = jnp.where(causal, wsgu_ref[g], 0.0).astype(BF16)
        bias = bs_ref[g]
        for r0 in range(row_lo, row_hi, CHUNK):
            rows = slice(r0, r0 + CHUNK)
            s = _dot(wg, va_ref[rows, cols]) + bias
            ya = gu_ref[rows, cols].astype(F32) * s * sza_ref[rows, cols].astype(F32)
            ya_ref[rows, cols] = ya.astype(BF16)


def _head_out(hh, o_blk, z_blk, mng_blk):
    hb = o_blk.astype(F32) * hh
    mu = jnp.mean(hb, axis=-1, keepdims=True)
    ctr = hb - mu
    var = jnp.mean(ctr * ctr, axis=-1, keepdims=True)
    y = ctr * lax.rsqrt(var + EPS) * mng_blk
    return (y * z_blk.astype(F32)).astype(BF16)


def _row_to_col(row, eye):
    return jnp.sum(jnp.where(eye, row, 0.0), axis=1, keepdims=True)


def _col_to_row(col, eye):
    return jnp.sum(jnp.where(eye, col, 0.0), axis=0, keepdims=True)


def _mix_merge_prompt_kernel(gu_ref, va_ref, sza_ref, pooled_ref, szc_ref, ga_ref, gb_ref, gc_ref, yb_ref, x_ref,
                             gate_ref, wsgu_ref, bs_ref, wpool_ref, pscale_ref, wa_ref, wb_ref, wc_ref, wo_ref,
                             fng_ref, o_ref, ya_ref, yc_ref, *, tc, sub, final):
    _, causal, _, _ = _block_causal_masks(CHUNK)

    def pool_project(rows):
        for g in range(N_HEADS):
            cols = slice(g * HEAD_W, (g + 1) * HEAD_W)
            pm = _dot(pooled_ref[rows, cols], wpool_ref[g])
            yc_ref[rows, cols] = (pm * pscale_ref[:, cols] * szc_ref[rows, cols].astype(F32)).astype(BF16)

    def gated(g_ref, y_ref, w_ref, rows):
        return g_ref[rows, :].astype(F32) * _dot(y_ref[rows, :], w_ref[...])

    for r0 in range(0, tc, sub):
        rows = slice(r0, r0 + sub)
        part_b = gated(gb_ref, yb_ref, wb_ref, rows)
        _sgu_rows(gu_ref, va_ref, sza_ref, wsgu_ref, bs_ref, ya_ref, causal, r0, r0 + sub)
        part_ab = gated(ga_ref, ya_ref, wa_ref, rows) + part_b
        pool_project(rows)
        merged = part_ab + gated(gc_ref, yc_ref, wc_ref, rows)
        xn = x_ref[rows, :] + gate_ref[0] * _dot(merged.astype(BF16), wo_ref[...])
        if final:
            ms = jnp.mean(xn * xn, axis=-1, keepdims=True)
            xn = (xn * lax.rsqrt(ms + EPS)) * fng_ref[...]
        o_ref[rows, :] = xn


def _mix_merge_prompt_call(p_act, yb, x, gate, w_sgu, bs_col, w_pool, pool_scale, wa, wb, wc, wo, fng,
                           *, seq, tc, final):
    n = x.shape[0]
    tiles_per_seq = seq // tc

    def seg(s):
        return pl.BlockSpec((tc, D_MODEL), lambda i: (i, s))

    tok_spec = pl.BlockSpec((tc, D_MODEL), lambda i: (i, 0))
    vec_spec = pl.BlockSpec((1, D_MODEL), lambda i: (0, 0))
    w_spec = pl.BlockSpec((D_MODEL, D_MODEL), lambda i: (0, 0), pipeline_mode=pl.Buffered(1))
    return pl.pallas_call(
        functools.partial(_mix_merge_prompt_kernel, tc=tc, sub=min(tc, 512), final=final),
        grid=(n // tc,),
        in_specs=[seg(SEG_U), seg(SEG_VA), seg(SEG_ZA), seg(SEG_P), seg(SEG_ZC),
                  seg(SEG_GA), seg(SEG_GB), seg(SEG_GC), tok_spec, tok_spec,
                  pl.BlockSpec((1, 1, D_MODEL), lambda i: (i // tiles_per_seq, 0, 0)),
                  pl.BlockSpec((N_HEADS, CHUNK, CHUNK), lambda i: (0, 0, 0)),
                  pl.BlockSpec((N_HEADS, CHUNK, 1), lambda i: (0, 0, 0)),
                  pl.BlockSpec((N_HEADS, HEAD_W, HEAD_W), lambda i: (0, 0, 0)),
                  vec_spec, w_spec, w_spec, w_spec, w_spec, vec_spec],
        out_specs=tok_spec,
        out_shape=jax.ShapeDtypeStruct((n, D_MODEL), F32),
        scratch_shapes=[pltpu.VMEM((tc, D_MODEL), BF16), pltpu.VMEM((tc, D_MODEL), BF16)],
        compiler_params=_params("arbitrary"),
        name="mix_merge_prompt",
    )(p_act, p_act, p_act, p_act, p_act, p_act, p_act, p_act, yb, x, gate, w_sgu, bs_col, w_pool, pool_scale,
      wa, wb, wc, wo, fng)


def _sgu_pool_sample_kernel(gu_ref, va_ref, sza_ref, szc_ref, pf_ref, buf_ref, wsgu_ref, bs_ref, wpool_ref,
                            pscale_ref, ya_ref, yc_ref, nbuf_ref, fscr, *, tb, ts, n_past):
    rows = tb * ts
    _, causal, _, _ = _block_causal_masks(ts)
    _sgu_rows(gu_ref, va_ref, sza_ref, wsgu_ref, bs_ref, ya_ref, causal, 0, rows)

    fscr[:, POOL_PAD - POOL_BUF:POOL_PAD, :] = buf_ref[0]
    fscr[:, POOL_PAD:, :] = pf_ref[...]
    pos = n_past + lax.broadcasted_iota(jnp.int32, (1, ts, 1), 1)
    for g, w in enumerate(POOL_WINDOWS):
        cols = slice(g * HEAD_W, (g + 1) * HEAD_W)
        cur = fscr[:, POOL_PAD:POOL_PAD + ts, cols]
        acc = cur
        for k in range(1, w):
            acc = acc + fscr[:, POOL_PAD - k:POOL_PAD - k + ts, cols]
        cnt = jnp.minimum(pos + 1, w).astype(F32)
        pooled = (acc / cnt - cur).reshape(rows, HEAD_W)
        pm = _dot(pooled.astype(BF16), wpool_ref[g])
        yc_ref[:, cols] = (pm * pscale_ref[:, cols] * szc_ref[:, cols].astype(F32)).astype(BF16)
    nbuf_ref[...] = fscr[:, POOL_PAD + ts - POOL_BUF:POOL_PAD + ts, :]


def _sgu_pool_sample_call(p_act, p_f32, pool_buf_all, w_sgu_tiled, bs_col_tiled, w_pool, pool_scale,
                          *, layer, tb, ts):
    n = p_act.shape[0]
    db = n // ts
    rows = tb * ts

    def seg(s):
        return pl.BlockSpec((rows, D_MODEL), lambda i: (i, s))

    tok_spec = pl.BlockSpec((rows, D_MODEL), lambda i: (i, 0))
    return pl.pallas_call(
        functools.partial(_sgu_pool_sample_kernel, tb=tb, ts=ts, n_past=PAST_LEN),
        grid=(db // tb,),
        in_specs=[seg(SEG_U), seg(SEG_VA), seg(SEG_ZA), seg(SEG_ZC),
                  pl.BlockSpec((tb, ts, D_MODEL), lambda i: (i, 0, 0)),
                  pl.BlockSpec((1, tb, POOL_BUF, D_MODEL), lambda i: (layer, i, 0, 0)),
                  pl.BlockSpec((N_HEADS, CHUNK, CHUNK), lambda i: (0, 0, 0)),
                  pl.BlockSpec((N_HEADS, CHUNK, 1), lambda i: (0, 0, 0)),
                  pl.BlockSpec((N_HEADS, HEAD_W, HEAD_W), lambda i: (0, 0, 0)),
                  pl.BlockSpec((1, D_MODEL), lambda i: (0, 0))],
        out_specs=[tok_spec, tok_spec, pl.BlockSpec((tb, POOL_BUF, D_MODEL), lambda i: (i, 0, 0))],
        out_shape=[jax.ShapeDtypeStruct((n, D_MODEL), BF16), jax.ShapeDtypeStruct((n, D_MODEL), BF16),
                   jax.ShapeDtypeStruct((db, POOL_BUF, D_MODEL), F32)],
        scratch_shapes=[pltpu.VMEM((tb, POOL_PAD + ts, D_MODEL), F32)],
        compiler_params=_params("arbitrary"),
        name="sgu_pool_sample",
    )(p_act, p_act, p_act, p_act, p_f32.reshape(db, ts, D_MODEL), pool_buf_all, w_sgu_tiled, bs_col_tiled, w_pool,
      pool_scale)


def _mlstm_prompt_kernel(q_ref, k_ref, v_ref, o_ref, z_ref, g_ref, mng_ref,
                         yb_ref, c_out, n_out, m_out, c_scr, n_scr, m_scr, *, n_chunks):
    step = pl.program_id(1)

    @pl.when(step == 0)
    def _():
        c_scr[...] = jnp.zeros(c_scr.shape, F32)
        n_scr[...] = jnp.zeros(n_scr.shape, F32)
        m_scr[...] = jnp.zeros(m_scr.shape, F32)

    _, causal, eye, _ = _block_causal_masks(CHUNK)
    heads = range(N_HEADS)
    probs = [(ci, h) for ci in range(n_chunks) for h in heads]

    def rows(p):
        return slice(p[0] * CHUNK, (p[0] + 1) * CHUNK)

    def cols(p):
        return slice(p[1] * HEAD_W, (p[1] + 1) * HEAD_W)

    def each(fn, keys=probs):
        return {p: fn(p) for p in keys}

    lower = jnp.where(causal, 1.0, 0.0).astype(BF16)
    upper = lower.T
    ones_sq = jnp.ones((CHUNK, CHUNK), BF16)
    mean_w = jnp.full((HEAD_W, CHUNK), 1.0 / HEAD_W, BF16)

    def split(x):
        hi = x.astype(BF16)
        return hi, (x - hi.astype(F32)).astype(BF16)

    def wide(x):
        return jnp.concatenate([x, x], axis=1)

    ig_all = g_ref[0:N_HEADS, :]
    lf_all = _log_sigmoid(g_ref[N_HEADS:2 * N_HEADS, :])
    b_col, r_col, num_intra, den_intra, b_end, r_end, c_upd, n_upd = ({} for _ in range(8))

    def independent(ps):
        lf_parts = each(lambda p: split(lf_all[p[1]:p[1] + 1, rows(p)]), ps)
        b_col.update(each(lambda p: sum(_dot_nt(lower, jnp.broadcast_to(part, (CHUNK, CHUNK)))
                                        for part in lf_parts[p]), ps))
        b_row = each(lambda p: sum(_dot(jnp.broadcast_to(part, (16, CHUNK)), upper)
                                   for part in lf_parts[p])[0:1, :], ps)
        a_row = each(lambda p: ig_all[p[1]:p[1] + 1, rows(p)] - b_row[p], ps)
        d_log = each(lambda p: jnp.where(causal, b_col[p] + a_row[p], -jnp.inf), ps)
        r_col.update(each(lambda p: jnp.max(d_log[p], axis=1, keepdims=True), ps))
        qk = each(lambda p: _dot_nt(q_ref[rows(p), cols(p)], k_ref[rows(p), cols(p)]), ps)
        s = each(lambda p: (qk[p] * jnp.exp(d_log[p] - r_col[p])).astype(BF16), ps)
        num_intra.update(each(lambda p: _dot(s[p], v_ref[rows(p), cols(p)]), ps))
        den_intra.update(each(lambda p: _dot(s[p], ones_sq), ps))
        b_end.update(each(lambda p: b_row[p][:, CHUNK - 1:CHUNK], ps))
        log_end = each(lambda p: b_end[p] + a_row[p], ps)
        r_end.update(each(lambda p: jnp.max(log_end[p], axis=1, keepdims=True), ps))
        w_row = each(lambda p: jnp.broadcast_to(jnp.exp(log_end[p] - r_end[p]).astype(BF16), (16, CHUNK)), ps)
        kw_t = each(lambda p: k_ref[rows(p), cols(p)].T * w_row[p][0:1, :], ps)
        c_upd.update(each(lambda p: _dot(kw_t[p], v_ref[rows(p), cols(p)]), ps))
        n_upd.update(each(lambda p: _dot(w_row[p], k_ref[rows(p), cols(p)])[0:1, :], ps))

    independent(probs)

    for ci in range(n_chunks):
        ps = [(ci, h) for h in heads]
        m_prev = each(lambda p: m_scr[p[1]:p[1] + 1, 0:1], ps)
        inter = each(lambda p: b_col[p] + m_prev[p], ps)
        m_t = each(lambda p: jnp.maximum(inter[p], r_col[p]), ps)
        w_intra = each(lambda p: jnp.exp(r_col[p] - m_t[p]), ps)
        w_inter = each(lambda p: jnp.exp(inter[p] - m_t[p]), ps)
        qc = each(lambda p: _dot(q_ref[rows(p), cols(p)], c_scr[p[1]].astype(BF16)), ps)
        qn = each(lambda p: _dot_nt(q_ref[rows(p), cols(p)],
                                    jnp.broadcast_to(n_scr[p[1]:p[1] + 1, :].astype(BF16), (CHUNK, HEAD_W))), ps)
        den = each(lambda p: w_inter[p] * qn[p] + w_intra[p] * den_intra[p], ps)
        inv = each(lambda p: 1.0 / jnp.maximum(jnp.abs(den[p]), jnp.exp(-m_t[p])), ps)
        hh = each(lambda p: wide(w_inter[p] * inv[p]) * qc[p] + wide(w_intra[p] * inv[p]) * num_intra[p], ps)
        hb = each(lambda p: o_ref[rows(p), cols(p)].astype(F32) * hh[p], ps)
        ctr = each(lambda p: hb[p] - wide(_dot(hb[p].astype(BF16), mean_w)), ps)
        var = each(lambda p: _dot((ctr[p] * ctr[p]).astype(BF16), mean_w), ps)
        for p in ps:
            y = ctr[p] * wide(lax.rsqrt(var[p] + EPS)) * mng_ref[:, cols(p)]
            yb_ref[rows(p), cols(p)] = (y * z_ref[rows(p), cols(p)].astype(F32)).astype(BF16)
        m_new = each(lambda p: jnp.maximum(b_end[p] + m_prev[p], r_end[p]), ps)
        decay = each(lambda p: jnp.exp(b_end[p] + m_prev[p] - m_new[p]), ps)
        w_end = each(lambda p: jnp.exp(r_end[p] - m_new[p]), ps)
        for p in ps:
            h = p[1]
            c_scr[h] = decay[p] * c_scr[h] + w_end[p] * c_upd[p]
            n_scr[h:h + 1, :] = decay[p] * n_scr[h:h + 1, :] + w_end[p] * n_upd[p]
            m_scr[h:h + 1, :] = jnp.broadcast_to(m_new[p], (1, CHUNK))

    @pl.when(step == pl.num_programs(1) - 1)
    def _():
        c_out[0] = c_scr[...]
        n_out[0] = n_scr[0:N_HEADS, :]
        m_out[0] = m_scr[...]


def _mlstm_prompt_call(p_act, gates, mng, *, batch, seq, n_chunks):
    n = batch * seq
    rows = n_chunks * CHUNK
    steps = seq // rows

    def seg(s):
        return pl.BlockSpec((rows, D_MODEL), lambda b, c: (b * steps + c, s))

    return pl.pallas_call(
        functools.partial(_mlstm_prompt_kernel, n_chunks=n_chunks),
        grid=(batch, steps),
        in_specs=[seg(SEG_Q), seg(SEG_K), seg(SEG_VB), seg(SEG_OB), seg(SEG_ZB),
                  pl.BlockSpec((8, rows), lambda b, c: (0, b * steps + c)),
                  pl.BlockSpec((1, D_MODEL), lambda b, c: (0, 0))],
        out_specs=[pl.BlockSpec((rows, D_MODEL), lambda b, c: (b * steps + c, 0)),
                   pl.BlockSpec((1, N_HEADS, HEAD_W, HEAD_W), lambda b, c: (b, 0, 0, 0)),
                   pl.BlockSpec((1, N_HEADS, HEAD_W), lambda b, c: (b, 0, 0)),
                   pl.BlockSpec((1, 8, CHUNK), lambda b, c: (b, 0, 0))],
        out_shape=[jax.ShapeDtypeStruct((n, D_MODEL), BF16),
                   jax.ShapeDtypeStruct((batch, N_HEADS, HEAD_W, HEAD_W), F32),
                   jax.ShapeDtypeStruct((batch, N_HEADS, HEAD_W), F32),
                   jax.ShapeDtypeStruct((batch, 8, CHUNK), F32)],
        scratch_shapes=[pltpu.VMEM((N_HEADS, HEAD_W, HEAD_W), F32), pltpu.VMEM((8, HEAD_W), F32),
                        pltpu.VMEM((8, CHUNK), F32)],
        compiler_params=_params("arbitrary", "arbitrary"),
        name="mlstm_prompt",
    )(p_act, p_act, p_act, p_act, p_act, gates, mng)


def _mlstm_sample_kernel(q_ref, k_ref, v_ref, o_ref, z_ref, g_ref, c0_ref, n0_ref, m0_ref, mng_ref, *rest, ts):
    yb_ref, c_out, n_out, m_out = rest[-4:]
    h = pl.program_id(1)
    n_seq = CHUNK // ts
    same, causal, eye, si = _block_causal_masks(ts)
    ig_row = g_ref[pl.ds(h, 1), :]
    lf_row = _log_sigmoid(g_ref[pl.ds(N_HEADS + h, 1), :])
    b_col = jnp.sum(jnp.where(causal, lf_row, 0.0), axis=1, keepdims=True)
    b_row = _col_to_row(b_col, eye)
    a_row = ig_row - b_row
    m_col = m0_ref[0]
    d_log = jnp.where(causal, b_col + a_row, -jnp.inf)
    inter = b_col + m_col
    m_t = jnp.maximum(inter, jnp.max(d_log, axis=1, keepdims=True))
    w_intra = jnp.exp(d_log - m_t)
    w_inter = jnp.exp(inter - m_t)
    q = q_ref[...]
    k = k_ref[...]
    v = v_ref[...]
    s = _dot_nt(q, k) * w_intra
    num_inter = jnp.concatenate(
        [_dot(q[b * ts:(b + 1) * ts, :], c0_ref[0, b, 0].astype(BF16)) for b in range(n_seq)], axis=0)
    n_tok = jnp.concatenate([jnp.broadcast_to(n0_ref[b, 0], (ts, HEAD_W)) for b in range(n_seq)], axis=0)
    num = w_inter * num_inter + _dot(s.astype(BF16), v)
    qn = jnp.sum(q.astype(F32) * n_tok, axis=1, keepdims=True)
    den = w_inter * qn + jnp.sum(s, axis=1, keepdims=True)
    hh = num / jnp.maximum(jnp.abs(den), jnp.exp(-m_t))
    yb_ref[...] = _head_out(hh, o_ref[...], z_ref[...], mng_ref[...])

    last_in_seq = same & ((si & (ts - 1)) == ts - 1)
    b_end_col = jnp.sum(jnp.where(last_in_seq, b_row, 0.0), axis=1, keepdims=True)
    b_end_row = _col_to_row(b_end_col, eye)
    log_end_row = b_end_row + a_row
    seq_max_col = jnp.max(jnp.where(same, log_end_row, -jnp.inf), axis=1, keepdims=True)
    m_new_col = jnp.maximum(b_end_col + m_col, seq_max_col)
    decay_col = jnp.exp(b_end_col + m_col - m_new_col)
    m_new_row = _col_to_row(m_new_col, eye)
    w_end_col = _row_to_col(jnp.exp(log_end_row - m_new_row), eye)
    kw = k.astype(F32) * w_end_col
    kw_t = kw.T
    lane_seq = lax.broadcasted_iota(jnp.int32, (1, CHUNK), 1) >> (ts.bit_length() - 1)
    for b in range(n_seq):
        decay = decay_col[b * ts:b * ts + 1, :]
        kw_b = jnp.where(lane_seq == b, kw_t, 0.0).astype(BF16)
        c_out[0, b, 0] = decay * c0_ref[0, b, 0] + _dot(kw_b, v)
        n_out[b, 0] = decay * n0_ref[b, 0] + jnp.sum(kw[b * ts:(b + 1) * ts, :], axis=0, keepdims=True)
        m_out[b, 0] = jnp.broadcast_to(m_new_col[b * ts:b * ts + 1, :], (1, CHUNK))


def _mlstm_sample_call(p_act, gates, c0_all, n0, m0_tok, mng, c_new_all, *, layer, ts):
    n = p_act.shape[0]
    db = n // ts
    n_seq = CHUNK // ts

    def seg(s):
        return pl.BlockSpec((CHUNK, HEAD_W), lambda i, h: (i, s * N_HEADS + h))

    state_c = pl.BlockSpec((1, n_seq, 1, HEAD_W, HEAD_W), lambda i, h: (layer, i, h, 0, 0))
    state_n = pl.BlockSpec((n_seq, 1, 1, HEAD_W), lambda i, h: (i, h, 0, 0))
    in_specs = [seg(SEG_Q), seg(SEG_K), seg(SEG_VB), seg(SEG_OB), seg(SEG_ZB),
                pl.BlockSpec((8, CHUNK), lambda i, h: (0, i)),
                state_c, state_n,
                pl.BlockSpec((1, CHUNK, 1), lambda i, h: (h, i, 0)),
                pl.BlockSpec((1, HEAD_W), lambda i, h: (0, h))]
    args = [p_act, p_act, p_act, p_act, p_act, gates, c0_all, n0, m0_tok, mng]
    aliases = {}
    if c_new_all is not None:
        aliases = {len(args): 1}
        in_specs.append(pl.BlockSpec(memory_space=pl.ANY))
        args.append(c_new_all)
    return pl.pallas_call(
        functools.partial(_mlstm_sample_kernel, ts=ts),
        grid=(n // CHUNK, N_HEADS),
        in_specs=in_specs,
        out_specs=[pl.BlockSpec((CHUNK, HEAD_W), lambda i, h: (i, h)),
                   state_c, state_n,
                   pl.BlockSpec((n_seq, 1, 1, CHUNK), lambda i, h: (i, h, 0, 0))],
        out_shape=[jax.ShapeDtypeStruct((n, D_MODEL), BF16),
                   jax.ShapeDtypeStruct(c0_all.shape, F32),
                   jax.ShapeDtypeStruct((db, N_HEADS, 1, HEAD_W), F32),
                   jax.ShapeDtypeStruct((db, N_HEADS, 1, CHUNK), F32)],
        input_output_aliases=aliases,
        compiler_params=_params("arbitrary", "arbitrary"),
        name="mlstm_sample",
    )(*args)


def _merge_kernel(ya_ref, yb_ref, yc_ref, ga_ref, gb_ref, gc_ref, x_ref, gate_ref, wa_ref, wb_ref, wc_ref,
                  wo_ref, fng_ref, o_ref, *, mod_rows, final):
    merged = (ga_ref[...].astype(F32) * _dot(ya_ref[...], wa_ref[...])
              + gb_ref[...].astype(F32) * _dot(yb_ref[...], wb_ref[...])
              + gc_ref[...].astype(F32) * _dot(yc_ref[...], wc_ref[...]))
    n_rows = x_ref.shape[0]
    gate = jnp.broadcast_to(gate_ref[...], (n_rows // mod_rows, mod_rows, D_MODEL)).reshape(n_rows, D_MODEL)
    xn = x_ref[...] + gate * _dot(merged.astype(BF16), wo_ref[...])
    if final:
        ms = jnp.mean(xn * xn, axis=-1, keepdims=True)
        xn = (xn * lax.rsqrt(ms + EPS)) * fng_ref[...]
    o_ref[...] = xn


def _merge_call(ya, yb, yc, p_act, x, gate, wa, wb, wc, wo, fng, *, tm, rows_per_mod, final):
    n = x.shape[0]
    tok_spec = pl.BlockSpec((tm, D_MODEL), lambda i: (i, 0))
    gate_spec = pl.BlockSpec((tm // rows_per_mod, 1, D_MODEL), lambda i: (i, 0, 0))

    def seg(s):
        return pl.BlockSpec((tm, D_MODEL), lambda i: (i, s))

    w_spec = pl.BlockSpec((D_MODEL, D_MODEL), lambda i: (0, 0))
    return pl.pallas_call(
        functools.partial(_merge_kernel, mod_rows=rows_per_mod, final=final),
        grid=(n // tm,),
        in_specs=[tok_spec, tok_spec, tok_spec, seg(SEG_GA), seg(SEG_GB), seg(SEG_GC), tok_spec, gate_spec,
                  w_spec, w_spec, w_spec, w_spec, pl.BlockSpec((1, D_MODEL), lambda i: (0, 0))],
        out_specs=tok_spec,
        out_shape=jax.ShapeDtypeStruct((n, D_MODEL), F32),
        compiler_params=_params("arbitrary"),
        name="merge_out",
    )(ya, yb, yc, p_act, p_act, p_act, x, gate, wa, wb, wc, wo, fng)


def _row_tile(n, target):
    tile = min(n, target)
    assert n % tile == 0 and tile % CHUNK == 0, (n, tile)
    return tile


def kernel(x_prompt, x_sample, c_prompt, c_sample, state_mlstm_C, state_mlstm_n, state_mlstm_m, state_pool, w_mod, b_mod, norm_g, w_in, b_if, sgu_ln_g, sgu_ln_b, w_sgu, b_sgu, mlstm_norm_g, w_pool, pool_scale, w_br_a, w_br_b, w_br_c, w_out, final_norm_g):
    batch, seq, _ = x_prompt.shape
    db, ts, _ = x_sample.shape
    depth = w_mod.shape[0]
    assert seq % CHUNK == 0 and CHUNK % ts == 0 and ts & (ts - 1) == 0 and (db * ts) % CHUNK == 0
    n_p, n_s = batch * seq, db * ts

    n_mod = batch + db
    c_all = jnp.concatenate([c_prompt, c_sample, jnp.zeros((-n_mod % 8, D_MODEL), F32)], axis=0)
    mod = _mod_call(c_all, w_mod, b_mod)

    w_in_t = jnp.swapaxes(w_in, 1, 2)
    w_bf, w_gt = _wcast_call(w_in_t)
    square_w = _square_cast_call(w_br_a, w_br_b, w_br_c, w_out)

    xp = x_prompt.reshape(n_p, D_MODEL)
    xs = x_sample.reshape(n_s, D_MODEL)
    tm_p = _row_tile(seq, 2048)
    tm_s = _row_tile(n_s, 1024)
    tc_p = _row_tile(seq, 512)
    mlstm_chunks = min(seq // CHUNK, 8)
    assert (seq // CHUNK) % mlstm_chunks == 0
    tb_s = min(db, 32)
    fng = final_norm_g.reshape(1, D_MODEL)
    n_rep = CHUNK // ts

    outs = {k: [] for k in ("cp", "np", "mp", "bp", "ns", "ms", "bs", "vs")}
    c_s_all = None
    for l in range(depth):
        bif = b_if[l].reshape(8, 1)
        ng = norm_g[l].reshape(1, D_MODEL)
        lng = sgu_ln_g[l].reshape(1, D_MODEL)
        lnb = sgu_ln_b[l].reshape(1, D_MODEL)
        mng = mlstm_norm_g[l].reshape(1, D_MODEL)
        pscale = pool_scale[l].reshape(1, D_MODEL)
        wpool = w_pool[l].astype(BF16)
        wa, wb, wc, wo = (w[l] for w in square_w)
        final = l == depth - 1

        mod_p = mod[l, :batch].reshape(batch, 1, 3 * D_MODEL)
        shift_p, scale_p, gate_p = (mod_p[..., i * D_MODEL:(i + 1) * D_MODEL] for i in range(3))
        p_act, gates, pool_p = _inproj_call(xp, shift_p, scale_p, ng, w_bf, w_gt[l], bif, lng, lnb,
                                            layer=l, tm=tm_p, rows_per_mod=seq, emit_f32=False)
        yb, c_p, n_p_state, m_p = _mlstm_prompt_call(p_act, gates, mng, batch=batch, seq=seq,
                                                     n_chunks=mlstm_chunks)
        xp = _mix_merge_prompt_call(p_act, yb, xp, gate_p, w_sgu[l], b_sgu[l][:, :, None], wpool, pscale,
                                    wa, wb, wc, wo, fng, seq=seq, tc=tc_p, final=final)
        outs["cp"].append(c_p)
        outs["np"].append(n_p_state)
        outs["mp"].append(m_p[:, :N_HEADS, 0])
        outs["bp"].append(pool_p[:, 1:, :])

        mod_s = mod[l, batch:batch + db].reshape(db, 1, 3 * D_MODEL)
        shift_s, scale_s, gate_s = (mod_s[..., i * D_MODEL:(i + 1) * D_MODEL] for i in range(3))
        p_act, gates, va_f32, p_f32 = _inproj_call(xs, shift_s, scale_s, ng, w_bf, w_gt[l], bif, lng, lnb,
                                                   layer=l, tm=tm_s, rows_per_mod=ts, emit_f32=True)
        w_sgu_tiled = jnp.tile(w_sgu[l][:, :ts, :ts], (1, n_rep, n_rep))
        bs_tiled = jnp.tile(b_sgu[l][:, :ts], (1, n_rep))[:, :, None]
        ya, yc, pool_s = _sgu_pool_sample_call(p_act, p_f32, state_pool, w_sgu_tiled, bs_tiled, wpool, pscale,
                                               layer=l, tb=tb_s, ts=ts)
        m0_tok = jnp.repeat(state_mlstm_m[l], ts, axis=0).T[:, :, None]
        yb, c_s_all, n_s_state, m_s = _mlstm_sample_call(p_act, gates, state_mlstm_C,
                                                         state_mlstm_n[l][:, :, None, :], m0_tok, mng, c_s_all,
                                                         layer=l, ts=ts)
        xs = _merge_call(ya, yb, yc, p_act, xs, gate_s, wa, wb, wc, wo, fng,
                         tm=_row_tile(n_s, 512), rows_per_mod=ts, final=final)
        outs["ns"].append(n_s_state[:, :, 0, :])
        outs["ms"].append(m_s[:, :, 0, 0])
        outs["bs"].append(pool_s)
        outs["vs"].append(va_f32.reshape(db, ts, D_MODEL))

    y_prompt = xp.reshape(batch, seq, D_MODEL)
    y_sample = xs.reshape(db, ts, D_MODEL)
    return (y_prompt, y_sample,
            jnp.stack(outs["cp"]), jnp.stack(outs["np"]), jnp.stack(outs["mp"]), jnp.stack(outs["bp"]),
            c_s_all, jnp.stack(outs["ns"]), jnp.stack(outs["ms"]), jnp.stack(outs["bs"]),
            jnp.stack(outs["vs"]))
```

```python
import functools

import jax
import jax.numpy as jnp
from jax import lax
from jax.experimental import pallas as pl
from jax.experimental.pallas import tpu as pltpu

F32 = jnp.float32
BF16 = jnp.bfloat16

D_MODEL = 1024
N_HEADS = 4
HEAD_W = D_MODEL // N_HEADS
CHUNK = 128
POOL_WINDOWS = (2, 4, 8, 16)
POOL_BUF = 15
POOL_PAD = 16
POOL_HALO = 24
PAST_LEN = 16384
EPS = 1e-6
N_SEG = 13
SEG_GA, SEG_GB, SEG_GC, SEG_U, SEG_VA, SEG_ZA, SEG_Q, SEG_K, SEG_VB, SEG_OB, SEG_ZB, SEG_P, SEG_ZC = range(N_SEG)
N_ALIGNED_SEG = 11
N_GATE_COLS = 2 * N_HEADS
VMEM_LIMIT_BYTES = 56 * 1024 * 1024


def _params(*semantics):
    return pltpu.CompilerParams(dimension_semantics=semantics, vmem_limit_bytes=VMEM_LIMIT_BYTES)


def _dot(a, b):
    return jnp.dot(a, b, preferred_element_type=F32)


def _dot_nt(a, b):
    return lax.dot_general(a, b, (((1,), (1,)), ((), ())), preferred_element_type=F32)


def _sigmoid(x):
    return 0.5 * jnp.tanh(0.5 * x) + 0.5


def _silu(x):
    return x * _sigmoid(x)


def _log_sigmoid(x):
    return jnp.minimum(x, 0.0) - jnp.log(1.0 + jnp.exp(-jnp.abs(x)))


def _gelu(x):
    return jax.nn.gelu(x, approximate=True)


def _mod_kernel(c_ref, w_ref, b_ref, o_ref):
    c = c_ref[...]
    o_ref[0] = _dot(_silu(c).astype(BF16), w_ref[0].astype(BF16)) + b_ref[0]


def _mod_call(c_all, w_mod, b_mod):
    depth = w_mod.shape[0]
    rows = c_all.shape[0]
    return pl.pallas_call(
        _mod_kernel,
        grid=(depth, 3),
        in_specs=[pl.BlockSpec((rows, D_MODEL), lambda l, j: (0, 0)),
                  pl.BlockSpec((1, D_MODEL, D_MODEL), lambda l, j: (l, 0, j)),
                  pl.BlockSpec((1, 1, D_MODEL), lambda l, j: (l, 0, j))],
        out_specs=pl.BlockSpec((1, rows, D_MODEL), lambda l, j: (l, 0, j)),
        out_shape=jax.ShapeDtypeStruct((depth, rows, 3 * D_MODEL), F32),
        compiler_params=_params("arbitrary", "arbitrary"),
        name="adaln_mod",
    )(c_all, w_mod, b_mod.reshape(depth, 1, 3 * D_MODEL))


def _wcast_kernel(w_ref, wnext_ref, o_ref, gate_ref):
    j = pl.program_id(1)

    @pl.when(j < N_ALIGNED_SEG)
    def _():
        o_ref[0] = w_ref[0].T.astype(BF16)

    @pl.when(j == N_ALIGNED_SEG - 1)
    def _():
        gate_ref[...] = wnext_ref[...].astype(BF16)

    @pl.when(j >= N_ALIGNED_SEG)
    def _():
        rows = jnp.concatenate([w_ref[0][N_GATE_COLS:, :], wnext_ref[0]], axis=0)
        o_ref[0] = rows.T.astype(BF16)


def _wcast_call(w_in_t):
    depth = w_in_t.shape[0]
    next_blocks = D_MODEL // N_GATE_COLS
    return pl.pallas_call(
        _wcast_kernel,
        grid=(depth, N_SEG),
        in_specs=[pl.BlockSpec((1, D_MODEL, D_MODEL), lambda l, j: (l, j, 0)),
                  pl.BlockSpec((1, N_GATE_COLS, D_MODEL), lambda l, j: (l, (j + 1) * next_blocks, 0))],
        out_specs=[pl.BlockSpec((1, D_MODEL, D_MODEL), lambda l, j: (l, 0, j)),
                   pl.BlockSpec((1, N_GATE_COLS, D_MODEL), lambda l, j: (l, 0, 0))],
        out_shape=[jax.ShapeDtypeStruct((depth, D_MODEL, N_SEG * D_MODEL), BF16),
                   jax.ShapeDtypeStruct((depth, N_GATE_COLS, D_MODEL), BF16)],
        compiler_params=_params("arbitrary", "arbitrary"),
        name="w_in_bf16",
    )(w_in_t, w_in_t)


def _square_cast_kernel(a_ref, b_ref, c_ref, d_ref, ao_ref, bo_ref, co_ref, do_ref):
    for src, dst in ((a_ref, ao_ref), (b_ref, bo_ref), (c_ref, co_ref), (d_ref, do_ref)):
        dst[...] = src[...].astype(BF16)


def _square_cast_call(*weights):
    depth = weights[0].shape[0]
    rows = D_MODEL // 4
    spec = pl.BlockSpec((1, rows, D_MODEL), lambda l, r: (l, r, 0))
    return pl.pallas_call(
        _square_cast_kernel,
        grid=(depth, D_MODEL // rows),
        in_specs=[spec] * 4,
        out_specs=[spec] * 4,
        out_shape=[jax.ShapeDtypeStruct(w.shape, BF16) for w in weights],
        compiler_params=_params("arbitrary", "arbitrary"),
        name="branch_w_bf16",
    )(*weights)


def _inproj_kernel(x_ref, shift_ref, scale_ref, ng_ref, w_ref, wgt_ref, bif_ref, lng_ref, lnb_ref,
                   p_ref, g_ref, *rest, mod_rows, emit_f32, tm, sub, tiles_per_seq):
    if emit_f32:
        va_ref, pf_ref, h_scr = rest
    else:
        pool_ref, h_scr, pbuf, *level_bufs = rest
    j = pl.program_id(1)

    def normed_rows(rows):
        x = x_ref[rows, :]
        ms = jnp.mean(x * x, axis=-1, keepdims=True)
        y = (x * lax.rsqrt(ms + EPS)) * ng_ref[...]
        if mod_rows >= tm:
            sc, sh = scale_ref[0], shift_ref[0]
        else:
            n_rows = rows.stop - rows.start
            groups = slice(rows.start // mod_rows, rows.stop // mod_rows)
            sc, sh = (jnp.broadcast_to(ref[groups], (n_rows // mod_rows, mod_rows, D_MODEL)).reshape(n_rows, D_MODEL)
                      for ref in (scale_ref, shift_ref))
        h = (y * (1.0 + sc) + sh).astype(BF16)
        h_scr[rows, :] = h
        g_ref[:, rows] = _dot_nt(wgt_ref[...], h) + bif_ref[...]
        return h

    def project(epilogue, first=False):
        w = w_ref[0]
        step = max(sub // 2, 2 * CHUNK) if first else sub
        for r0 in range(0, tm, step):
            rows = slice(r0, r0 + step)
            h = normed_rows(rows) if first else h_scr[rows, :]
            epilogue(rows, _dot(h, w))

    def store(fn):
        def epilogue(rows, acc):
            p_ref[rows, :] = fn(acc).astype(BF16)
        return epilogue

    def va_epilogue(rows, acc):
        gl = _gelu(acc)
        mu = jnp.mean(gl, axis=-1, keepdims=True)
        ctr = gl - mu
        var = jnp.mean(ctr * ctr, axis=-1, keepdims=True)
        y = ctr * lax.rsqrt(var + EPS) * lng_ref[...] + lnb_ref[...]
        p_ref[rows, :] = y.astype(BF16)
        if emit_f32:
            va_ref[rows, :] = y

    def p_epilogue(rows, acc):
        p_ref[rows, :] = acc.astype(BF16)
        pf_ref[rows, :] = acc

    def pooled_epilogue(rows, acc):
        n_rows = rows.stop - rows.start
        tile_in_seq = pl.program_id(0) % tiles_per_seq
        sums = (pbuf,) + tuple(level_bufs)
        if rows.start == 0:
            @pl.when(tile_in_seq == 0)
            def _():
                for buf in sums:
                    buf[0:POOL_HALO, :] = jnp.zeros((POOL_HALO, buf.shape[1]), F32)
        pbuf[POOL_HALO:POOL_HALO + n_rows, :] = acc
        pos = tile_in_seq * tm + rows.start + lax.broadcasted_iota(jnp.int32, (n_rows, 1), 0)
        live = slice(POOL_HALO, POOL_HALO + n_rows)
        for g, w in enumerate(POOL_WINDOWS):
            prev = sums[g]
            shift = w // 2
            span = slice(8, POOL_HALO + n_rows)
            shifted = slice(8 - shift, POOL_HALO + n_rows - shift)
            if g + 1 < len(POOL_WINDOWS):
                sums[g + 1][span, :] = prev[span, HEAD_W:] + prev[shifted, HEAD_W:]
            total = prev[live, 0:HEAD_W] + prev[slice(live.start - shift, live.stop - shift), 0:HEAD_W]
            cnt = jnp.minimum(pos + 1, w).astype(F32)
            cols = slice(g * HEAD_W, (g + 1) * HEAD_W)
            p_ref[rows, cols] = (total / cnt - pbuf[live, cols]).astype(BF16)
        tail = pbuf[n_rows + POOL_HALO - POOL_PAD:n_rows + POOL_HALO, :]
        pool_ref[0] = tail
        pbuf[POOL_HALO - POOL_PAD:POOL_HALO, :] = tail

    pl.when(j == SEG_GA)(lambda: project(store(_sigmoid), first=True))
    pl.when((j == SEG_GB) | (j == SEG_GC) | (j == SEG_OB))(lambda: project(store(_sigmoid)))
    pl.when(j == SEG_U)(lambda: project(store(_gelu)))
    pl.when(j == SEG_VA)(lambda: project(va_epilogue))
    pl.when((j == SEG_ZA) | (j == SEG_ZB) | (j == SEG_ZC))(lambda: project(store(_silu)))
    pl.when((j == SEG_Q) | (j == SEG_VB))(lambda: project(store(lambda a: a)))
    pl.when(j == SEG_K)(lambda: project(store(lambda a: a * HEAD_W ** -0.5)))
    pl.when(j == SEG_P)(lambda: project(p_epilogue if emit_f32 else pooled_epilogue))


def _inproj_call(x, shift, scale, norm_g, w_bf, w_gt, b_if, ln_g, ln_b, *, layer, tm, rows_per_mod, emit_f32):
    n = x.shape[0]
    sub = min(tm, max(tm // 4, 2 * CHUNK))
    tiles_per_seq = None if emit_f32 else rows_per_mod // tm
    vec_spec = pl.BlockSpec((1, D_MODEL), lambda i, j: (0, 0))
    row_spec = pl.BlockSpec((tm, D_MODEL), lambda i, j: (i, 0))
    out_shape = [jax.ShapeDtypeStruct((n, N_SEG * D_MODEL), BF16), jax.ShapeDtypeStruct((8, n), F32)]
    out_specs = [pl.BlockSpec((tm, D_MODEL), lambda i, j: (i, j)), pl.BlockSpec((8, tm), lambda i, j: (0, i))]
    scratch = [pltpu.VMEM((tm, D_MODEL), BF16)]
    if emit_f32:
        mod_spec = pl.BlockSpec((tm // rows_per_mod, 1, D_MODEL), lambda i, j: (i, 0, 0))
        out_shape += [jax.ShapeDtypeStruct((n, D_MODEL), F32)] * 2
        out_specs += [row_spec, row_spec]
    else:
        mod_spec = pl.BlockSpec((1, 1, D_MODEL), lambda i, j: (i // tiles_per_seq, 0, 0))
        out_shape.append(jax.ShapeDtypeStruct((n // rows_per_mod, POOL_PAD, D_MODEL), F32))
        out_specs.append(pl.BlockSpec((1, POOL_PAD, D_MODEL), lambda i, j: (i // tiles_per_seq, 0, 0)))
        scratch += [pltpu.VMEM((POOL_HALO + sub, D_MODEL - g * HEAD_W), F32) for g in range(len(POOL_WINDOWS))]
    return pl.pallas_call(
        functools.partial(_inproj_kernel, mod_rows=rows_per_mod, emit_f32=emit_f32, tm=tm, sub=sub,
                          tiles_per_seq=tiles_per_seq),
        grid=(n // tm, N_SEG),
        in_specs=[row_spec, mod_spec, mod_spec, vec_spec,
                  pl.BlockSpec((1, D_MODEL, D_MODEL), lambda i, j: (layer, 0, j)),
                  pl.BlockSpec((8, D_MODEL), lambda i, j: (0, 0)),
                  pl.BlockSpec((8, 1), lambda i, j: (0, 0)),
                  vec_spec, vec_spec],
        out_specs=out_specs,
        out_shape=out_shape,
        scratch_shapes=scratch,
        compiler_params=_params("arbitrary", "arbitrary"),
        name="inproj",
    )(x, shift, scale, norm_g, w_bf, w_gt, b_if, ln_g, ln_b)


def _block_causal_masks(block_len):
    ti = lax.broadcasted_iota(jnp.int32, (CHUNK, CHUNK), 0)
    si = lax.broadcasted_iota(jnp.int32, (CHUNK, CHUNK), 1)
    shift = block_len.bit_length() - 1
    same = (ti >> shift) == (si >> shift)
    return same, same & (si <= ti), si == ti, si


def _sgu_rows(gu_ref, va_ref, sza_ref, wsgu_ref, bs_ref, ya_ref, causal, row_lo, row_hi):
    for g in range(N_HEADS):
        cols = slice(g * HEAD_W, (g + 1) * HEAD_W)
        wg = jnp.where(causal, wsgu_ref[g], 0.0).astype(BF16)
        bias = bs_ref[g]
        for r0 in range(row_lo, row_hi, CHUNK):
            rows = slice(r0, r0 + CHUNK)
            s = _dot(wg, va_ref[rows, cols]) + bias
            ya = gu_ref[rows, cols].astype(F32) * s * sza_ref[rows, cols].astype(F32)
            ya_ref[rows, cols] = ya.astype(BF16)


def _head_out(hh, o_blk, z_blk, mng_blk):
    hb = o_blk.astype(F32) * hh
    mu = jnp.mean(hb, axis=-1, keepdims=True)
    ctr = hb - mu
    var = jnp.mean(ctr * ctr, axis=-1, keepdims=True)
    y = ctr * lax.rsqrt(var + EPS) * mng_blk
    return (y * z_blk.astype(F32)).astype(BF16)


def _row_to_col(row, eye):
    return jnp.sum(jnp.where(eye, row, 0.0), axis=1, keepdims=True)


def _col_to_row(col, eye):
    return jnp.sum(jnp.where(eye, col, 0.0), axis=0, keepdims=True)


def _mix_merge_prompt_kernel(gu_ref, va_ref, sza_ref, pooled_ref, szc_ref, ga_ref, gb_ref, gc_ref, yb_ref, x_ref,
                             gate_ref, wsgu_ref, bs_ref, wpool_ref, pscale_ref, wa_ref, wb_ref, wc_ref, wo_ref,
                             fng_ref, o_ref, ya_ref, yc_ref, *, tc, sub, final):
    _, causal, _, _ = _block_causal_masks(CHUNK)

    def pool_project(rows):
        for g in range(N_HEADS):
            cols = slice(g * HEAD_W, (g + 1) * HEAD_W)
            pm = _dot(pooled_ref[rows, cols], wpool_ref[g])
            yc_ref[rows, cols] = (pm * pscale_ref[:, cols] * szc_ref[rows, cols].astype(F32)).astype(BF16)

    def gated(g_ref, y_ref, w_ref, rows):
        return g_ref[rows, :].astype(F32) * _dot(y_ref[rows, :], w_ref[...])

    for r0 in range(0, tc, sub):
        rows = slice(r0, r0 + sub)
        part_b = gated(gb_ref, yb_ref, wb_ref, rows)
        _sgu_rows(gu_ref, va_ref, sza_ref, wsgu_ref, bs_ref, ya_ref, causal, r0, r0 + sub)
        part_ab = gated(ga_ref, ya_ref, wa_ref, rows) + part_b
        pool_project(rows)
        merged = part_ab + gated(gc_ref, yc_ref, wc_ref, rows)
        xn = x_ref[rows, :] + gate_ref[0] * _dot(merged.astype(BF16), wo_ref[...])
        if final:
            ms = jnp.mean(xn * xn, axis=-1, keepdims=True)
            xn = (xn * lax.rsqrt(ms + EPS)) * fng_ref[...]
        o_ref[rows, :] = xn


def _mix_merge_prompt_call(p_act, yb, x, gate, w_sgu, bs_col, w_pool, pool_scale, wa, wb, wc, wo, fng,
                           *, seq, tc, final):
    n = x.shape[0]
    tiles_per_seq = seq // tc

    def seg(s):
        return pl.BlockSpec((tc, D_MODEL), lambda i: (i, s))

    tok_spec = pl.BlockSpec((tc, D_MODEL), lambda i: (i, 0))
    vec_spec = pl.BlockSpec((1, D_MODEL), lambda i: (0, 0))
    w_spec = pl.BlockSpec((D_MODEL, D_MODEL), lambda i: (0, 0), pipeline_mode=pl.Buffered(1))
    return pl.pallas_call(
        functools.partial(_mix_merge_prompt_kernel, tc=tc, sub=min(tc, 512), final=final),
        grid=(n // tc,),
        in_specs=[seg(SEG_U), seg(SEG_VA), seg(SEG_ZA), seg(SEG_P), seg(SEG_ZC),
                  seg(SEG_GA), seg(SEG_GB), seg(SEG_GC), tok_spec, tok_spec,
                  pl.BlockSpec((1, 1, D_MODEL), lambda i: (i // tiles_per_seq, 0, 0)),
                  pl.BlockSpec((N_HEADS, CHUNK, CHUNK), lambda i: (0, 0, 0)),
                  pl.BlockSpec((N_HEADS, CHUNK, 1), lambda i: (0, 0, 0)),
                  pl.BlockSpec((N_HEADS, HEAD_W, HEAD_W), lambda i: (0, 0, 0)),
                  vec_spec, w_spec, w_spec, w_spec, w_spec, vec_spec],
        out_specs=tok_spec,
        out_shape=jax.ShapeDtypeStruct((n, D_MODEL), F32),
        scratch_shapes=[pltpu.VMEM((tc, D_MODEL), BF16), pltpu.VMEM((tc, D_MODEL), BF16)],
        compiler_params=_params("arbitrary"),
        name="mix_merge_prompt",
    )(p_act, p_act, p_act, p_act, p_act, p_act, p_act, p_act, yb, x, gate, w_sgu, bs_col, w_pool, pool_scale,
      wa, wb, wc, wo, fng)


def _sgu_pool_sample_kernel(gu_ref, va_ref, sza_ref, szc_ref, pf_ref, buf_ref, wsgu_ref, bs_ref, wpool_ref,
                            pscale_ref, ya_ref, yc_ref, nbuf_ref, fscr, *, tb, ts, n_past):
    rows = tb * ts
    _, causal, _, _ = _block_causal_masks(ts)
    _sgu_rows(gu_ref, va_ref, sza_ref, wsgu_ref, bs_ref, ya_ref, causal, 0, rows)

    fscr[:, POOL_PAD - POOL_BUF:POOL_PAD, :] = buf_ref[0]
    fscr[:, POOL_PAD:, :] = pf_ref[...]
    pos = n_past + lax.broadcasted_iota(jnp.int32, (1, ts, 1), 1)
    for g, w in enumerate(POOL_WINDOWS):
        cols = slice(g * HEAD_W, (g + 1) * HEAD_W)
        cur = fscr[:, POOL_PAD:POOL_PAD + ts, cols]
        acc = cur
        for k in range(1, w):
            acc = acc + fscr[:, POOL_PAD - k:POOL_PAD - k + ts, cols]
        cnt = jnp.minimum(pos + 1, w).astype(F32)
        pooled = (acc / cnt - cur).reshape(rows, HEAD_W)
        pm = _dot(pooled.astype(BF16), wpool_ref[g])
        yc_ref[:, cols] = (pm * pscale_ref[:, cols] * szc_ref[:, cols].astype(F32)).astype(BF16)
    nbuf_ref[...] = fscr[:, POOL_PAD + ts - POOL_BUF:POOL_PAD + ts, :]


def _sgu_pool_sample_call(p_act, p_f32, pool_buf_all, w_sgu_tiled, bs_col_tiled, w_pool, pool_scale,
                          *, layer, tb, ts):
    n = p_act.shape[0]
    db = n // ts
    rows = tb * ts

    def seg(s):
        return pl.BlockSpec((rows, D_MODEL), lambda i: (i, s))

    tok_spec = pl.BlockSpec((rows, D_MODEL), lambda i: (i, 0))
    return pl.pallas_call(
        functools.partial(_sgu_pool_sample_kernel, tb=tb, ts=ts, n_past=PAST_LEN),
        grid=(db // tb,),
        in_specs=[seg(SEG_U), seg(SEG_VA), seg(SEG_ZA), seg(SEG_ZC),
                  pl.BlockSpec((tb, ts, D_MODEL), lambda i: (i, 0, 0)),
                  pl.BlockSpec((1, tb, POOL_BUF, D_MODEL), lambda i: (layer, i, 0, 0)),
                  pl.BlockSpec((N_HEADS, CHUNK, CHUNK), lambda i: (0, 0, 0)),
                  pl.BlockSpec((N_HEADS, CHUNK, 1), lambda i: (0, 0, 0)),
                  pl.BlockSpec((N_HEADS, HEAD_W, HEAD_W), lambda i: (0, 0, 0)),
                  pl.BlockSpec((1, D_MODEL), lambda i: (0, 0))],
        out_specs=[tok_spec, tok_spec, pl.BlockSpec((tb, POOL_BUF, D_MODEL), lambda i: (i, 0, 0))],
        out_shape=[jax.ShapeDtypeStruct((n, D_MODEL), BF16), jax.ShapeDtypeStruct((n, D_MODEL), BF16),
                   jax.ShapeDtypeStruct((db, POOL_BUF, D_MODEL), F32)],
        scratch_shapes=[pltpu.VMEM((tb, POOL_PAD + ts, D_MODEL), F32)],
        compiler_params=_params("arbitrary"),
        name="sgu_pool_sample",
    )(p_act, p_act, p_act, p_act, p_f32.reshape(db, ts, D_MODEL), pool_buf_all, w_sgu_tiled, bs_col_tiled, w_pool,
      pool_scale)


def _mlstm_prompt_kernel(q_ref, k_ref, v_ref, o_ref, z_ref, g_ref, mng_ref,
                         yb_ref, c_out, n_out, m_out, c_scr, n_scr, m_scr, *, n_chunks):
    step = pl.program_id(1)

    @pl.when(step == 0)
    def _():
        c_scr[...] = jnp.zeros(c_scr.shape, F32)
        n_scr[...] = jnp.zeros(n_scr.shape, F32)
        m_scr[...] = jnp.zeros(m_scr.shape, F32)

    _, causal, eye, _ = _block_causal_masks(CHUNK)
    heads = range(N_HEADS)
    probs = [(ci, h) for ci in range(n_chunks) for h in heads]

    def rows(p):
        return slice(p[0] * CHUNK, (p[0] + 1) * CHUNK)

    def cols(p):
        return slice(p[1] * HEAD_W, (p[1] + 1) * HEAD_W)

    def each(fn, keys=probs):
        return {p: fn(p) for p in keys}

    lower = jnp.where(causal, 1.0, 0.0).astype(BF16)
    upper = lower.T
    ones_sq = jnp.ones((CHUNK, CHUNK), BF16)
    mean_w = jnp.full((HEAD_W, CHUNK), 1.0 / HEAD_W, BF16)

    def split(x):
        hi = x.astype(BF16)
        return hi, (x - hi.astype(F32)).astype(BF16)

    def wide(x):
        return jnp.concatenate([x, x], axis=1)

    ig_all = g_ref[0:N_HEADS, :]
    lf_all = _log_sigmoid(g_ref[N_HEADS:2 * N_HEADS, :])
    b_col, r_col, num_intra, den_intra, b_end, r_end, kw_t, w_row = ({} for _ in range(8))

    def independent(ps):
        lf_parts = each(lambda p: split(lf_all[p[1]:p[1] + 1, rows(p)]), ps)
        b_col.update(each(lambda p: sum(_dot_nt(lower, jnp.broadcast_to(part, (CHUNK, CHUNK)))
                                        for part in lf_parts[p]), ps))
        b_row = each(lambda p: sum(_dot(jnp.broadcast_to(part, (16, CHUNK)), upper)
                                   for part in lf_parts[p])[0:1, :], ps)
        a_row = each(lambda p: ig_all[p[1]:p[1] + 1, rows(p)] - b_row[p], ps)
        d_log = each(lambda p: jnp.where(causal, b_col[p] + a_row[p], -jnp.inf), ps)
        r_col.update(each(lambda p: jnp.max(d_log[p], axis=1, keepdims=True), ps))
        qk = each(lambda p: _dot_nt(q_ref[rows(p), cols(p)], k_ref[rows(p), cols(p)]), ps)
        s = each(lambda p: (qk[p] * jnp.exp(d_log[p] - r_col[p])).astype(BF16), ps)
        num_intra.update(each(lambda p: _dot(s[p], v_ref[rows(p), cols(p)]), ps))
        den_intra.update(each(lambda p: _dot(s[p], ones_sq), ps))
        b_end.update(each(lambda p: b_row[p][:, CHUNK - 1:CHUNK], ps))
        log_end = each(lambda p: b_end[p] + a_row[p], ps)
        r_end.update(each(lambda p: jnp.max(log_end[p], axis=1, keepdims=True), ps))
        w_row.update(each(lambda p: jnp.broadcast_to(jnp.exp(log_end[p] - r_end[p]).astype(BF16), (16, CHUNK)), ps))
        kw_t.update(each(lambda p: k_ref[rows(p), cols(p)].T * w_row[p][0:1, :], ps))

    independent(probs)

    for ci in range(n_chunks):
        ps = [(ci, h) for h in heads]
        m_prev = each(lambda p: m_scr[p[1]:p[1] + 1, 0:1], ps)
        inter = each(lambda p: b_col[p] + m_prev[p], ps)
        m_t = each(lambda p: jnp.maximum(inter[p], r_col[p]), ps)
        w_intra = each(lambda p: jnp.exp(r_col[p] - m_t[p]), ps)
        w_inter = each(lambda p: jnp.exp(inter[p] - m_t[p]), ps)
        qc = each(lambda p: _dot(q_ref[rows(p), cols(p)], c_scr[p[1]].astype(BF16)), ps)
        qn = each(lambda p: _dot_nt(q_ref[rows(p), cols(p)],
                                    jnp.broadcast_to(n_scr[p[1]:p[1] + 1, :].astype(BF16), (CHUNK, HEAD_W))), ps)
        den = each(lambda p: w_inter[p] * qn[p] + w_intra[p] * den_intra[p], ps)
        inv = each(lambda p: 1.0 / jnp.maximum(jnp.abs(den[p]), jnp.exp(-m_t[p])), ps)
        hh = each(lambda p: wide(w_inter[p] * inv[p]) * qc[p] + wide(w_intra[p] * inv[p]) * num_intra[p], ps)
        hb = each(lambda p: o_ref[rows(p), cols(p)].astype(F32) * hh[p], ps)
        ctr = each(lambda p: hb[p] - wide(_dot(hb[p].astype(BF16), mean_w)), ps)
        var = each(lambda p: _dot((ctr[p] * ctr[p]).astype(BF16), mean_w), ps)
        for p in ps:
            y = ctr[p] * wide(lax.rsqrt(var[p] + EPS)) * mng_ref[:, cols(p)]
            yb_ref[rows(p), cols(p)] = (y * z_ref[rows(p), cols(p)].astype(F32)).astype(BF16)
        m_new = each(lambda p: jnp.maximum(b_end[p] + m_prev[p], r_end[p]), ps)
        decay = each(lambda p: jnp.exp(b_end[p] + m_prev[p] - m_new[p]), ps)
        w_end = each(lambda p: jnp.exp(r_end[p] - m_new[p]).astype(BF16), ps)
        c_upd = each(lambda p: _dot(kw_t[p] * w_end[p], v_ref[rows(p), cols(p)]), ps)
        n_upd = each(lambda p: _dot(w_row[p] * w_end[p], k_ref[rows(p), cols(p)])[0:1, :], ps)
        for p in ps:
            h = p[1]
            c_scr[h] = decay[p] * c_scr[h] + c_upd[p]
            n_scr[h:h + 1, :] = decay[p] * n_scr[h:h + 1, :] + n_upd[p]
            m_scr[h:h + 1, :] = jnp.broadcast_to(m_new[p], (1, CHUNK))

    @pl.when(step == pl.num_programs(1) - 1)
    def _():
        c_out[0] = c_scr[...]
        n_out[0] = n_scr[0:N_HEADS, :]
        m_out[0] = m_scr[...]


def _mlstm_prompt_call(p_act, gates, mng, *, batch, seq, n_chunks):
    n = batch * seq
    rows = n_chunks * CHUNK
    steps = seq // rows

    def seg(s):
        return pl.BlockSpec((rows, D_MODEL), lambda b, c: (b * steps + c, s))

    return pl.pallas_call(
        functools.partial(_mlstm_prompt_kernel, n_chunks=n_chunks),
        grid=(batch, steps),
        in_specs=[seg(SEG_Q), seg(SEG_K), seg(SEG_VB), seg(SEG_OB), seg(SEG_ZB),
                  pl.BlockSpec((8, rows), lambda b, c: (0, b * steps + c)),
                  pl.BlockSpec((1, D_MODEL), lambda b, c: (0, 0))],
        out_specs=[pl.BlockSpec((rows, D_MODEL), lambda b, c: (b * steps + c, 0)),
                   pl.BlockSpec((1, N_HEADS, HEAD_W, HEAD_W), lambda b, c: (b, 0, 0, 0)),
                   pl.BlockSpec((1, N_HEADS, HEAD_W), lambda b, c: (b, 0, 0)),
                   pl.BlockSpec((1, 8, CHUNK), lambda b, c: (b, 0, 0))],
        out_shape=[jax.ShapeDtypeStruct((n, D_MODEL), BF16),
                   jax.ShapeDtypeStruct((batch, N_HEADS, HEAD_W, HEAD_W), F32),
                   jax.ShapeDtypeStruct((batch, N_HEADS, HEAD_W), F32),
                   jax.ShapeDtypeStruct((batch, 8, CHUNK), F32)],
        scratch_shapes=[pltpu.VMEM((N_HEADS, HEAD_W, HEAD_W), F32), pltpu.VMEM((8, HEAD_W), F32),
                        pltpu.VMEM((8, CHUNK), F32)],
        compiler_params=_params("arbitrary", "arbitrary"),
        name="mlstm_prompt",
    )(p_act, p_act, p_act, p_act, p_act, gates, mng)


def _mlstm_sample_kernel(q_ref, k_ref, v_ref, o_ref, z_ref, g_ref, c0_ref, n0_ref, m0_ref, mng_ref, *rest, ts):
    yb_ref, c_out, n_out, m_out = rest[-4:]
    h = pl.program_id(1)
    n_seq = CHUNK // ts
    same, causal, eye, si = _block_causal_masks(ts)
    ig_row = g_ref[pl.ds(h, 1), :]
    lf_row = _log_sigmoid(g_ref[pl.ds(N_HEADS + h, 1), :])
    b_col = jnp.sum(jnp.where(causal, lf_row, 0.0), axis=1, keepdims=True)
    b_row = _col_to_row(b_col, eye)
    a_row = ig_row - b_row
    m_col = m0_ref[0]
    d_log = jnp.where(causal, b_col + a_row, -jnp.inf)
    inter = b_col + m_col
    m_t = jnp.maximum(inter, jnp.max(d_log, axis=1, keepdims=True))
    w_intra = jnp.exp(d_log - m_t)
    w_inter = jnp.exp(inter - m_t)
    q = q_ref[...]
    k = k_ref[...]
    v = v_ref[...]
    s = _dot_nt(q, k) * w_intra
    num_inter = jnp.concatenate(
        [_dot(q[b * ts:(b + 1) * ts, :], c0_ref[0, b, 0].astype(BF16)) for b in range(n_seq)], axis=0)
    n_tok = jnp.concatenate([jnp.broadcast_to(n0_ref[b, 0], (ts, HEAD_W)) for b in range(n_seq)], axis=0)
    num = w_inter * num_inter + _dot(s.astype(BF16), v)
    qn = jnp.sum(q.astype(F32) * n_tok, axis=1, keepdims=True)
    den = w_inter * qn + jnp.sum(s, axis=1, keepdims=True)
    hh = num / jnp.maximum(jnp.abs(den), jnp.exp(-m_t))
    yb_ref[...] = _head_out(hh, o_ref[...], z_ref[...], mng_ref[...])

    last_in_seq = same & ((si & (ts - 1)) == ts - 1)
    b_end_col = jnp.sum(jnp.where(last_in_seq, b_row, 0.0), axis=1, keepdims=True)
    b_end_row = _col_to_row(b_end_col, eye)
    log_end_row = b_end_row + a_row
    seq_max_col = jnp.max(jnp.where(same, log_end_row, -jnp.inf), axis=1, keepdims=True)
    m_new_col = jnp.maximum(b_end_col + m_col, seq_max_col)
    decay_col = jnp.exp(b_end_col + m_col - m_new_col)
    m_new_row = _col_to_row(m_new_col, eye)
    w_end_col = _row_to_col(jnp.exp(log_end_row - m_new_row), eye)
    kw = k.astype(F32) * w_end_col
    kw_t = kw.T
    lane_seq = lax.broadcasted_iota(jnp.int32, (1, CHUNK), 1) >> (ts.bit_length() - 1)
    for b in range(n_seq):
        decay = decay_col[b * ts:b * ts + 1, :]
        kw_b = jnp.where(lane_seq == b, kw_t, 0.0).astype(BF16)
        c_out[0, b, 0] = decay * c0_ref[0, b, 0] + _dot(kw_b, v)
        n_out[b, 0] = decay * n0_ref[b, 0] + jnp.sum(kw[b * ts:(b + 1) * ts, :], axis=0, keepdims=True)
        m_out[b, 0] = jnp.broadcast_to(m_new_col[b * ts:b * ts + 1, :], (1, CHUNK))


def _mlstm_sample_call(p_act, gates, c0_all, n0, m0_tok, mng, c_new_all, *, layer, ts):
    n = p_act.shape[0]
    db = n // ts
    n_seq = CHUNK // ts

    def seg(s):
        return pl.BlockSpec((CHUNK, HEAD_W), lambda i, h: (i, s * N_HEADS + h))

    state_c = pl.BlockSpec((1, n_seq, 1, HEAD_W, HEAD_W), lambda i, h: (layer, i, h, 0, 0))
    state_n = pl.BlockSpec((n_seq, 1, 1, HEAD_W), lambda i, h: (i, h, 0, 0))
    in_specs = [seg(SEG_Q), seg(SEG_K), seg(SEG_VB), seg(SEG_OB), seg(SEG_ZB),
                pl.BlockSpec((8, CHUNK), lambda i, h: (0, i)),
                state_c, state_n,
                pl.BlockSpec((1, CHUNK, 1), lambda i, h: (h, i, 0)),
                pl.BlockSpec((1, HEAD_W), lambda i, h: (0, h))]
    args = [p_act, p_act, p_act, p_act, p_act, gates, c0_all, n0, m0_tok, mng]
    aliases = {}
    if c_new_all is not None:
        aliases = {len(args): 1}
        in_specs.append(pl.BlockSpec(memory_space=pl.ANY))
        args.append(c_new_all)
    return pl.pallas_call(
        functools.partial(_mlstm_sample_kernel, ts=ts),
        grid=(n // CHUNK, N_HEADS),
        in_specs=in_specs,
        out_specs=[pl.BlockSpec((CHUNK, HEAD_W), lambda i, h: (i, h)),
                   state_c, state_n,
                   pl.BlockSpec((n_seq, 1, 1, CHUNK), lambda i, h: (i, h, 0, 0))],
        out_shape=[jax.ShapeDtypeStruct((n, D_MODEL), BF16),
                   jax.ShapeDtypeStruct(c0_all.shape, F32),
                   jax.ShapeDtypeStruct((db, N_HEADS, 1, HEAD_W), F32),
                   jax.ShapeDtypeStruct((db, N_HEADS, 1, CHUNK), F32)],
        input_output_aliases=aliases,
        compiler_params=_params("arbitrary", "arbitrary"),
        name="mlstm_sample",
    )(*args)


def _merge_kernel(ya_ref, yb_ref, yc_ref, ga_ref, gb_ref, gc_ref, x_ref, gate_ref, wa_ref, wb_ref, wc_ref,
                  wo_ref, fng_ref, o_ref, *, mod_rows, final):
    merged = (ga_ref[...].astype(F32) * _dot(ya_ref[...], wa_ref[...])
              + gb_ref[...].astype(F32) * _dot(yb_ref[...], wb_ref[...])
              + gc_ref[...].astype(F32) * _dot(yc_ref[...], wc_ref[...]))
    n_rows = x_ref.shape[0]
    gate = jnp.broadcast_to(gate_ref[...], (n_rows // mod_rows, mod_rows, D_MODEL)).reshape(n_rows, D_MODEL)
    xn = x_ref[...] + gate * _dot(merged.astype(BF16), wo_ref[...])
    if final:
        ms = jnp.mean(xn * xn, axis=-1, keepdims=True)
        xn = (xn * lax.rsqrt(ms + EPS)) * fng_ref[...]
    o_ref[...] = xn


def _merge_call(ya, yb, yc, p_act, x, gate, wa, wb, wc, wo, fng, *, tm, rows_per_mod, final):
    n = x.shape[0]
    tok_spec = pl.BlockSpec((tm, D_MODEL), lambda i: (i, 0))
    gate_spec = pl.BlockSpec((tm // rows_per_mod, 1, D_MODEL), lambda i: (i, 0, 0))

    def seg(s):
        return pl.BlockSpec((tm, D_MODEL), lambda i: (i, s))

    w_spec = pl.BlockSpec((D_MODEL, D_MODEL), lambda i: (0, 0))
    return pl.pallas_call(
        functools.partial(_merge_kernel, mod_rows=rows_per_mod, final=final),
        grid=(n // tm,),
        in_specs=[tok_spec, tok_spec, tok_spec, seg(SEG_GA), seg(SEG_GB), seg(SEG_GC), tok_spec, gate_spec,
                  w_spec, w_spec, w_spec, w_spec, pl.BlockSpec((1, D_MODEL), lambda i: (0, 0))],
        out_specs=tok_spec,
        out_shape=jax.ShapeDtypeStruct((n, D_MODEL), F32),
        compiler_params=_params("arbitrary"),
        name="merge_out",
    )(ya, yb, yc, p_act, p_act, p_act, x, gate, wa, wb, wc, wo, fng)


def _row_tile(n, target):
    tile = min(n, target)
    assert n % tile == 0 and tile % CHUNK == 0, (n, tile)
    return tile


def kernel(x_prompt, x_sample, c_prompt, c_sample, state_mlstm_C, state_mlstm_n, state_mlstm_m, state_pool, w_mod, b_mod, norm_g, w_in, b_if, sgu_ln_g, sgu_ln_b, w_sgu, b_sgu, mlstm_norm_g, w_pool, pool_scale, w_br_a, w_br_b, w_br_c, w_out, final_norm_g):
    batch, seq, _ = x_prompt.shape
    db, ts, _ = x_sample.shape
    depth = w_mod.shape[0]
    assert seq % CHUNK == 0 and CHUNK % ts == 0 and ts & (ts - 1) == 0 and (db * ts) % CHUNK == 0
    n_p, n_s = batch * seq, db * ts

    n_mod = batch + db
    c_all = jnp.concatenate([c_prompt, c_sample, jnp.zeros((-n_mod % 8, D_MODEL), F32)], axis=0)
    mod = _mod_call(c_all, w_mod, b_mod)

    w_in_t = jnp.swapaxes(w_in, 1, 2)
    w_bf, w_gt = _wcast_call(w_in_t)
    square_w = _square_cast_call(w_br_a, w_br_b, w_br_c, w_out)

    xp = x_prompt.reshape(n_p, D_MODEL)
    xs = x_sample.reshape(n_s, D_MODEL)
    tm_p = _row_tile(seq, 2048)
    tm_s = _row_tile(n_s, 1024)
    tc_p = _row_tile(seq, 512)
    mlstm_chunks = min(seq // CHUNK, 8)
    assert (seq // CHUNK) % mlstm_chunks == 0
    tb_s = min(db, 32)
    fng = final_norm_g.reshape(1, D_MODEL)
    n_rep = CHUNK // ts

    outs = {k: [] for k in ("cp", "np", "mp", "bp", "ns", "ms", "bs", "vs")}
    c_s_all = None
    for l in range(depth):
        bif = b_if[l].reshape(8, 1)
        ng = norm_g[l].reshape(1, D_MODEL)
        lng = sgu_ln_g[l].reshape(1, D_MODEL)
        lnb = sgu_ln_b[l].reshape(1, D_MODEL)
        mng = mlstm_norm_g[l].reshape(1, D_MODEL)
        pscale = pool_scale[l].reshape(1, D_MODEL)
        wpool = w_pool[l].astype(BF16)
        wa, wb, wc, wo = (w[l] for w in square_w)
        final = l == depth - 1

        mod_p = mod[l, :batch].reshape(batch, 1, 3 * D_MODEL)
        shift_p, scale_p, gate_p = (mod_p[..., i * D_MODEL:(i + 1) * D_MODEL] for i in range(3))
        p_act, gates, pool_p = _inproj_call(xp, shift_p, scale_p, ng, w_bf, w_gt[l], bif, lng, lnb,
                                            layer=l, tm=tm_p, rows_per_mod=seq, emit_f32=False)
        yb, c_p, n_p_state, m_p = _mlstm_prompt_call(p_act, gates, mng, batch=batch, seq=seq,
                                                     n_chunks=mlstm_chunks)
        xp = _mix_merge_prompt_call(p_act, yb, xp, gate_p, w_sgu[l], b_sgu[l][:, :, None], wpool, pscale,
                                    wa, wb, wc, wo, fng, seq=seq, tc=tc_p, final=final)
        outs["cp"].append(c_p)
        outs["np"].append(n_p_state)
        outs["mp"].append(m_p[:, :N_HEADS, 0])
        outs["bp"].append(pool_p[:, 1:, :])

        mod_s = mod[l, batch:batch + db].reshape(db, 1, 3 * D_MODEL)
        shift_s, scale_s, gate_s = (mod_s[..., i * D_MODEL:(i + 1) * D_MODEL] for i in range(3))
        p_act, gates, va_f32, p_f32 = _inproj_call(xs, shift_s, scale_s, ng, w_bf, w_gt[l], bif, lng, lnb,
                                                   layer=l, tm=tm_s, rows_per_mod=ts, emit_f32=True)
        w_sgu_tiled = jnp.tile(w_sgu[l][:, :ts, :ts], (1, n_rep, n_rep))
        bs_tiled = jnp.tile(b_sgu[l][:, :ts], (1, n_rep))[:, :, None]
        ya, yc, pool_s = _sgu_pool_sample_call(p_act, p_f32, state_pool, w_sgu_tiled, bs_tiled, wpool, pscale,
                                               layer=l, tb=tb_s, ts=ts)
        m0_tok = jnp.repeat(state_mlstm_m[l], ts, axis=0).T[:, :, None]
        yb, c_s_all, n_s_state, m_s = _mlstm_sample_call(p_act, gates, state_mlstm_C,
                                                         state_mlstm_n[l][:, :, None, :], m0_tok, mng, c_s_all,
                                                         layer=l, ts=ts)
        xs = _merge_call(ya, yb, yc, p_act, xs, gate_s, wa, wb, wc, wo, fng,
                         tm=_row_tile(n_s, 512), rows_per_mod=ts, final=final)
        outs["ns"].append(n_s_state[:, :, 0, :])
        outs["ms"].append(m_s[:, :, 0, 0])
        outs["bs"].append(pool_s)
        outs["vs"].append(va_f32.reshape(db, ts, D_MODEL))

    y_prompt = xp.reshape(batch, seq, D_MODEL)
    y_sample = xs.reshape(db, ts, D_MODEL)
    return (y_prompt, y_sample,
            jnp.stack(outs["cp"]), jnp.stack(outs["np"]), jnp.stack(outs["mp"]), jnp.stack(outs["bp"]),
            c_s_all, jnp.stack(outs["ns"]), jnp.stack(outs["ms"]), jnp.stack(outs["bs"]),
            jnp.stack(outs["vs"]))
```
